```python
import math
import jax, jax.numpy as jnp
from jax import lax
import numpy as np

D_MODEL = 1024
BATCH = 4
SEQ = 4096
DEPTH = 4
DEC_BATCH = 32
DEC_SEQ = 8
PAST_LEN = 8192
PAGE_SIZE = 128

N_META = 16
HEAD_DIM = 64
A_HEADS = 4
A_KV_HEADS = 2
A_GROUP = A_HEADS // A_KV_HEADS
A_VDIM = 2 * HEAD_DIM
A_WIDTH = A_HEADS * A_VDIM
B_HEADS = 8
B_KV_HEADS = 2
B_GROUP = B_HEADS // B_KV_HEADS
B_WIDTH = B_HEADS * HEAD_DIM
IDX_HEADS = 8
IDX_DIM = 64
IDX_SCALE = (IDX_HEADS * IDX_DIM) ** -0.5
TOPK_MAX = 256
Q_BLOCK = 128
MIX_WIDTH = A_WIDTH + B_WIDTH
DEEPNORM_ALPHA = (2 * DEPTH) ** 0.25
DEEPNORM_BETA = (8 * DEPTH) ** -0.25
LN_EPS = 1e-5

COL_SIZES = (
    A_HEADS * 2 * HEAD_DIM,
    A_KV_HEADS * 2 * HEAD_DIM,
    A_KV_HEADS * A_VDIM,
    A_WIDTH,
    B_HEADS * HEAD_DIM,
    B_KV_HEADS * HEAD_DIM,
    B_KV_HEADS * HEAD_DIM,
    IDX_HEADS * IDX_DIM,
    IDX_DIM,
    IDX_HEADS,
    B_WIDTH,
)
D_IN = sum(COL_SIZES)

kernel_name = 'hybrid_diffattn_dsa_deepnorm_step'


def _col_offsets():
    return np.cumsum(np.array(COL_SIZES))[:-1].tolist()


def _layernorm(x, g, b):
    xf = x.astype(jnp.float32)
    mu = jnp.mean(xf, axis=-1, keepdims=True)
    var = jnp.mean(jnp.square(xf - mu), axis=-1, keepdims=True)
    y = (xf - mu) * lax.rsqrt(var + LN_EPS) * g.astype(jnp.float32) + b.astype(jnp.float32)
    return y.astype(x.dtype)


def _head_rmsnorm(o, g):
    of = o.astype(jnp.float32)
    y = of * lax.rsqrt(jnp.mean(of * of, axis=-1, keepdims=True) + LN_EPS) * g.astype(jnp.float32)
    return y.astype(o.dtype)


def _alibi_slopes(n_heads, n_kv):
    m = jnp.exp2(-8.0 * jnp.arange(1, n_heads + 1, dtype=jnp.float32) / n_heads)
    return m.reshape(n_kv, n_heads // n_kv)


def _diff_lambda(layer, lq1, lk1, lq2, lk2):
    lam_init = 0.8 - 0.6 * math.exp(-0.3 * layer)
    lam = (jnp.exp(jnp.sum((lq1 * lk1).astype(jnp.float32)))
           - jnp.exp(jnp.sum((lq2 * lk2).astype(jnp.float32))) + lam_init)
    return lam, lam_init


def _project(x, w):
    bsz, t, _ = x.shape
    p = jnp.einsum('btd,de->bte', x, w)
    aq, ak, av, az, bq, bk, bv, iq, ik, iw, bz = jnp.split(p, _col_offsets(), axis=-1)
    aq = aq.reshape(bsz, t, A_KV_HEADS, A_GROUP, 2, HEAD_DIM)
    ak = ak.reshape(bsz, t, A_KV_HEADS, 2, HEAD_DIM)
    av = av.reshape(bsz, t, A_KV_HEADS, A_VDIM)
    bq = bq.reshape(bsz, t, B_KV_HEADS, B_GROUP, HEAD_DIM)
    bk = bk.reshape(bsz, t, B_KV_HEADS, HEAD_DIM)
    bv = bv.reshape(bsz, t, B_KV_HEADS, HEAD_DIM)
    iq = iq.reshape(bsz, t, IDX_HEADS, IDX_DIM)
    return aq, ak, av, az, bq, bk, bv, iq, ik, iw, bz


def _diff_attention(q, k, v, qpos, kpos, lam, slopes):
    s = jnp.einsum('bqngcd,bsncd->bngcqs', q, k).astype(jnp.float32) * HEAD_DIM ** -0.5
    dist = (qpos[:, None] - kpos[None, :]).astype(jnp.float32)
    bias = -slopes[:, :, None, None, None] * dist
    s = jnp.where(dist >= 0, s + bias, -jnp.inf)
    p = jax.nn.softmax(s, axis=-1)
    a = p[:, :, :, 0] - lam * p[:, :, :, 1]
    return jnp.einsum('bngqs,bsnv->bqngv', a.astype(v.dtype), v)


def _dsa_attention(q, qi, w, qpos, ki, kpos, gather_kv, topk, slopes):
    idx_logits = jnp.einsum('bqhd,bsd->bqhs', qi, ki)
    score = jnp.einsum('bqhs,bqh->bqs', jax.nn.relu(idx_logits), w).astype(jnp.float32) * IDX_SCALE
    score = jnp.where(kpos[None, None, :] <= qpos[None, :, None], score, -jnp.inf)
    _, sel = lax.top_k(score, topk)
    sel_pos = kpos[sel]
    dist = (qpos[None, :, None] - sel_pos).astype(jnp.float32)
    k_sel, v_sel = gather_kv(sel)
    s = jnp.einsum('bqngd,bqknd->bngqk', q, k_sel).astype(jnp.float32) * HEAD_DIM ** -0.5
    bias = -slopes[None, :, :, None, None] * dist[:, None, None]
    s = jnp.where(dist[:, None, None] >= 0, s + bias, -jnp.inf)
    p = jax.nn.softmax(s, axis=-1)
    return jnp.einsum('bngqk,bqknd->bqngd', p.astype(v_sel.dtype), v_sel)


def _query_blocks(fn, qargs, qpos):
    meta_out = fn(*[a[:, :N_META] for a in qargs], qpos[:N_META])
    nblk = (qpos.shape[0] - N_META) // Q_BLOCK

    def to_blocks(a):
        a = a[:, N_META:]
        a = a.reshape((a.shape[0], nblk, Q_BLOCK) + a.shape[2:])
        return jnp.moveaxis(a, 1, 0)

    blk_args = tuple(to_blocks(a) for a in qargs) + (qpos[N_META:].reshape(nblk, Q_BLOCK),)
    out = lax.map(lambda t: fn(*t), blk_args)
    out = jnp.moveaxis(out, 0, 1)
    out = out.reshape((out.shape[0], nblk * Q_BLOCK) + out.shape[3:])
    return jnp.concatenate([meta_out, out], axis=1)


def _merge(x, ya, az, yb, bz, w_out, sub_g, lam_init, ln_g, ln_b):
    bsz, t, _ = x.shape
    ya = _head_rmsnorm(ya.reshape(bsz, t, A_HEADS, A_VDIM), sub_g) * (1.0 - lam_init)
    ya = ya.reshape(bsz, t, A_WIDTH) * jax.nn.silu(az)
    yb = yb.reshape(bsz, t, B_WIDTH) * jax.nn.silu(bz)
    o = jnp.einsum('btm,md->btd', jnp.concatenate([ya, yb], axis=-1), w_out)
    return _layernorm(DEEPNORM_ALPHA * x + o, ln_g, ln_b)


def _prompt_layer(x, layer, w_in, w_out, lq1, lk1, lq2, lk2, sub_g, ln_g, ln_b):
    bsz, t, _ = x.shape
    aq, ak, av, az, bq, bk, bv, iq, ik, iw, bz = _project(x, w_in)
    pos = jnp.arange(t, dtype=jnp.int32)
    lam, lam_init = _diff_lambda(layer, lq1, lk1, lq2, lk2)
    slopes_a = _alibi_slopes(A_HEADS, A_KV_HEADS)
    slopes_b = _alibi_slopes(B_HEADS, B_KV_HEADS)
    topk = min(TOPK_MAX, (t - N_META) // 4)
    bidx = jnp.arange(bsz)[:, None, None]

    def gather(sel):
        return bk[bidx, sel], bv[bidx, sel]

    ya = _query_blocks(lambda q, qp: _diff_attention(q, ak, av, qp, pos, lam, slopes_a), (aq,), pos)
    yb = _query_blocks(lambda q, qi, w, qp: _dsa_attention(q, qi, w, qp, ik, pos, gather, topk, slopes_b),
                       (bq, iq, iw), pos)
    x = _merge(x, ya, az, yb, bz, w_out, sub_g, lam_init, ln_g, ln_b)
    return x, (ak, av, bk, bv, ik)


def _sample_layer(x, layer, cache_diff_k, cache_diff_v, cache_dsa_k, cache_dsa_v, cache_idx_k, page_table,
                  w_in, w_out, lq1, lk1, lq2, lk2, sub_g, ln_g, ln_b):
    bsz, t, _ = x.shape
    aq, ak, av, az, bq, bk, bv, iq, ik, iw, bz = _project(x, w_in)
    past = page_table.shape[1] * PAGE_SIZE
    qpos = past + jnp.arange(t, dtype=jnp.int32)
    kpos = jnp.arange(past + t, dtype=jnp.int32)
    lam, lam_init = _diff_lambda(layer, lq1, lk1, lq2, lk2)
    slopes_a = _alibi_slopes(A_HEADS, A_KV_HEADS)
    slopes_b = _alibi_slopes(B_HEADS, B_KV_HEADS)
    topk = min(TOPK_MAX, (past + t) // 4)
    ka = jnp.concatenate([cache_diff_k[layer, page_table].reshape(bsz, past, A_KV_HEADS, 2, HEAD_DIM), ak], axis=1)
    va = jnp.concatenate([cache_diff_v[layer, page_table].reshape(bsz, past, A_KV_HEADS, A_VDIM), av], axis=1)
    ya = _diff_attention(aq, ka, va, qpos, kpos, lam, slopes_a)
    ki = jnp.concatenate([cache_idx_k[layer, page_table].reshape(bsz, past, IDX_DIM), ik], axis=1)
    bidx = jnp.arange(bsz)[:, None, None]

    def gather(sel):
        p_idx = jnp.minimum(sel, past - 1)
        phys = page_table[bidx, p_idx // PAGE_SIZE]
        off = p_idx % PAGE_SIZE
        n_idx = jnp.maximum(sel - past, 0)
        is_past = (sel < past)[..., None, None]
        k_sel = jnp.where(is_past, cache_dsa_k[layer, phys, off], bk[bidx, n_idx])
        v_sel = jnp.where(is_past, cache_dsa_v[layer, phys, off], bv[bidx, n_idx])
        return k_sel, v_sel

    yb = _dsa_attention(bq, iq, iw, qpos, ki, kpos, gather, topk, slopes_b)
    x = _merge(x, ya, az, yb, bz, w_out, sub_g, lam_init, ln_g, ln_b)
    return x, (ak, av, bk, bv, ik)


def setup_inputs(seed: int = 0) -> dict:
    key = jax.random.key(seed)
    ks = jax.random.split(key, 20)
    nrm = jax.random.normal
    n_pages = PAST_LEN // PAGE_SIZE
    n_pool = (5 * DEC_BATCH * n_pages + 3) // 4
    perm = jax.random.permutation(ks[0], n_pool)
    page_table = perm[:DEC_BATCH * n_pages].reshape(DEC_BATCH, n_pages).astype(jnp.int32)
    offs = [0] + _col_offsets() + [D_IN]
    col_scale = np.ones((D_IN,), np.float32)
    col_scale[offs[2]:offs[3]] = DEEPNORM_BETA
    col_scale[offs[6]:offs[7]] = DEEPNORM_BETA
    return {
        'x_prompt': nrm(ks[1], (BATCH, SEQ, D_MODEL), jnp.float32),
        'x_sample': nrm(ks[2], (DEC_BATCH, DEC_SEQ, D_MODEL), jnp.float32),
        'cache_diff_k': nrm(ks[3], (DEPTH, n_pool, PAGE_SIZE, A_KV_HEADS, 2, HEAD_DIM), jnp.float32),
        'cache_diff_v': nrm(ks[4], (DEPTH, n_pool, PAGE_SIZE, A_KV_HEADS, A_VDIM), jnp.float32) * DEEPNORM_BETA,
        'cache_dsa_k': nrm(ks[5], (DEPTH, n_pool, PAGE_SIZE, B_KV_HEADS, HEAD_DIM), jnp.float32),
        'cache_dsa_v': nrm(ks[6], (DEPTH, n_pool, PAGE_SIZE, B_KV_HEADS, HEAD_DIM), jnp.float32) * DEEPNORM_BETA,
        'cache_idx_k': nrm(ks[7], (DEPTH, n_pool, PAGE_SIZE, IDX_DIM), jnp.float32),
        'page_table': page_table,
        'meta_tokens': nrm(ks[8], (N_META, D_MODEL), jnp.float32),
        'ln_in_g': 1.0 + 0.02 * nrm(ks[9], (D_MODEL,), jnp.float32),
        'ln_in_b': 0.02 * nrm(ks[10], (D_MODEL,), jnp.float32),
        'w_in': nrm(ks[11], (DEPTH, D_MODEL, D_IN), jnp.float32) * (D_MODEL ** -0.5) * jnp.asarray(col_scale),
        'w_out': nrm(ks[12], (DEPTH, MIX_WIDTH, D_MODEL), jnp.float32) * (MIX_WIDTH ** -0.5) * DEEPNORM_BETA,
        'lambda_q1': 0.1 * nrm(ks[13], (DEPTH, HEAD_DIM), jnp.float32),
        'lambda_k1': 0.1 * nrm(ks[14], (DEPTH, HEAD_DIM), jnp.float32),
        'lambda_q2': 0.1 * nrm(ks[15], (DEPTH, HEAD_DIM), jnp.float32),
        'lambda_k2': 0.1 * nrm(ks[16], (DEPTH, HEAD_DIM), jnp.float32),
        'subln_g': 1.0 + 0.02 * nrm(ks[17], (DEPTH, A_VDIM), jnp.float32),
        'ln_g': 1.0 + 0.02 * nrm(ks[18], (DEPTH, D_MODEL), jnp.float32),
        'ln_b': 0.02 * nrm(ks[19], (DEPTH, D_MODEL), jnp.float32),
    }


def reference(x_prompt, x_sample, cache_diff_k, cache_diff_v, cache_dsa_k, cache_dsa_v, cache_idx_k, page_table,
              meta_tokens, ln_in_g, ln_in_b, w_in, w_out, lambda_q1, lambda_k1, lambda_q2, lambda_k2,
              subln_g, ln_g, ln_b):
    bsz = x_prompt.shape[0]
    meta = jnp.broadcast_to(meta_tokens[None].astype(x_prompt.dtype), (bsz, N_META, meta_tokens.shape[-1]))
    xp = _layernorm(jnp.concatenate([meta, x_prompt], axis=1), ln_in_g, ln_in_b)
    xs = _layernorm(x_sample, ln_in_g, ln_in_b)
    p_rows = [[], [], [], [], []]
    s_rows = [[], [], [], [], []]
    for layer in range(DEPTH):
        lw = (w_in[layer], w_out[layer], lambda_q1[layer], lambda_k1[layer], lambda_q2[layer],
              lambda_k2[layer], subln_g[layer], ln_g[layer], ln_b[layer])
        xp, p_new = _prompt_layer(xp, layer, *lw)
        xs, s_new = _sample_layer(xs, layer, cache_diff_k, cache_diff_v, cache_dsa_k, cache_dsa_v,
                                  cache_idx_k, page_table, *lw)
        for lst, a in zip(p_rows, p_new):
            lst.append(a)
        for lst, a in zip(s_rows, s_new):
            lst.append(a)
    p_diff_k, p_diff_v, p_dsa_k, p_dsa_v, p_idx_k = [jnp.stack(r) for r in p_rows]
    s_diff_k, s_diff_v, s_dsa_k, s_dsa_v, s_idx_k = [jnp.stack(r) for r in s_rows]
    return (xp[:, N_META:], xs, p_diff_k, p_diff_v, p_dsa_k, p_dsa_v, p_idx_k,
            s_diff_k, s_diff_v, s_dsa_k, s_dsa_v, s_idx_k)
```

```python
import functools
import math

import jax
import jax.numpy as jnp
from jax import lax
from jax.experimental import pallas as pl
from jax.experimental.pallas import tpu as pltpu

N_META = 16
HEAD_DIM = 64
A_HEADS = 4
A_KV_HEADS = 2
B_HEADS = 8
B_KV_HEADS = 2
IDX_HEADS = 8
IDX_DIM = 64
IDX_SCALE = (IDX_HEADS * IDX_DIM) ** -0.5
TOPK_MAX = 256
PAGE_SIZE = 128
LN_EPS = 1e-5

Q_TILE = 128
KV_CHUNK = 256
PAGES_PER_STEP = 8
VMEM_LIMIT_BYTES = 56 * 1024 * 1024

INT_MIN = -(2 ** 31)
NEG_INF = float("-inf")

C_AQ, C_BQ, C_IQ, C_AK, C_AV, C_BK, C_BV, C_IK = 0, 512, 1024, 1536, 1792, 2048, 2176, 2304
W16 = 2432
C_AZ, C_BZ, C_IW = 0, 512, 1024
W32 = 1152
NCOL = W16 + W32

F32 = jnp.float32
BF16 = jnp.bfloat16
I32 = jnp.int32


def _dot_nt(a, b):
    return lax.dot_general(a, b, (((1,), (1,)), ((), ())), preferred_element_type=F32)


def _cparams(sem):
    return pltpu.CompilerParams(dimension_semantics=sem, vmem_limit_bytes=VMEM_LIMIT_BYTES)


def _sort_key(score):
    bits = lax.bitcast_convert_type(score, I32)
    return bits ^ ((bits >> 31) & 0x7FFFFFFF)


def _lane_lo(width=128):
    return lax.broadcasted_iota(I32, (1, width), 1) < 64


def _ln_kernel(x_ref, g_ref, b_ref, o_ref):
    x = x_ref[...]
    mu = jnp.mean(x, axis=-1, keepdims=True)
    xc = x - mu
    var = jnp.mean(xc * xc, axis=-1, keepdims=True)
    o_ref[...] = xc * lax.rsqrt(var + LN_EPS) * g_ref[...] + b_ref[...]


def _layernorm_rows(x, g, b, tm):
    rows, d = x.shape
    return pl.pallas_call(
        _ln_kernel,
        grid=(rows // tm,),
        in_specs=[pl.BlockSpec((tm, d), lambda i: (i, 0)),
                  pl.BlockSpec((1, d), lambda i: (0, 0)),
                  pl.BlockSpec((1, d), lambda i: (0, 0))],
        out_specs=pl.BlockSpec((tm, d), lambda i: (i, 0)),
        out_shape=jax.ShapeDtypeStruct((rows, d), F32),
        compiler_params=_cparams(("arbitrary",)),
        name="ln_in",
    )(x, g.reshape(1, d), b.reshape(1, d))


def _proj_kernel(x_ref, w_ref, p16_ref, p32_ref, ak_ref, av_ref, bk_ref, bv_ref, ik_ref):
    xb = x_ref[...].astype(BF16)
    res = jnp.dot(xb, w_ref[...], preferred_element_type=F32)
    p16_ref[...] = res[:, :W16].astype(p16_ref.dtype)
    p32_ref[...] = res[:, W16:]
    ak_ref[...] = res[:, C_AK:C_AK + 256]
    av_ref[...] = res[:, C_AV:C_AV + 256]
    bk_ref[...] = res[:, C_BK:C_BK + 128]
    bv_ref[...] = res[:, C_BV:C_BV + 128]
    ik_ref[...] = res[:, C_IK:C_IK + 64]


def _project(x, w, layer, t_out, tm, p16_dtype, name):
    nb, t, d = x.shape
    row = lambda wd: pl.BlockSpec((None, tm, wd), lambda b, i: (b, i, 0))
    shp = lambda wd: jax.ShapeDtypeStruct((nb, t_out, wd), F32)
    return pl.pallas_call(
        _proj_kernel,
        grid=(nb, t // tm),
        in_specs=[row(d), pl.BlockSpec((None, d, NCOL), lambda b, i: (layer, 0, 0))],
        out_specs=[row(W16), row(W32), row(256), row(256), row(128), row(128), row(64)],
        out_shape=[jax.ShapeDtypeStruct((nb, t, W16), p16_dtype),
                   jax.ShapeDtypeStruct((nb, t, W32), F32),
                   shp(256), shp(256), shp(128), shp(128), shp(64)],
        compiler_params=_cparams(("arbitrary", "arbitrary")),
        name=name,
    )(x, w)


def _diff_lambda(lam_ref):
    r = lam_ref[...]
    e1 = jnp.exp(jnp.sum(r[0:1] * r[1:2], axis=1, keepdims=True))
    e2 = jnp.exp(jnp.sum(r[2:3] * r[3:4], axis=1, keepdims=True))
    return e1 - e2 + r[4:5, 0:1]


def _softmax_step(s, cj, v, m, l, acc):
    mt = jnp.max(s, axis=1, keepdims=True)
    m_new = jnp.maximum(m, mt + cj)
    m_safe = jnp.where(m_new == NEG_INF, 0.0, m_new)
    p = jnp.exp(s - (m_safe - cj))
    alpha = jnp.exp(m - m_safe)
    l = alpha * l + jnp.sum(p, axis=1, keepdims=True)
    acc = alpha * acc + jnp.dot(p.astype(BF16), v, preferred_element_type=F32)
    return m_new, l, acc


def _half_select(x, keep_low):
    lo = _lane_lo()
    return jnp.where(lo, x, 0.0) if keep_low else jnp.where(lo, 0.0, x)


def _diff_query_rows(q, n_rows):
    pieces = [q[:, 0:128], q[:, 128:256]]
    return jnp.concatenate([_half_select(p, True) for p in pieces] +
                           [_half_select(p, False) for p in pieces], axis=0).astype(BF16)


def _dsa_query_rows(q):
    out = []
    for h in range(B_HEADS):
        n = h // (B_HEADS // B_KV_HEADS)
        p = q[:, 128 * (h // 2):128 * (h // 2) + 128]
        if h % 2 != n:
            p = pltpu.roll(p, 64, 1)
        out.append(_half_select(p, n == 0))
    return jnp.concatenate(out, axis=0).astype(BF16)


def _dsa_merge_heads(o, r):
    outs = []
    lo = _lane_lo()
    for j in range(B_HEADS // 2):
        n = j // 2
        oe, oo = o[2 * j * r:(2 * j + 1) * r], o[(2 * j + 1) * r:(2 * j + 2) * r]
        if n == 0:
            outs.append(jnp.where(lo, oe, pltpu.roll(oo, 64, 1)))
        else:
            outs.append(jnp.where(lo, pltpu.roll(oe, 64, 1), oo))
    return jnp.concatenate(outs, axis=1)


def _const_rows(values, r):
    return jnp.concatenate([jnp.full((r, 1), v, F32) for v in values], axis=0)


def _count(mask):
    return jnp.sum(jnp.where(mask, 1.0, 0.0), axis=1, keepdims=True)


def _diff_prompt_kernel(lam_ref, q_ref, k_ref, v_ref, o_ref, *, n_real_tiles):
    n = pl.program_id(1)
    i = pl.program_id(2)
    tq, tc = Q_TILE, KV_CHUNK

    @pl.when(i >= n_real_tiles)
    def _():
        o_ref[...] = jnp.zeros_like(o_ref)

    @pl.when(i < n_real_tiles)
    def _():
        lam = _diff_lambda(lam_ref)
        qst = _diff_query_rows(q_ref[...].astype(F32), tq)
        row = lax.broadcasted_iota(I32, (4 * tq, 1), 0)
        s_g0 = jnp.where(n == 0, 2.0 ** -2, 2.0 ** -6)
        s_g1 = jnp.where(n == 0, 2.0 ** -4, 2.0 ** -8)
        slope = jnp.where(((row >> 7) & 1) == 0, s_g0, s_g1).astype(F32)
        qpos = i * tq + (row & (tq - 1))
        col = lax.broadcasted_iota(I32, (1, tc), 1)
        bias = slope * col.astype(F32)

        def step(j, carry, masked):
            m, l, acc = carry
            k0 = pl.multiple_of(j * tc, tc)
            k = k_ref[pl.ds(k0, tc), :]
            v = v_ref[pl.ds(k0, tc), :]
            s = _dot_nt(qst, k) + bias
            if masked:
                s = jnp.where(col + k0 <= qpos, s, NEG_INF)
            cj = slope * k0.astype(F32)
            return _softmax_step(s, cj, v, m, l, acc)

        init = (jnp.full((4 * tq, 1), NEG_INF, F32), jnp.zeros((4 * tq, 1), F32),
                jnp.zeros((4 * tq, 128), F32))
        n_full = (i * tq) // tc
        carry = lax.fori_loop(0, n_full, lambda j, c: step(j, c, False), init)
        _, l, acc = step(n_full, carry, True)
        o = acc / l
        out_g0 = o[0:tq] - lam * o[2 * tq:3 * tq]
        out_g1 = o[tq:2 * tq] - lam * o[3 * tq:4 * tq]
        o_ref[...] = jnp.concatenate([out_g0, out_g1], axis=1)


def _diff_prompt(p16, lam, t_real):
    nb, tpad, _ = p16.shape
    nq = tpad // Q_TILE
    kern = functools.partial(_diff_prompt_kernel, n_real_tiles=pl.cdiv(t_real, Q_TILE))
    return pl.pallas_call(
        kern,
        grid=(nb, A_KV_HEADS, nq),
        in_specs=[pl.BlockSpec((8, 128), lambda b, n, i: (0, 0)),
                  pl.BlockSpec((None, Q_TILE, 256), lambda b, n, i: (b, i, C_AQ // 256 + n)),
                  pl.BlockSpec((None, tpad, 128), lambda b, n, i: (b, 0, C_AK // 128 + n)),
                  pl.BlockSpec((None, tpad, 128), lambda b, n, i: (b, 0, C_AV // 128 + n))],
        out_specs=pl.BlockSpec((None, Q_TILE, 256), lambda b, n, i: (b, i, n)),
        out_shape=jax.ShapeDtypeStruct((nb, tpad, 512), F32),
        compiler_params=_cparams(("arbitrary", "arbitrary", "arbitrary")),
        name="diff_prompt",
    )(lam, p16, p16, p16)


def _dsa_prompt_kernel(bq_ref, iq_ref, iw_ref, bk_ref, bv_ref, ik_ref, o_ref, keys_ref, *,
                       n_real_tiles, topk, idx_bits):
    i = pl.program_id(1)
    tq, tc = Q_TILE, KV_CHUNK

    @pl.when(i >= n_real_tiles)
    def _():
        o_ref[...] = jnp.zeros_like(o_ref)

    @pl.when(i < n_real_tiles)
    def _():
        n_chunks = (i * tq) // tc + 1
        col = lax.broadcasted_iota(I32, (1, tc), 1)
        qpos = i * tq + lax.broadcasted_iota(I32, (tq, 1), 0)

        iq = iq_ref[...].astype(F32)
        iqst = jnp.concatenate(
            [_half_select(iq[:, 128 * (h // 2):128 * (h // 2) + 128], h % 2 == 0)
             for h in range(IDX_HEADS)], axis=0).astype(BF16)
        w = iw_ref[...]
        wcol = [w[:, h:h + 1] for h in range(IDX_HEADS)]

        def index_step(j, _):
            k0 = pl.multiple_of(j * tc, tc)
            logits = _dot_nt(iqst, ik_ref[pl.ds(k0, tc), :])
            sc = jnp.maximum(logits[0:tq], 0.0) * wcol[0]
            for h in range(1, IDX_HEADS):
                sc = sc + jnp.maximum(logits[h * tq:(h + 1) * tq], 0.0) * wcol[h]
            key = _sort_key(sc * IDX_SCALE)
            keys_ref[j] = jnp.where(col + k0 <= qpos, key, INT_MIN)
            return 0

        lax.fori_loop(0, n_chunks, index_step, 0)

        def count_over_chunks(pred):
            def body(j, acc):
                return acc + jnp.where(pred(keys_ref[j], col + j * tc), 1.0, 0.0)
            acc = lax.fori_loop(0, n_chunks, body, jnp.zeros((tq, tc), F32))
            return jnp.sum(acc, axis=1, keepdims=True)

        def bit_step(b, t):
            cand = t + lax.shift_left(jnp.int32(1), 31 - b)
            cnt = count_over_chunks(lambda kc, _: kc >= cand)
            return jnp.where(cnt >= topk, cand, t)

        t = lax.fori_loop(0, 32, bit_step, jnp.full((tq, 1), INT_MIN, I32))
        thr = jnp.maximum(t, INT_MIN + 1)
        need = topk - count_over_chunks(lambda kc, _: kc > thr)
        n_eq = count_over_chunks(lambda kc, _: kc == thr)
        excess = n_eq > need

        def tie_bound():
            def tie_step(b, jb):
                cand = jb + lax.shift_left(jnp.int32(1), idx_bits - 1 - b)
                cnt = count_over_chunks(lambda kc, idx: (kc == thr) & (idx < cand))
                return jnp.where(cnt < need, cand, jb)
            return lax.fori_loop(0, idx_bits, tie_step, jnp.zeros((tq, 1), I32))

        any_excess = jnp.max(jnp.where(excess, 1.0, 0.0)) > 0.0
        big = jnp.full((tq, 1), 2 ** 30, I32)
        bound = lax.cond(any_excess, lambda: jnp.where(excess, tie_bound(), big), lambda: big)

        qst = _dsa_query_rows(bq_ref[...].astype(F32))
        slope = _const_rows([2.0 ** -(h + 1) for h in range(B_HEADS)], tq)
        bias = slope * col.astype(F32)

        def attn_step(j, carry):
            m, l, acc = carry
            k0 = pl.multiple_of(j * tc, tc)
            kc = keys_ref[j]
            sel = (kc > thr) | ((kc == thr) & (col + k0 <= bound))
            s = (_dot_nt(qst, bk_ref[pl.ds(k0, tc), :]) + bias).reshape(B_HEADS, tq, tc)
            s = jnp.where(sel[None], s, NEG_INF).reshape(B_HEADS * tq, tc)
            cj = slope * k0.astype(F32)
            return _softmax_step(s, cj, bv_ref[pl.ds(k0, tc), :], m, l, acc)

        init = (jnp.full((B_HEADS * tq, 1), NEG_INF, F32), jnp.zeros((B_HEADS * tq, 1), F32),
                jnp.zeros((B_HEADS * tq, 128), F32))
        _, l, acc = lax.fori_loop(0, n_chunks, attn_step, init)
        o_ref[...] = _dsa_merge_heads(acc / l, tq)


def _dsa_prompt(p16, p32, t_real, topk):
    nb, tpad, _ = p16.shape
    nq = tpad // Q_TILE
    kern = functools.partial(_dsa_prompt_kernel, n_real_tiles=pl.cdiv(t_real, Q_TILE), topk=float(topk),
                             idx_bits=max(1, int(tpad).bit_length()))
    whole = lambda c: pl.BlockSpec((None, tpad, 128), lambda b, i: (b, 0, c // 128))
    return pl.pallas_call(
        kern,
        grid=(nb, nq),
        in_specs=[pl.BlockSpec((None, Q_TILE, 512), lambda b, i: (b, i, C_BQ // 512)),
                  pl.BlockSpec((None, Q_TILE, 512), lambda b, i: (b, i, C_IQ // 512)),
                  pl.BlockSpec((None, Q_TILE, 128), lambda b, i: (b, i, C_IW // 128)),
                  whole(C_BK), whole(C_BV), whole(C_IK)],
        out_specs=pl.BlockSpec((None, Q_TILE, 512), lambda b, i: (b, i, 0)),
        out_shape=jax.ShapeDtypeStruct((nb, tpad, 512), F32),
        scratch_shapes=[pltpu.VMEM((tpad // KV_CHUNK, Q_TILE, KV_CHUNK), I32)],
        compiler_params=_cparams(("arbitrary", "arbitrary")),
        name="dsa_prompt",
    )(p16, p16, p32, p16, p16, p16)


def _merge_kernel(sc_ref, x_ref, ya_ref, yb_ref, az_ref, bz_ref, w_ref, sg_ref, g_ref, b_ref, o_ref, *, alpha):
    one_minus_lam_init = sc_ref[5:6, 0:1]
    ya = ya_ref[...]
    parts = []
    for h in range(A_HEADS):
        o = ya[:, 128 * h:128 * h + 128]
        y = o * lax.rsqrt(jnp.mean(o * o, axis=-1, keepdims=True) + LN_EPS) * sg_ref[...]
        parts.append(y * one_minus_lam_init)
    az = az_ref[...]
    bz = bz_ref[...]
    ya_g = jnp.concatenate(parts, axis=1) * (az * (1.0 / (1.0 + jnp.exp(-az))))
    yb_g = yb_ref[...] * (bz * (1.0 / (1.0 + jnp.exp(-bz))))
    mix = jnp.concatenate([ya_g, yb_g], axis=1).astype(BF16)
    y = alpha * x_ref[...] + jnp.dot(mix, w_ref[...], preferred_element_type=F32)
    mu = jnp.mean(y, axis=-1, keepdims=True)
    yc = y - mu
    var = jnp.mean(yc * yc, axis=-1, keepdims=True)
    o_ref[...] = yc * lax.rsqrt(var + LN_EPS) * g_ref[...] + b_ref[...]


def _merge(x, ya, yb, p32, w_out, layer, sc, sub_g, ln_g, ln_b, alpha, tm, name):
    nb, t, d = x.shape
    row = lambda wd, c: pl.BlockSpec((None, tm, wd), lambda b, i: (b, i, c))
    vec = lambda wd: pl.BlockSpec((1, wd), lambda b, i: (0, 0))
    return pl.pallas_call(
        functools.partial(_merge_kernel, alpha=alpha),
        grid=(nb, t // tm),
        in_specs=[pl.BlockSpec((8, 128), lambda b, i: (0, 0)),
                  row(d, 0), row(512, 0), row(512, 0), row(512, C_AZ // 512), row(512, C_BZ // 512),
                  pl.BlockSpec((None, d, d), lambda b, i: (layer, 0, 0)),
                  vec(128), vec(d), vec(d)],
        out_specs=row(d, 0),
        out_shape=jax.ShapeDtypeStruct((nb, t, d), F32),
        compiler_params=_cparams(("arbitrary", "arbitrary")),
        name=name,
    )(sc, x, ya, yb, p32, p32, w_out, sub_g.reshape(1, 128), ln_g.reshape(1, d), ln_b.reshape(1, d))


def _pad_rows(x, rows):
    return jnp.concatenate([x, jnp.zeros((rows - x.shape[0], x.shape[1]), x.dtype)], axis=0)


def _diff_sample_kernel(pt_ref, lam_ref, q_ref, kn_ref, vn_ref, *rest, n_pg, past):
    k_refs, v_refs = rest[:n_pg], rest[n_pg:2 * n_pg]
    o_ref, m_ref, l_ref, acc_ref = rest[2 * n_pg:]
    pg = pl.program_id(1)
    ds = q_ref.shape[0]
    rows = 2 * A_KV_HEADS * 2 * ds
    span = n_pg * PAGE_SIZE

    @pl.when(pg == 0)
    def _():
        m_ref[...] = jnp.full_like(m_ref, NEG_INF)
        l_ref[...] = jnp.zeros_like(l_ref)
        acc_ref[...] = jnp.zeros_like(acc_ref)

    q = q_ref[...]
    qst = jnp.concatenate([_diff_query_rows(q[:, 256 * n:256 * n + 256], ds) for n in range(A_KV_HEADS)], axis=0)
    slope = _const_rows([2.0 ** (-2 * (2 * n + g + 1)) for n in range(A_KV_HEADS) for _c in range(2)
                         for g in range(2)], ds)
    half = rows // A_KV_HEADS

    def attend(score, weigh, width, k0, mask):
        col = lax.broadcasted_iota(I32, (1, width), 1)
        s = jnp.concatenate([score(n, qst[half * n:half * (n + 1)]) for n in range(A_KV_HEADS)],
                            axis=0) + slope * col.astype(F32)
        if mask is not None:
            s = jnp.where(mask(col), s, NEG_INF)
        cj = slope * k0
        m, l, acc = m_ref[...][:, 0:1], l_ref[...][:, 0:1], acc_ref[...]
        mt = jnp.max(s, axis=1, keepdims=True)
        m_new = jnp.maximum(m, mt + cj)
        p = jnp.exp(s - (m_new - cj))
        alpha = jnp.exp(m - m_new)
        l = alpha * l + jnp.sum(p, axis=1, keepdims=True)
        pv = jnp.concatenate([weigh(n, p[half * n:half * (n + 1)].astype(BF16)) for n in range(A_KV_HEADS)],
                             axis=0)
        m_ref[...] = jnp.broadcast_to(m_new, m_ref.shape)
        l_ref[...] = jnp.broadcast_to(l, l_ref.shape)
        acc_ref[...] = alpha * acc + pv

    def past_score(n, q):
        kt = jnp.concatenate([r[128 * n:128 * n + 128, :] for r in k_refs], axis=1).astype(BF16)
        return jnp.dot(q, kt, preferred_element_type=F32)

    def past_weigh(n, p):
        v = jnp.concatenate([r[pl.ds(n, PAGE_SIZE, stride=A_KV_HEADS), :] for r in v_refs], axis=0).astype(BF16)
        return jnp.dot(p, v, preferred_element_type=F32)

    attend(past_score, past_weigh, span, (pg * span).astype(F32), None)

    @pl.when(pg == past // span - 1)
    def _():
        r = lax.broadcasted_iota(I32, (rows, 1), 0) & (ds - 1)
        kn = _pad_rows(kn_ref[...], 128).astype(BF16)
        vn = _pad_rows(vn_ref[...], 128).astype(BF16)
        attend(lambda n, q: _dot_nt(q, kn[:, 128 * n:128 * n + 128]),
               lambda n, p: jnp.dot(p, vn[:, 128 * n:128 * n + 128], preferred_element_type=F32),
               128, jnp.float32(past), lambda col: col <= r)
        lam = _diff_lambda(lam_ref)
        o = acc_ref[...] / l_ref[...][:, 0:1]
        outs = []
        for n in range(A_KV_HEADS):
            base = half * n
            for g in range(2):
                outs.append(o[base + g * ds:base + (g + 1) * ds]
                            - lam * o[base + (2 + g) * ds:base + (3 + g) * ds])
        o_ref[...] = jnp.concatenate(outs, axis=1)


def _diff_sample(p16s, lam, cache_k, cache_v, pt_flat, layer, n_pages):
    db, ds, _ = p16s.shape
    n_pg = PAGES_PER_STEP
    rows = 2 * A_KV_HEADS * 2 * ds
    past = n_pages * PAGE_SIZE
    kern = functools.partial(_diff_sample_kernel, n_pg=n_pg, past=past)

    def page(r):
        return pl.BlockSpec((None, None, 256, PAGE_SIZE),
                            lambda sb, pg, pt: (layer, pt[sb * n_pages + pg * n_pg + r], 0, 0))

    new = lambda c: pl.BlockSpec((None, ds, 256), lambda sb, pg, pt: (sb, 0, c // 256))
    grid_spec = pltpu.PrefetchScalarGridSpec(
        num_scalar_prefetch=1,
        grid=(db, n_pages // n_pg),
        in_specs=[pl.BlockSpec((8, 128), lambda sb, pg, pt: (0, 0)),
                  pl.BlockSpec((None, ds, 512), lambda sb, pg, pt: (sb, 0, C_AQ // 512)),
                  new(C_AK), new(C_AV)] + [page(r) for r in range(n_pg)] * 2,
        out_specs=pl.BlockSpec((None, ds, 512), lambda sb, pg, pt: (sb, 0, 0)),
        scratch_shapes=[pltpu.VMEM((rows, 128), F32), pltpu.VMEM((rows, 128), F32),
                        pltpu.VMEM((rows, 128), F32)])
    return pl.pallas_call(
        kern, grid_spec=grid_spec,
        out_shape=jax.ShapeDtypeStruct((db, ds, 512), F32),
        compiler_params=_cparams(("arbitrary", "arbitrary")),
        name="diff_sample",
    )(pt_flat, lam, p16s, p16s, p16s, *([cache_k] * n_pg), *([cache_v] * n_pg))


def _dsa_sample_kernel(pt_ref, bq_ref, iq_ref, iw_ref, kn_ref, vn_ref, in_ref, *rest, n_pg, past, topk, idx_bits):
    i_refs, k_refs, v_refs = rest[:n_pg], rest[n_pg:2 * n_pg], rest[2 * n_pg:3 * n_pg]
    o_ref, keys_ref, keys_new_ref, thr_ref, bound_ref, m_ref, l_ref, acc_ref = rest[3 * n_pg:]
    pg = pl.program_id(1)
    ds = bq_ref.shape[0]
    rows = B_HEADS * ds
    span = n_pg * PAGE_SIZE
    n_groups = past // span
    r_id = lax.broadcasted_iota(I32, (ds, 1), 0)

    def index_keys(ik, transposed):
        iq = iq_ref[...]
        heads = []
        for h in range(IDX_HEADS):
            p = iq[:, 128 * (h // 2):128 * (h // 2) + 128]
            if h % 2:
                p = pltpu.roll(p, 64, 1)
            heads.append(p[:, 0:64])
        iqst = jnp.concatenate(heads, axis=0).astype(BF16)
        if transposed:
            logits = jnp.dot(iqst, ik.astype(BF16), preferred_element_type=F32)
        else:
            logits = _dot_nt(iqst, ik.astype(BF16))
        w = iw_ref[...]
        sc = jnp.maximum(logits[0:ds], 0.0) * w[:, 0:1]
        for h in range(1, IDX_HEADS):
            sc = sc + jnp.maximum(logits[h * ds:(h + 1) * ds], 0.0) * w[:, h:h + 1]
        return _sort_key(sc * IDX_SCALE)

    @pl.when(pg < n_groups)
    def _():
        ik = jnp.concatenate([r[...] for r in i_refs], axis=1)
        keys_ref[pg] = index_keys(ik, True)

    @pl.when(pg == n_groups - 1)
    def _():
        col = lax.broadcasted_iota(I32, (1, 128), 1)
        key_new = index_keys(_pad_rows(in_ref[...][:, 0:64], 128), False)
        keys_new = jnp.where(col <= r_id, key_new, INT_MIN)
        keys_new_ref[...] = keys_new
        keys = keys_ref[...]
        idx = (lax.broadcasted_iota(I32, (n_groups, 1, span), 0) * span
               + lax.broadcasted_iota(I32, (n_groups, 1, span), 2))
        idx_new = col + past

        def count(pred):
            hit = jnp.sum(jnp.where(pred(keys, idx), 1.0, 0.0), axis=0)
            return (jnp.sum(hit, axis=1, keepdims=True) + _count(pred(keys_new, idx_new)))

        def bit_step(b, t):
            cand = t + lax.shift_left(jnp.int32(1), 31 - b)
            return jnp.where(count(lambda kc, _: kc >= cand) >= topk, cand, t)

        t = lax.fori_loop(0, 32, bit_step, jnp.full((ds, 1), INT_MIN, I32))
        thr = jnp.maximum(t, INT_MIN + 1)
        need = topk - count(lambda kc, _: kc > thr)
        excess = count(lambda kc, _: kc == thr) > need

        def tie_step(b, jb):
            cand = jb + lax.shift_left(jnp.int32(1), idx_bits - 1 - b)
            return jnp.where(count(lambda kc, ix: (kc == thr) & (ix < cand)) < need, cand, jb)

        jb = lax.fori_loop(0, idx_bits, tie_step, jnp.zeros((ds, 1), I32))
        thr_ref[...] = jnp.broadcast_to(thr, thr_ref.shape)
        bound_ref[...] = jnp.broadcast_to(jnp.where(excess, jb, 2 ** 30), bound_ref.shape)

    qst = _dsa_query_rows(bq_ref[...])
    slope = _const_rows([2.0 ** -(h + 1) for h in range(B_HEADS)], ds)

    def attend(qk, weigh, kc, k0):
        width = qk.shape[1]
        col = lax.broadcasted_iota(I32, (1, width), 1)
        thr, bound = thr_ref[...][:, 0:1], bound_ref[...][:, 0:1]
        sel = (kc > thr) | ((kc == thr) & (col + k0 <= bound))
        s = (qk + slope * col.astype(F32)).reshape(B_HEADS, ds, width)
        s = jnp.where(sel[None], s, NEG_INF).reshape(rows, width)
        cj = slope * k0.astype(F32)
        m, l, acc = m_ref[...][:, 0:1], l_ref[...][:, 0:1], acc_ref[...]
        mt = jnp.max(s, axis=1, keepdims=True)
        m_new = jnp.maximum(m, mt + cj)
        m_safe = jnp.where(m_new == NEG_INF, 0.0, m_new)
        p = jnp.exp(s - (m_safe - cj))
        alpha = jnp.exp(m - m_safe)
        m_ref[...] = jnp.broadcast_to(m_new, m_ref.shape)
        l_ref[...] = jnp.broadcast_to(alpha * l + jnp.sum(p, axis=1, keepdims=True), l_ref.shape)
        acc_ref[...] = alpha * acc + weigh(p.astype(BF16))

    @pl.when(pg == n_groups)
    def _():
        m_ref[...] = jnp.full_like(m_ref, NEG_INF)
        l_ref[...] = jnp.zeros_like(l_ref)
        acc_ref[...] = jnp.zeros_like(acc_ref)

    @pl.when(pg >= n_groups)
    def _():
        kt = jnp.concatenate([r[...] for r in k_refs], axis=1).astype(BF16)
        vt = jnp.concatenate([r[...] for r in v_refs], axis=1).astype(BF16)
        attend(jnp.dot(qst, kt, preferred_element_type=F32), lambda p: _dot_nt(p, vt),
               keys_ref[pg - n_groups], (pg - n_groups) * span)

    @pl.when(pg == 2 * n_groups - 1)
    def _():
        kn = _pad_rows(kn_ref[...], 128).astype(BF16)
        vn = _pad_rows(vn_ref[...], 128).astype(BF16)
        attend(_dot_nt(qst, kn), lambda p: jnp.dot(p, vn, preferred_element_type=F32),
               keys_new_ref[...], jnp.int32(past))
        o_ref[...] = _dsa_merge_heads(acc_ref[...] / l_ref[...][:, 0:1], ds)


def _dsa_sample(p16s, p32s, cache_i, cache_k, cache_v, pt_flat, layer, n_pages, topk):
    db, ds, _ = p16s.shape
    n_pg = PAGES_PER_STEP
    n_groups = n_pages // n_pg
    past = n_pages * PAGE_SIZE
    rows = B_HEADS * ds
    kern = functools.partial(_dsa_sample_kernel, n_pg=n_pg, past=past, topk=float(topk),
                             idx_bits=int(past + 128).bit_length())

    def page(r, feat, first_phase):
        def index(sb, pg, pt):
            grp = jnp.minimum(pg, n_groups - 1) if first_phase else jnp.maximum(pg - n_groups, 0)
            return (layer, pt[sb * n_pages + grp * n_pg + r], 0, 0)
        return pl.BlockSpec((None, None, feat, PAGE_SIZE), index)

    new = lambda c: pl.BlockSpec((None, ds, 128), lambda sb, pg, pt: (sb, 0, c // 128))
    grid_spec = pltpu.PrefetchScalarGridSpec(
        num_scalar_prefetch=1,
        grid=(db, 2 * n_groups),
        in_specs=[pl.BlockSpec((None, ds, 512), lambda sb, pg, pt: (sb, 0, C_BQ // 512)),
                  pl.BlockSpec((None, ds, 512), lambda sb, pg, pt: (sb, 0, C_IQ // 512)),
                  pl.BlockSpec((None, ds, 128), lambda sb, pg, pt: (sb, 0, C_IW // 128)),
                  new(C_BK), new(C_BV), new(C_IK)]
                 + [page(r, 64, True) for r in range(n_pg)]
                 + [page(r, 128, False) for r in range(n_pg)] * 2,
        out_specs=pl.BlockSpec((None, ds, 512), lambda sb, pg, pt: (sb, 0, 0)),
        scratch_shapes=[pltpu.VMEM((n_groups, ds, n_pg * PAGE_SIZE), I32), pltpu.VMEM((ds, 128), I32),
                        pltpu.VMEM((ds, 128), I32), pltpu.VMEM((ds, 128), I32), pltpu.VMEM((rows, 128), F32),
                        pltpu.VMEM((rows, 128), F32), pltpu.VMEM((rows, 128), F32)])
    return pl.pallas_call(
        kern, grid_spec=grid_spec,
        out_shape=jax.ShapeDtypeStruct((db, ds, 512), F32),
        compiler_params=_cparams(("arbitrary", "arbitrary")),
        name="dsa_sample",
    )(pt_flat, p16s, p16s, p32s, p16s, p16s, p16s,
      *([cache_i] * n_pg), *([cache_k] * n_pg), *([cache_v] * n_pg))


def _regroup_weights(w_in):
    sizes = (512, 256, 256, 512, 512, 128, 128, 512, 64, 8, 512)
    offs = [0]
    for s in sizes:
        offs.append(offs[-1] + s)
    aq, ak, av, az, bq, bk, bv, iq, ik, iw, bz = [w_in[..., offs[j]:offs[j + 1]] for j in range(len(sizes))]
    scale = HEAD_DIM ** -0.5
    pad = jnp.zeros(iw.shape[:-1] + (128 - iw.shape[-1],), w_in.dtype)
    return jnp.concatenate([aq * scale, bq * scale, iq, ak, av, bk, bv, ik, ik, az, bz, iw, pad],
                           axis=-1).astype(BF16)


def _pick_tile(n, candidates):
    for c in candidates:
        if n % c == 0:
            return c
    return n


def kernel(x_prompt, x_sample, cache_diff_k, cache_diff_v, cache_dsa_k, cache_dsa_v, cache_idx_k, page_table,
           meta_tokens, ln_in_g, ln_in_b, w_in, w_out, lambda_q1, lambda_k1, lambda_q2, lambda_k2,
           subln_g, ln_g, ln_b):
    nb, seq, d = x_prompt.shape
    db, ds, _ = x_sample.shape
    depth = w_in.shape[0]
    n_pool = cache_diff_k.shape[1]
    n_pages = page_table.shape[1]
    t_real = seq + N_META
    tpad = -(-t_real // KV_CHUNK) * KV_CHUNK
    assert ds == 8 and n_pages % PAGES_PER_STEP == 0 and d == 1024
    alpha = (2 * depth) ** 0.25
    topk_p = min(TOPK_MAX, seq // 4)
    topk_s = min(TOPK_MAX, (n_pages * PAGE_SIZE + ds) // 4)

    meta = jnp.broadcast_to(meta_tokens[None].astype(x_prompt.dtype), (nb, N_META, d))
    xp = jnp.concatenate([meta, x_prompt, jnp.zeros((nb, tpad - t_real, d), x_prompt.dtype)], axis=1)
    xp = _layernorm_rows(xp.reshape(nb * tpad, d), ln_in_g, ln_in_b, KV_CHUNK).reshape(nb, tpad, d)
    xs = _layernorm_rows(x_sample.reshape(db * ds, d), ln_in_g, ln_in_b,
                         _pick_tile(db * ds, (256, 128, 8))).reshape(1, db * ds, d)

    w = _regroup_weights(w_in)
    w_o = w_out.astype(BF16)
    pad64 = lambda v: jnp.pad(v, ((0, 0), (0, 128 - v.shape[-1])))
    ck = cache_diff_k.transpose(0, 1, 3, 4, 5, 2).reshape(depth, n_pool, 256, PAGE_SIZE)
    cv = cache_diff_v.reshape(depth, n_pool, PAGE_SIZE * A_KV_HEADS, 128)
    cbk = cache_dsa_k.transpose(0, 1, 3, 4, 2).reshape(depth, n_pool, 128, PAGE_SIZE)
    cbv = cache_dsa_v.transpose(0, 1, 3, 4, 2).reshape(depth, n_pool, 128, PAGE_SIZE)
    ci = cache_idx_k.transpose(0, 1, 3, 2)
    pt_flat = page_table.reshape(-1).astype(I32)
    tm_p = _pick_tile(tpad, (256, 128))
    tm_s = _pick_tile(db * ds, (256, 128, 8))

    p_rows = [[] for _ in range(5)]
    s_rows = [[] for _ in range(5)]
    for layer in range(depth):
        lam_init = 0.8 - 0.6 * math.exp(-0.3 * layer)
        consts = jnp.stack([jnp.full((128,), lam_init, F32), jnp.full((128,), 1.0 - lam_init, F32),
                            jnp.zeros((128,), F32), jnp.zeros((128,), F32)])
        lam = jnp.concatenate([pad64(jnp.stack([lambda_q1[layer], lambda_k1[layer],
                                                 lambda_q2[layer], lambda_k2[layer]])), consts], axis=0)

        p16, p32, *rows_p = _project(xp, w, layer, t_real, tm_p, BF16, "proj_prompt")
        ya = _diff_prompt(p16, lam, t_real)
        yb = _dsa_prompt(p16, p32, t_real, topk_p)
        xp = _merge(xp, ya, yb, p32, w_o, layer, lam, subln_g[layer], ln_g[layer], ln_b[layer],
                    alpha, tm_p, "merge_prompt")

        p16s, p32s, *rows_s = _project(xs, w, layer, db * ds, tm_s, F32, "proj_sample")
        p16s3 = p16s.reshape(db, ds, W16)
        ya_s = _diff_sample(p16s3, lam, ck, cv, pt_flat, layer, n_pages)
        yb_s = _dsa_sample(p16s3, p32s.reshape(db, ds, W32), ci, cbk, cbv, pt_flat, layer,
                           n_pages, topk_s)
        xs = _merge(xs, ya_s.reshape(1, db * ds, 512), yb_s.reshape(1, db * ds, 512), p32s, w_o, layer, lam,
                    subln_g[layer], ln_g[layer], ln_b[layer], alpha, tm_s, "merge_sample")

        for lst, a in zip(p_rows, rows_p):
            lst.append(a)
        for lst, a in zip(s_rows, rows_s):
            lst.append(a)

    pk, pv, pbk, pbv, pik = [jnp.stack(r) for r in p_rows]
    sk, sv, sbk, sbv, sik = [jnp.stack(r).reshape(depth, db, ds, -1) for r in s_rows]
    return (xp[:, N_META:t_real], xs.reshape(db, ds, d),
            pk.reshape(depth, nb, t_real, A_KV_HEADS, 2, HEAD_DIM),
            pv.reshape(depth, nb, t_real, A_KV_HEADS, 2 * HEAD_DIM),
            pbk.reshape(depth, nb, t_real, B_KV_HEADS, HEAD_DIM),
            pbv.reshape(depth, nb, t_real, B_KV_HEADS, HEAD_DIM),
            pik,
            sk.reshape(depth, db, ds, A_KV_HEADS, 2, HEAD_DIM),
            sv.reshape(depth, db, ds, A_KV_HEADS, 2 * HEAD_DIM),
            sbk.reshape(depth, db, ds, B_KV_HEADS, HEAD_DIM),
            sbv.reshape(depth, db, ds, B_KV_HEADS, HEAD_DIM),
            sik)
```

```python
import functools
import math

import jax
import jax.numpy as jnp
from jax import lax
from jax.experimental import pallas as pl
from jax.experimental.pallas import tpu as pltpu

N_META = 16
HEAD_DIM = 64
A_HEADS = 4
A_KV_HEADS = 2
B_HEADS = 8
B_KV_HEADS = 2
IDX_HEADS = 8
IDX_DIM = 64
IDX_SCALE = (IDX_HEADS * IDX_DIM) ** -0.5
TOPK_MAX = 256
PAGE_SIZE = 128
LN_EPS = 1e-5

Q_TILE = 128
DIFF_Q_TILE = 256
KV_CHUNK = 512
DSA_CHUNK = 512
PAGES_PER_STEP = 32
VMEM_LIMIT_BYTES = 56 * 1024 * 1024

INT_MIN = -(2 ** 31)
NEG_INF = float("-inf")

C_AQ, C_BQ, C_IQ, C_AK, C_AV, C_BK, C_BV, C_IK = 0, 512, 1024, 1536, 1792, 2048, 2176, 2304
W16 = 2432
C_AZ, C_BZ, C_IW = 0, 512, 1024
W32 = 1152
NCOL = W16 + W32

F32 = jnp.float32
BF16 = jnp.bfloat16
I32 = jnp.int32


def _dot_nt(a, b):
    return lax.dot_general(a, b, (((1,), (1,)), ((), ())), preferred_element_type=F32)


def _cparams(sem):
    return pltpu.CompilerParams(dimension_semantics=sem, vmem_limit_bytes=VMEM_LIMIT_BYTES)


def _sort_key(score):
    bits = lax.bitcast_convert_type(score, I32)
    return bits ^ ((bits >> 31) & 0x7FFFFFFF)


def _lane_lo(width=128):
    return lax.broadcasted_iota(I32, (1, width), 1) < 64


def _ln_kernel(x_ref, g_ref, b_ref, o_ref):
    x = x_ref[...]
    mu = jnp.mean(x, axis=-1, keepdims=True)
    xc = x - mu
    var = jnp.mean(xc * xc, axis=-1, keepdims=True)
    o_ref[...] = xc * lax.rsqrt(var + LN_EPS) * g_ref[...] + b_ref[...]


def _layernorm_rows(x, g, b, tm):
    rows, d = x.shape
    return pl.pallas_call(
        _ln_kernel,
        grid=(rows // tm,),
        in_specs=[pl.BlockSpec((tm, d), lambda i: (i, 0)),
                  pl.BlockSpec((1, d), lambda i: (0, 0)),
                  pl.BlockSpec((1, d), lambda i: (0, 0))],
        out_specs=pl.BlockSpec((tm, d), lambda i: (i, 0)),
        out_shape=jax.ShapeDtypeStruct((rows, d), F32),
        compiler_params=_cparams(("arbitrary",)),
        name="ln_in",
    )(x, g.reshape(1, d), b.reshape(1, d))


def _proj_kernel(x_ref, w_ref, p16_ref, p32_ref, ak_ref, av_ref, bk_ref, bv_ref, ik_ref, *, last_out_tile):
    xb = x_ref[...].astype(BF16)
    res = jnp.dot(xb, w_ref[...], preferred_element_type=F32)
    p16_ref[...] = res[:, :W16].astype(p16_ref.dtype)
    p32_ref[...] = res[:, W16:]

    @pl.when(pl.program_id(1) <= last_out_tile)
    def _():
        ak_ref[...] = res[:, C_AK:C_AK + 256]
        av_ref[...] = res[:, C_AV:C_AV + 256]
        bk_ref[...] = res[:, C_BK:C_BK + 128]
        bv_ref[...] = res[:, C_BV:C_BV + 128]
        ik_ref[...] = res[:, C_IK:C_IK + 64]


def _project(x, w, layer, t_out, tm, p16_dtype, name):
    nb, t, d = x.shape
    last = pl.cdiv(t_out, tm) - 1
    row = lambda wd: pl.BlockSpec((None, tm, wd), lambda b, i: (b, i, 0))
    out = lambda wd: pl.BlockSpec((None, tm, wd), lambda b, i: (b, jnp.minimum(i, last), 0))
    shp = lambda wd: jax.ShapeDtypeStruct((nb, t_out, wd), F32)
    return pl.pallas_call(
        functools.partial(_proj_kernel, last_out_tile=last),
        grid=(nb, t // tm),
        in_specs=[row(d), pl.BlockSpec((None, d, NCOL), lambda b, i: (layer, 0, 0))],
        out_specs=[row(W16), row(W32), out(256), out(256), out(128), out(128), out(64)],
        out_shape=[jax.ShapeDtypeStruct((nb, t, W16), p16_dtype),
                   jax.ShapeDtypeStruct((nb, t, W32), F32),
                   shp(256), shp(256), shp(128), shp(128), shp(64)],
        compiler_params=_cparams(("arbitrary", "arbitrary")),
        name=name,
    )(x, w)


def _diff_lambda(lam_ref):
    r = lam_ref[...]
    e1 = jnp.exp(jnp.sum(r[0:1] * r[1:2], axis=1, keepdims=True))
    e2 = jnp.exp(jnp.sum(r[2:3] * r[3:4], axis=1, keepdims=True))
    return e1 - e2 + r[4:5, 0:1]


def _half_select(x, keep_low):
    lo = _lane_lo()
    return jnp.where(lo, x, 0.0) if keep_low else jnp.where(lo, 0.0, x)


def _diff_query_rows(q, n_rows):
    pieces = [q[:, 0:128], q[:, 128:256]]
    return jnp.concatenate([_half_select(p, True) for p in pieces] +
                           [_half_select(p, False) for p in pieces], axis=0).astype(BF16)


def _dsa_query_rows(q):
    out = []
    for h in range(B_HEADS):
        n = h // (B_HEADS // B_KV_HEADS)
        p = q[:, 128 * (h // 2):128 * (h // 2) + 128]
        if h % 2 != n:
            p = pltpu.roll(p, 64, 1)
        out.append(_half_select(p, n == 0))
    return jnp.concatenate(out, axis=0).astype(BF16)


def _dsa_merge_heads(o, r):
    outs = []
    lo = _lane_lo()
    for j in range(B_HEADS // 2):
        n = j // 2
        oe, oo = o[2 * j * r:(2 * j + 1) * r], o[(2 * j + 1) * r:(2 * j + 2) * r]
        if n == 0:
            outs.append(jnp.where(lo, oe, pltpu.roll(oo, 64, 1)))
        else:
            outs.append(jnp.where(lo, pltpu.roll(oe, 64, 1), oo))
    return jnp.concatenate(outs, axis=1)


def _const_rows(values, r):
    return jnp.concatenate([jnp.full((r, 1), v, F32) for v in values], axis=0)


def _count(mask):
    return jnp.sum(jnp.where(mask, 1.0, 0.0), axis=1, keepdims=True)


def _lane_fold(x, op):
    acc = x[:, 0:128]
    for c in range(1, x.shape[1] // 128):
        acc = op(acc, x[:, 128 * c:128 * (c + 1)])
    return acc


def _softmax_weigh(n_chunks, tc, rows, s_ref, v_ref, macc, acc_ref, lacc_ref):
    m = jnp.broadcast_to(jnp.max(macc, axis=1, keepdims=True), (rows, 128))
    acc_ref[...] = jnp.zeros_like(acc_ref)
    lacc_ref[...] = jnp.zeros_like(lacc_ref)

    def step(j, _):
        s = s_ref[j]
        p = jnp.concatenate([jnp.exp(s[:, 128 * c:128 * (c + 1)] - m) for c in range(tc // 128)], axis=1)
        lacc_ref[...] += _lane_fold(p, jnp.add)
        acc_ref[...] += jnp.dot(p.astype(BF16), v_ref[pl.ds(pl.multiple_of(j * tc, tc), tc), :],
                                preferred_element_type=F32)
        return 0

    lax.fori_loop(0, n_chunks, step, 0)
    return acc_ref[...] / jnp.sum(lacc_ref[...], axis=1, keepdims=True)


def _diff_prompt_kernel(lam_ref, q_ref, k_ref, v_ref, o_ref, s_ref, acc_ref, lacc_ref, *, n_real_tiles):
    n = pl.program_id(1)
    i = pl.program_id(2)
    tq, tc = DIFF_Q_TILE, KV_CHUNK

    @pl.when(i >= n_real_tiles)
    def _():
        o_ref[...] = jnp.zeros_like(o_ref)

    @pl.when(i < n_real_tiles)
    def _():
        lam = _diff_lambda(lam_ref)
        qst = _diff_query_rows(q_ref[...].astype(F32), tq)
        slope_g = (jnp.where(n == 0, 2.0 ** -2, 2.0 ** -6), jnp.where(n == 0, 2.0 ** -4, 2.0 ** -8))
        qpos = i * tq + (lax.broadcasted_iota(I32, (4 * tq, 1), 0) & (tq - 1))
        col = lax.broadcasted_iota(I32, (1, tc), 1)

        def score_step(j, macc, masked):
            k0 = pl.multiple_of(j * tc, tc)
            qk = _dot_nt(qst, k_ref[pl.ds(k0, tc), :])
            kpos = (col + k0).astype(F32)
            s = jnp.concatenate([qk[r * tq:(r + 1) * tq] + slope_g[r % 2] * kpos for r in range(4)], axis=0)
            if masked:
                s = jnp.where(col + k0 <= qpos, s, NEG_INF)
            s_ref[j] = s
            return jnp.maximum(macc, _lane_fold(s, jnp.maximum))

        n_full = (i * tq) // tc
        macc = lax.fori_loop(0, n_full, lambda j, c: score_step(j, c, False),
                             jnp.full((4 * tq, 128), NEG_INF, F32))
        macc = score_step(n_full, macc, True)
        o = _softmax_weigh(n_full + 1, tc, 4 * tq, s_ref, v_ref, macc, acc_ref, lacc_ref)
        out_g0 = o[0:tq] - lam * o[2 * tq:3 * tq]
        out_g1 = o[tq:2 * tq] - lam * o[3 * tq:4 * tq]
        o_ref[...] = jnp.concatenate([out_g0, out_g1], axis=1)


def _diff_prompt(p16, lam, t_real):
    nb, tpad, _ = p16.shape
    tq = DIFF_Q_TILE
    kern = functools.partial(_diff_prompt_kernel, n_real_tiles=pl.cdiv(t_real, tq))
    return pl.pallas_call(
        kern,
        grid=(nb, A_KV_HEADS, tpad // tq),
        in_specs=[pl.BlockSpec((8, 128), lambda b, n, i: (0, 0)),
                  pl.BlockSpec((None, tq, 256), lambda b, n, i: (b, i, C_AQ // 256 + n)),
                  pl.BlockSpec((None, tpad, 128), lambda b, n, i: (b, 0, C_AK // 128 + n)),
                  pl.BlockSpec((None, tpad, 128), lambda b, n, i: (b, 0, C_AV // 128 + n))],
        out_specs=pl.BlockSpec((None, tq, 256), lambda b, n, i: (b, i, n)),
        out_shape=jax.ShapeDtypeStruct((nb, tpad, 512), F32),
        scratch_shapes=[pltpu.VMEM((tpad // KV_CHUNK, 4 * tq, KV_CHUNK), F32),
                        pltpu.VMEM((4 * tq, 128), F32), pltpu.VMEM((4 * tq, 128), F32)],
        compiler_params=_cparams(("arbitrary", "arbitrary", "arbitrary")),
        name="diff_prompt",
    )(lam, p16, p16, p16)


def _dsa_prompt_kernel(bq_ref, iq_ref, iw_ref, bk_ref, bv_ref, ik_ref, o_ref, keys_ref, s_ref, acc_ref, lacc_ref, *,
                       n_real_tiles, topk, idx_bits):
    i = pl.program_id(1)
    tq, tc = Q_TILE, DSA_CHUNK

    @pl.when(i >= n_real_tiles)
    def _():
        o_ref[...] = jnp.zeros_like(o_ref)

    @pl.when(i < n_real_tiles)
    def _():
        n_chunks = (i * tq) // tc + 1
        col = lax.broadcasted_iota(I32, (1, tc), 1)
        qpos = i * tq + lax.broadcasted_iota(I32, (tq, 1), 0)

        iq = iq_ref[...].astype(F32)
        iqst = jnp.concatenate(
            [_half_select(iq[:, 128 * (h // 2):128 * (h // 2) + 128], h % 2 == 0)
             for h in range(IDX_HEADS)], axis=0).astype(BF16)
        w = iw_ref[...]
        wcol = [jnp.broadcast_to(w[:, h:h + 1], (tq, tc)) for h in range(IDX_HEADS)]

        def index_step(j, _):
            k0 = pl.multiple_of(j * tc, tc)
            logits = _dot_nt(iqst, ik_ref[pl.ds(k0, tc), :])
            sc = jnp.maximum(logits[0:tq], 0.0) * wcol[0]
            for h in range(1, IDX_HEADS):
                sc = sc + jnp.maximum(logits[h * tq:(h + 1) * tq], 0.0) * wcol[h]
            key = _sort_key(sc * IDX_SCALE)
            keys_ref[j] = jnp.where(col + k0 <= qpos, key, INT_MIN)
            return 0

        lax.fori_loop(0, n_chunks, index_step, 0)

        def count_over_chunks(pred):
            def body(j, acc):
                return acc + _lane_fold(jnp.where(pred(keys_ref[j], col + j * tc), 1.0, 0.0), jnp.add)
            acc = lax.fori_loop(0, n_chunks, body, jnp.zeros((tq, 128), F32))
            return jnp.sum(acc, axis=1, keepdims=True)

        def bit_step(b, t):
            cand = t + lax.shift_left(jnp.int32(1), 31 - b)
            cnt = count_over_chunks(lambda kc, _: kc >= cand)
            return jnp.where(cnt >= topk, cand, t)

        t = lax.fori_loop(0, 32, bit_step, jnp.full((tq, 1), INT_MIN, I32))
        thr = jnp.maximum(t, INT_MIN + 1)
        need = topk - count_over_chunks(lambda kc, _: kc > thr)
        n_eq = count_over_chunks(lambda kc, _: kc == thr)
        excess = n_eq > need

        def tie_bound():
            def tie_step(b, jb):
                cand = jb + lax.shift_left(jnp.int32(1), idx_bits - 1 - b)
                cnt = count_over_chunks(lambda kc, idx: (kc == thr) & (idx < cand))
                return jnp.where(cnt < need, cand, jb)
            return lax.fori_loop(0, idx_bits, tie_step, jnp.zeros((tq, 1), I32))

        any_excess = jnp.max(jnp.where(excess, 1.0, 0.0)) > 0.0
        big = jnp.full((tq, 1), 2 ** 30, I32)
        bound = lax.cond(any_excess, lambda: jnp.where(excess, tie_bound(), big), lambda: big)

        qst = _dsa_query_rows(bq_ref[...].astype(F32))

        def score_step(j, macc):
            k0 = pl.multiple_of(j * tc, tc)
            kc = keys_ref[j]
            sel = (kc > thr) | ((kc == thr) & (col + k0 <= bound))
            qk = _dot_nt(qst, bk_ref[pl.ds(k0, tc), :])
            kpos = (col + k0).astype(F32)
            s = jnp.concatenate(
                [jnp.where(sel, qk[h * tq:(h + 1) * tq] + 2.0 ** -(h + 1) * kpos, NEG_INF)
                 for h in range(B_HEADS)], axis=0)
            s_ref[j] = s
            return jnp.maximum(macc, _lane_fold(s, jnp.maximum))

        macc = lax.fori_loop(0, n_chunks, score_step, jnp.full((B_HEADS * tq, 128), NEG_INF, F32))
        o = _softmax_weigh(n_chunks, tc, B_HEADS * tq, s_ref, bv_ref, macc, acc_ref, lacc_ref)
        o_ref[...] = _dsa_merge_heads(o, tq)


def _dsa_prompt(p16, p32, t_real, topk):
    nb, tpad, _ = p16.shape
    nq = tpad // Q_TILE
    kern = functools.partial(_dsa_prompt_kernel, n_real_tiles=pl.cdiv(t_real, Q_TILE), topk=float(topk),
                             idx_bits=max(1, int(tpad).bit_length()))
    whole = lambda c: pl.BlockSpec((None, tpad, 128), lambda b, i: (b, 0, c // 128))
    return pl.pallas_call(
        kern,
        grid=(nb, nq),
        in_specs=[pl.BlockSpec((None, Q_TILE, 512), lambda b, i: (b, i, C_BQ // 512)),
                  pl.BlockSpec((None, Q_TILE, 512), lambda b, i: (b, i, C_IQ // 512)),
                  pl.BlockSpec((None, Q_TILE, 128), lambda b, i: (b, i, C_IW // 128)),
                  whole(C_BK), whole(C_BV), whole(C_IK)],
        out_specs=pl.BlockSpec((None, Q_TILE, 512), lambda b, i: (b, i, 0)),
        out_shape=jax.ShapeDtypeStruct((nb, tpad, 512), F32),
        scratch_shapes=[pltpu.VMEM((tpad // DSA_CHUNK, Q_TILE, DSA_CHUNK), I32),
                        pltpu.VMEM((tpad // DSA_CHUNK, B_HEADS * Q_TILE, DSA_CHUNK), F32),
                        pltpu.VMEM((B_HEADS * Q_TILE, 128), F32), pltpu.VMEM((B_HEADS * Q_TILE, 128), F32)],
        compiler_params=_cparams(("arbitrary", "arbitrary")),
        name="dsa_prompt",
    )(p16, p16, p32, p16, p16, p16)


def _merge_kernel(sc_ref, x_ref, ya_ref, yb_ref, az_ref, bz_ref, w_ref, sg_ref, g_ref, b_ref, o_ref, *, alpha):
    one_minus_lam_init = sc_ref[5:6, 0:1]
    ya = ya_ref[...]
    parts = []
    for h in range(A_HEADS):
        o = ya[:, 128 * h:128 * h + 128]
        y = o * lax.rsqrt(jnp.mean(o * o, axis=-1, keepdims=True) + LN_EPS) * sg_ref[...]
        parts.append(y * one_minus_lam_init)
    az = az_ref[...]
    bz = bz_ref[...]
    ya_g = jnp.concatenate(parts, axis=1) * (az * (1.0 / (1.0 + jnp.exp(-az))))
    yb_g = yb_ref[...] * (bz * (1.0 / (1.0 + jnp.exp(-bz))))
    mix = jnp.concatenate([ya_g, yb_g], axis=1).astype(BF16)
    y = alpha * x_ref[...] + jnp.dot(mix, w_ref[...], preferred_element_type=F32)
    mu = jnp.mean(y, axis=-1, keepdims=True)
    yc = y - mu
    var = jnp.mean(yc * yc, axis=-1, keepdims=True)
    o_ref[...] = yc * lax.rsqrt(var + LN_EPS) * g_ref[...] + b_ref[...]


def _merge(x, ya, yb, p32, w_out, layer, sc, sub_g, ln_g, ln_b, alpha, tm, name):
    nb, t, d = x.shape
    row = lambda wd, c: pl.BlockSpec((None, tm, wd), lambda b, i: (b, i, c))
    vec = lambda wd: pl.BlockSpec((1, wd), lambda b, i: (0, 0))
    return pl.pallas_call(
        functools.partial(_merge_kernel, alpha=alpha),
        grid=(nb, t // tm),
        in_specs=[pl.BlockSpec((8, 128), lambda b, i: (0, 0)),
                  row(d, 0), row(512, 0), row(512, 0), row(512, C_AZ // 512), row(512, C_BZ // 512),
                  pl.BlockSpec((None, d, d), lambda b, i: (layer, 0, 0)),
                  vec(128), vec(d), vec(d)],
        out_specs=row(d, 0),
        out_shape=jax.ShapeDtypeStruct((nb, t, d), F32),
        compiler_params=_cparams(("arbitrary", "arbitrary")),
        name=name,
    )(sc, x, ya, yb, p32, p32, w_out, sub_g.reshape(1, 128), ln_g.reshape(1, d), ln_b.reshape(1, d))


def _pad_rows(x, rows):
    return jnp.concatenate([x, jnp.zeros((rows - x.shape[0], x.shape[1]), x.dtype)], axis=0)


def _diff_sample_kernel(pt_ref, lam_ref, q_ref, kn_ref, vn_ref, *rest, n_pg, past):
    k_refs, v_refs = rest[:n_pg], rest[n_pg:2 * n_pg]
    o_ref, m_ref, l_ref, acc_ref = rest[2 * n_pg:]
    pg = pl.program_id(1)
    ds = q_ref.shape[0]
    rows = 2 * A_KV_HEADS * 2 * ds
    span = n_pg * PAGE_SIZE

    @pl.when(pg == 0)
    def _():
        m_ref[...] = jnp.full_like(m_ref, NEG_INF)
        l_ref[...] = jnp.zeros_like(l_ref)
        acc_ref[...] = jnp.zeros_like(acc_ref)

    q = q_ref[...]
    qst = jnp.concatenate([_diff_query_rows(q[:, 256 * n:256 * n + 256], ds) for n in range(A_KV_HEADS)], axis=0)
    slope = _const_rows([2.0 ** (-2 * (2 * n + g + 1)) for n in range(A_KV_HEADS) for _c in range(2)
                         for g in range(2)], ds)
    half = rows // A_KV_HEADS

    def attend(score, weigh, width, k0, mask):
        col = lax.broadcasted_iota(I32, (1, width), 1)
        s = jnp.concatenate([score(n, qst[half * n:half * (n + 1)]) for n in range(A_KV_HEADS)],
                            axis=0) + slope * col.astype(F32)
        if mask is not None:
            s = jnp.where(mask(col), s, NEG_INF)
        cj = slope * k0
        m, l, acc = m_ref[...][:, 0:1], l_ref[...][:, 0:1], acc_ref[...]
        mt = jnp.max(s, axis=1, keepdims=True)
        m_new = jnp.maximum(m, mt + cj)
        p = jnp.exp(s - (m_new - cj))
        alpha = jnp.exp(m - m_new)
        l = alpha * l + jnp.sum(p, axis=1, keepdims=True)
        pv = jnp.concatenate([weigh(n, p[half * n:half * (n + 1)].astype(BF16)) for n in range(A_KV_HEADS)],
                             axis=0)
        m_ref[...] = jnp.broadcast_to(m_new, m_ref.shape)
        l_ref[...] = jnp.broadcast_to(l, l_ref.shape)
        acc_ref[...] = alpha * acc + pv

    def past_score(n, q):
        kt = jnp.concatenate([r[128 * n:128 * n + 128, :] for r in k_refs], axis=1).astype(BF16)
        return jnp.dot(q, kt, preferred_element_type=F32)

    def past_weigh(n, p):
        v = jnp.concatenate([r[pl.ds(n, PAGE_SIZE, stride=A_KV_HEADS), :] for r in v_refs], axis=0).astype(BF16)
        return jnp.dot(p, v, preferred_element_type=F32)

    attend(past_score, past_weigh, span, (pg * span).astype(F32), None)

    @pl.when(pg == past // span - 1)
    def _():
        r = lax.broadcasted_iota(I32, (rows, 1), 0) & (ds - 1)
        kn = _pad_rows(kn_ref[...], 128).astype(BF16)
        vn = _pad_rows(vn_ref[...], 128).astype(BF16)
        attend(lambda n, q: _dot_nt(q, kn[:, 128 * n:128 * n + 128]),
               lambda n, p: jnp.dot(p, vn[:, 128 * n:128 * n + 128], preferred_element_type=F32),
               128, jnp.float32(past), lambda col: col <= r)
        lam = _diff_lambda(lam_ref)
        o = acc_ref[...] / l_ref[...][:, 0:1]
        outs = []
        for n in range(A_KV_HEADS):
            base = half * n
            for g in range(2):
                outs.append(o[base + g * ds:base + (g + 1) * ds]
                            - lam * o[base + (2 + g) * ds:base + (3 + g) * ds])
        o_ref[...] = jnp.concatenate(outs, axis=1)


def _diff_sample(p16s, lam, cache_k, cache_v, pt_flat, layer, n_pages):
    db, ds, _ = p16s.shape
    n_pg = PAGES_PER_STEP
    rows = 2 * A_KV_HEADS * 2 * ds
    past = n_pages * PAGE_SIZE
    kern = functools.partial(_diff_sample_kernel, n_pg=n_pg, past=past)

    def page(r):
        return pl.BlockSpec((None, None, 256, PAGE_SIZE),
                            lambda sb, pg, pt: (layer, pt[sb * n_pages + pg * n_pg + r], 0, 0))

    new = lambda c: pl.BlockSpec((None, ds, 256), lambda sb, pg, pt: (sb, 0, c // 256))
    grid_spec = pltpu.PrefetchScalarGridSpec(
        num_scalar_prefetch=1,
        grid=(db, n_pages // n_pg),
        in_specs=[pl.BlockSpec((8, 128), lambda sb, pg, pt: (0, 0)),
                  pl.BlockSpec((None, ds, 512), lambda sb, pg, pt: (sb, 0, C_AQ // 512)),
                  new(C_AK), new(C_AV)] + [page(r) for r in range(n_pg)] * 2,
        out_specs=pl.BlockSpec((None, ds, 512), lambda sb, pg, pt: (sb, 0, 0)),
        scratch_shapes=[pltpu.VMEM((rows, 128), F32), pltpu.VMEM((rows, 128), F32),
                        pltpu.VMEM((rows, 128), F32)])
    return pl.pallas_call(
        kern, grid_spec=grid_spec,
        out_shape=jax.ShapeDtypeStruct((db, ds, 512), F32),
        compiler_params=_cparams(("arbitrary", "arbitrary")),
        name="diff_sample",
    )(pt_flat, lam, p16s, p16s, p16s, *([cache_k] * n_pg), *([cache_v] * n_pg))


def _dsa_sample_kernel(pt_ref, bq_ref, iq_ref, iw_ref, kn_ref, vn_ref, in_ref, *rest, n_pg, past, topk, idx_bits):
    i_refs, k_refs, v_refs = rest[:n_pg], rest[n_pg:2 * n_pg], rest[2 * n_pg:3 * n_pg]
    o_ref, keys_ref, keys_new_ref, thr_ref, bound_ref, m_ref, l_ref, acc_ref = rest[3 * n_pg:]
    pg = pl.program_id(1)
    ds = bq_ref.shape[0]
    rows = B_HEADS * ds
    span = n_pg * PAGE_SIZE
    n_groups = past // span
    r_id = lax.broadcasted_iota(I32, (ds, 1), 0)

    def index_keys(ik, transposed):
        iq = iq_ref[...]
        heads = []
        for h in range(IDX_HEADS):
            p = iq[:, 128 * (h // 2):128 * (h // 2) + 128]
            if h % 2:
                p = pltpu.roll(p, 64, 1)
            heads.append(p[:, 0:64])
        iqst = jnp.concatenate(heads, axis=0).astype(BF16)
        if transposed:
            logits = jnp.dot(iqst, ik.astype(BF16), preferred_element_type=F32)
        else:
            logits = _dot_nt(iqst, ik.astype(BF16))
        w = iw_ref[...]
        sc = jnp.maximum(logits[0:ds], 0.0) * w[:, 0:1]
        for h in range(1, IDX_HEADS):
            sc = sc + jnp.maximum(logits[h * ds:(h + 1) * ds], 0.0) * w[:, h:h + 1]
        return _sort_key(sc * IDX_SCALE)

    @pl.when(pg < n_groups)
    def _():
        ik = jnp.concatenate([r[...] for r in i_refs], axis=1)
        keys_ref[pg] = index_keys(ik, True)

    @pl.when(pg == n_groups - 1)
    def _():
        col = lax.broadcasted_iota(I32, (1, 128), 1)
        key_new = index_keys(_pad_rows(in_ref[...][:, 0:64], 128), False)
        keys_new = jnp.where(col <= r_id, key_new, INT_MIN)
        keys_new_ref[...] = keys_new
        keys = keys_ref[...]
        idx = (lax.broadcasted_iota(I32, (n_groups, 1, span), 0) * span
               + lax.broadcasted_iota(I32, (n_groups, 1, span), 2))
        idx_new = col + past

        def count(pred):
            hit = jnp.sum(jnp.where(pred(keys, idx), 1.0, 0.0), axis=0)
            return (jnp.sum(hit, axis=1, keepdims=True) + _count(pred(keys_new, idx_new)))

        def bit_step(b, t):
            cand = t + lax.shift_left(jnp.int32(1), 31 - b)
            return jnp.where(count(lambda kc, _: kc >= cand) >= topk, cand, t)

        t = lax.fori_loop(0, 32, bit_step, jnp.full((ds, 1), INT_MIN, I32))
        thr = jnp.maximum(t, INT_MIN + 1)
        need = topk - count(lambda kc, _: kc > thr)
        excess = count(lambda kc, _: kc == thr) > need

        def tie_step(b, jb):
            cand = jb + lax.shift_left(jnp.int32(1), idx_bits - 1 - b)
            return jnp.where(count(lambda kc, ix: (kc == thr) & (ix < cand)) < need, cand, jb)

        jb = lax.fori_loop(0, idx_bits, tie_step, jnp.zeros((ds, 1), I32))
        thr_ref[...] = jnp.broadcast_to(thr, thr_ref.shape)
        bound_ref[...] = jnp.broadcast_to(jnp.where(excess, jb, 2 ** 30), bound_ref.shape)

    qst = _dsa_query_rows(bq_ref[...])
    slope = _const_rows([2.0 ** -(h + 1) for h in range(B_HEADS)], ds)

    def attend(qk, weigh, kc, k0):
        width = qk.shape[1]
        col = lax.broadcasted_iota(I32, (1, width), 1)
        thr, bound = thr_ref[...][:, 0:1], bound_ref[...][:, 0:1]
        sel = (kc > thr) | ((kc == thr) & (col + k0 <= bound))
        s = (qk + slope * col.astype(F32)).reshape(B_HEADS, ds, width)
        s = jnp.where(sel[None], s, NEG_INF).reshape(rows, width)
        cj = slope * k0.astype(F32)
        m, l, acc = m_ref[...][:, 0:1], l_ref[...][:, 0:1], acc_ref[...]
        mt = jnp.max(s, axis=1, keepdims=True)
        m_new = jnp.maximum(m, mt + cj)
        m_safe = jnp.where(m_new == NEG_INF, 0.0, m_new)
        p = jnp.exp(s - (m_safe - cj))
        alpha = jnp.exp(m - m_safe)
        m_ref[...] = jnp.broadcast_to(m_new, m_ref.shape)
        l_ref[...] = jnp.broadcast_to(alpha * l + jnp.sum(p, axis=1, keepdims=True), l_ref.shape)
        acc_ref[...] = alpha * acc + weigh(p.astype(BF16))

    @pl.when(pg == n_groups)
    def _():
        m_ref[...] = jnp.full_like(m_ref, NEG_INF)
        l_ref[...] = jnp.zeros_like(l_ref)
        acc_ref[...] = jnp.zeros_like(acc_ref)

    @pl.when(pg >= n_groups)
    def _():
        kt = jnp.concatenate([r[...] for r in k_refs], axis=1).astype(BF16)
        vt = jnp.concatenate([r[...] for r in v_refs], axis=1).astype(BF16)
        attend(jnp.dot(qst, kt, preferred_element_type=F32), lambda p: _dot_nt(p, vt),
               keys_ref[pg - n_groups], (pg - n_groups) * span)

    @pl.when(pg == 2 * n_groups - 1)
    def _():
        kn = _pad_rows(kn_ref[...], 128).astype(BF16)
        vn = _pad_rows(vn_ref[...], 128).astype(BF16)
        attend(_dot_nt(qst, kn), lambda p: jnp.dot(p, vn, preferred_element_type=F32),
               keys_new_ref[...], jnp.int32(past))
        o_ref[...] = _dsa_merge_heads(acc_ref[...] / l_ref[...][:, 0:1], ds)


def _dsa_sample(p16s, p32s, cache_i, cache_k, cache_v, pt_flat, layer, n_pages, topk):
    db, ds, _ = p16s.shape
    n_pg = PAGES_PER_STEP
    n_groups = n_pages // n_pg
    past = n_pages * PAGE_SIZE
    rows = B_HEADS * ds
    kern = functools.partial(_dsa_sample_kernel, n_pg=n_pg, past=past, topk=float(topk),
                             idx_bits=int(past + 128).bit_length())

    def page(r, feat, first_phase):
        def index(sb, pg, pt):
            grp = jnp.minimum(pg, n_groups - 1) if first_phase else jnp.maximum(pg - n_groups, 0)
            return (layer, pt[sb * n_pages + grp * n_pg + r], 0, 0)
        return pl.BlockSpec((None, None, feat, PAGE_SIZE), index)

    new = lambda c: pl.BlockSpec((None, ds, 128), lambda sb, pg, pt: (sb, 0, c // 128))
    grid_spec = pltpu.PrefetchScalarGridSpec(
        num_scalar_prefetch=1,
        grid=(db, 2 * n_groups),
        in_specs=[pl.BlockSpec((None, ds, 512), lambda sb, pg, pt: (sb, 0, C_BQ // 512)),
                  pl.BlockSpec((None, ds, 512), lambda sb, pg, pt: (sb, 0, C_IQ // 512)),
                  pl.BlockSpec((None, ds, 128), lambda sb, pg, pt: (sb, 0, C_IW // 128)),
                  new(C_BK), new(C_BV), new(C_IK)]
                 + [page(r, 64, True) for r in range(n_pg)]
                 + [page(r, 128, False) for r in range(n_pg)] * 2,
        out_specs=pl.BlockSpec((None, ds, 512), lambda sb, pg, pt: (sb, 0, 0)),
        scratch_shapes=[pltpu.VMEM((n_groups, ds, n_pg * PAGE_SIZE), I32), pltpu.VMEM((ds, 128), I32),
                        pltpu.VMEM((ds, 128), I32), pltpu.VMEM((ds, 128), I32), pltpu.VMEM((rows, 128), F32),
                        pltpu.VMEM((rows, 128), F32), pltpu.VMEM((rows, 128), F32)])
    return pl.pallas_call(
        kern, grid_spec=grid_spec,
        out_shape=jax.ShapeDtypeStruct((db, ds, 512), F32),
        compiler_params=_cparams(("arbitrary", "arbitrary")),
        name="dsa_sample",
    )(pt_flat, p16s, p16s, p32s, p16s, p16s, p16s,
      *([cache_i] * n_pg), *([cache_k] * n_pg), *([cache_v] * n_pg))


def _regroup_weights(w_in):
    sizes = (512, 256, 256, 512, 512, 128, 128, 512, 64, 8, 512)
    offs = [0]
    for s in sizes:
        offs.append(offs[-1] + s)
    aq, ak, av, az, bq, bk, bv, iq, ik, iw, bz = [w_in[..., offs[j]:offs[j + 1]] for j in range(len(sizes))]
    scale = HEAD_DIM ** -0.5
    pad = jnp.zeros(iw.shape[:-1] + (128 - iw.shape[-1],), w_in.dtype)
    return jnp.concatenate([aq * scale, bq * scale, iq, ak, av, bk, bv, ik, ik, az, bz, iw, pad],
                           axis=-1).astype(BF16)


def _pick_tile(n, candidates):
    for c in candidates:
        if n % c == 0:
            return c
    return n


def kernel(x_prompt, x_sample, cache_diff_k, cache_diff_v, cache_dsa_k, cache_dsa_v, cache_idx_k, page_table,
           meta_tokens, ln_in_g, ln_in_b, w_in, w_out, lambda_q1, lambda_k1, lambda_q2, lambda_k2,
           subln_g, ln_g, ln_b):
    nb, seq, d = x_prompt.shape
    db, ds, _ = x_sample.shape
    depth = w_in.shape[0]
    n_pool = cache_diff_k.shape[1]
    n_pages = page_table.shape[1]
    t_real = seq + N_META
    tpad = -(-t_real // DSA_CHUNK) * DSA_CHUNK
    assert ds == 8 and n_pages % PAGES_PER_STEP == 0 and d == 1024
    alpha = (2 * depth) ** 0.25
    topk_p = min(TOPK_MAX, seq // 4)
    topk_s = min(TOPK_MAX, (n_pages * PAGE_SIZE + ds) // 4)

    meta = jnp.broadcast_to(meta_tokens[None].astype(x_prompt.dtype), (nb, N_META, d))
    xp = jnp.concatenate([meta, x_prompt, jnp.zeros((nb, tpad - t_real, d), x_prompt.dtype)], axis=1)
    xp = _layernorm_rows(xp.reshape(nb * tpad, d), ln_in_g, ln_in_b, KV_CHUNK).reshape(nb, tpad, d)
    xs = _layernorm_rows(x_sample.reshape(db * ds, d), ln_in_g, ln_in_b,
                         _pick_tile(db * ds, (256, 128, 8))).reshape(1, db * ds, d)

    w = _regroup_weights(w_in)
    w_o = w_out.astype(BF16)
    pad64 = lambda v: jnp.pad(v, ((0, 0), (0, 128 - v.shape[-1])))
    ck = cache_diff_k.transpose(0, 1, 3, 4, 5, 2).reshape(depth, n_pool, 256, PAGE_SIZE)
    cv = cache_diff_v.reshape(depth, n_pool, PAGE_SIZE * A_KV_HEADS, 128)
    cbk = cache_dsa_k.transpose(0, 1, 3, 4, 2).reshape(depth, n_pool, 128, PAGE_SIZE)
    cbv = cache_dsa_v.transpose(0, 1, 3, 4, 2).reshape(depth, n_pool, 128, PAGE_SIZE)
    ci = cache_idx_k.transpose(0, 1, 3, 2)
    pt_flat = page_table.reshape(-1).astype(I32)
    tm_p = _pick_tile(tpad, (256, 128))
    tm_s = _pick_tile(db * ds, (256, 128, 8))

    p_rows = [[] for _ in range(5)]
    s_rows = [[] for _ in range(5)]
    for layer in range(depth):
        lam_init = 0.8 - 0.6 * math.exp(-0.3 * layer)
        consts = jnp.stack([jnp.full((128,), lam_init, F32), jnp.full((128,), 1.0 - lam_init, F32),
                            jnp.zeros((128,), F32), jnp.zeros((128,), F32)])
        lam = jnp.concatenate([pad64(jnp.stack([lambda_q1[layer], lambda_k1[layer],
                                                 lambda_q2[layer], lambda_k2[layer]])), consts], axis=0)

        p16, p32, *rows_p = _project(xp, w, layer, t_real, tm_p, BF16, "proj_prompt")
        ya = _diff_prompt(p16, lam, t_real)
        yb = _dsa_prompt(p16, p32, t_real, topk_p)
        xp = _merge(xp, ya, yb, p32, w_o, layer, lam, subln_g[layer], ln_g[layer], ln_b[layer],
                    alpha, tm_p, "merge_prompt")

        p16s, p32s, *rows_s = _project(xs, w, layer, db * ds, tm_s, F32, "proj_sample")
        p16s3 = p16s.reshape(db, ds, W16)
        ya_s = _diff_sample(p16s3, lam, ck, cv, pt_flat, layer, n_pages)
        yb_s = _dsa_sample(p16s3, p32s.reshape(db, ds, W32), ci, cbk, cbv, pt_flat, layer,
                           n_pages, topk_s)
        xs = _merge(xs, ya_s.reshape(1, db * ds, 512), yb_s.reshape(1, db * ds, 512), p32s, w_o, layer, lam,
                    subln_g[layer], ln_g[layer], ln_b[layer], alpha, tm_s, "merge_sample")

        for lst, a in zip(p_rows, rows_p):
            lst.append(a)
        for lst, a in zip(s_rows, rows_s):
            lst.append(a)

    pk, pv, pbk, pbv, pik = [jnp.stack(r) for r in p_rows]
    sk, sv, sbk, sbv, sik = [jnp.stack(r).reshape(depth, db, ds, -1) for r in s_rows]
    return (xp[:, N_META:t_real], xs.reshape(db, ds, d),
            pk.reshape(depth, nb, t_real, A_KV_HEADS, 2, HEAD_DIM),
            pv.reshape(depth, nb, t_real, A_KV_HEADS, 2 * HEAD_DIM),
            pbk.reshape(depth, nb, t_real, B_KV_HEADS, HEAD_DIM),
            pbv.reshape(depth, nb, t_real, B_KV_HEADS, HEAD_DIM),
            pik,
            sk.reshape(depth, db, ds, A_KV_HEADS, 2, HEAD_DIM),
            sv.reshape(depth, db, ds, A_KV_HEADS, 2 * HEAD_DIM),
            sbk.reshape(depth, db, ds, B_KV_HEADS, HEAD_DIM),
            sbv.reshape(depth, db, ds, B_KV_HEADS, HEAD_DIM),
            sik)
```

```python
import functools
import math

import jax
import jax.numpy as jnp
from jax import lax
from jax.experimental import pallas as pl
from jax.experimental.pallas import tpu as pltpu

N_META = 16
HEAD_DIM = 64
A_HEADS = 4
A_KV_HEADS = 2
B_HEADS = 8
B_KV_HEADS = 2
IDX_HEADS = 8
IDX_DIM = 64
IDX_SCALE = (IDX_HEADS * IDX_DIM) ** -0.5
TOPK_MAX = 256
PAGE_SIZE = 128
LN_EPS = 1e-5

Q_TILE = 128
DIFF_Q_TILE = 256
KV_CHUNK = 512
DSA_CHUNK = 512
PAGES_PER_STEP = 32
VMEM_LIMIT_BYTES = 56 * 1024 * 1024

INT_MIN = -(2 ** 31)
NEG_INF = float("-inf")

C_AQ, C_BQ, C_IQ, C_AK, C_AV, C_BK, C_BV, C_IK = 0, 512, 1024, 1536, 1792, 2048, 2176, 2304
W16 = 2432
C_AZ, C_BZ, C_IW = 0, 512, 1024
W32 = 1152
NCOL = W16 + W32

F32 = jnp.float32
BF16 = jnp.bfloat16
I32 = jnp.int32


def _dot_nt(a, b):
    return lax.dot_general(a, b, (((1,), (1,)), ((), ())), preferred_element_type=F32)


def _cparams(sem):
    return pltpu.CompilerParams(dimension_semantics=sem, vmem_limit_bytes=VMEM_LIMIT_BYTES)


def _sort_key(score):
    bits = lax.bitcast_convert_type(score, I32)
    return bits ^ ((bits >> 31) & 0x7FFFFFFF)


def _lane_lo(width=128):
    return lax.broadcasted_iota(I32, (1, width), 1) < 64


def _ln_kernel(x_ref, g_ref, b_ref, o_ref):
    x = x_ref[...]
    mu = jnp.mean(x, axis=-1, keepdims=True)
    xc = x - mu
    var = jnp.mean(xc * xc, axis=-1, keepdims=True)
    o_ref[...] = xc * lax.rsqrt(var + LN_EPS) * g_ref[...] + b_ref[...]


def _layernorm_rows(x, g, b, tm):
    rows, d = x.shape
    return pl.pallas_call(
        _ln_kernel,
        grid=(rows // tm,),
        in_specs=[pl.BlockSpec((tm, d), lambda i: (i, 0)),
                  pl.BlockSpec((1, d), lambda i: (0, 0)),
                  pl.BlockSpec((1, d), lambda i: (0, 0))],
        out_specs=pl.BlockSpec((tm, d), lambda i: (i, 0)),
        out_shape=jax.ShapeDtypeStruct((rows, d), F32),
        compiler_params=_cparams(("arbitrary",)),
        name="ln_in",
    )(x, g.reshape(1, d), b.reshape(1, d))


def _proj_kernel(x_ref, w_ref, p16_ref, p32_ref, ak_ref, av_ref, bk_ref, bv_ref, ik_ref, *, last_out_tile):
    xb = x_ref[...].astype(BF16)
    res = jnp.dot(xb, w_ref[...], preferred_element_type=F32)
    p16_ref[...] = res[:, :W16].astype(p16_ref.dtype)
    p32_ref[...] = res[:, W16:]

    @pl.when(pl.program_id(1) <= last_out_tile)
    def _():
        ak_ref[...] = res[:, C_AK:C_AK + 256]
        av_ref[...] = res[:, C_AV:C_AV + 256]
        bk_ref[...] = res[:, C_BK:C_BK + 128]
        bv_ref[...] = res[:, C_BV:C_BV + 128]
        ik_ref[...] = res[:, C_IK:C_IK + 64]


def _project(x, w, layer, t_out, tm, p16_dtype, name):
    nb, t, d = x.shape
    last = pl.cdiv(t_out, tm) - 1
    row = lambda wd: pl.BlockSpec((None, tm, wd), lambda b, i: (b, i, 0))
    out = lambda wd: pl.BlockSpec((None, tm, wd), lambda b, i: (b, jnp.minimum(i, last), 0))
    shp = lambda wd: jax.ShapeDtypeStruct((nb, t_out, wd), F32)
    return pl.pallas_call(
        functools.partial(_proj_kernel, last_out_tile=last),
        grid=(nb, t // tm),
        in_specs=[row(d), pl.BlockSpec((None, d, NCOL), lambda b, i: (layer, 0, 0))],
        out_specs=[row(W16), row(W32), out(256), out(256), out(128), out(128), out(64)],
        out_shape=[jax.ShapeDtypeStruct((nb, t, W16), p16_dtype),
                   jax.ShapeDtypeStruct((nb, t, W32), F32),
                   shp(256), shp(256), shp(128), shp(128), shp(64)],
        compiler_params=_cparams(("arbitrary", "arbitrary")),
        name=name,
    )(x, w)


def _diff_lambda(lam_ref):
    r = lam_ref[...]
    e1 = jnp.exp(jnp.sum(r[0:1] * r[1:2], axis=1, keepdims=True))
    e2 = jnp.exp(jnp.sum(r[2:3] * r[3:4], axis=1, keepdims=True))
    return e1 - e2 + r[4:5, 0:1]


def _half_select(x, keep_low):
    lo = _lane_lo()
    return jnp.where(lo, x, 0.0) if keep_low else jnp.where(lo, 0.0, x)


def _diff_query_rows(q, n_rows):
    pieces = [q[:, 0:128], q[:, 128:256]]
    return jnp.concatenate([_half_select(p, True) for p in pieces] +
                           [_half_select(p, False) for p in pieces], axis=0).astype(BF16)


def _dsa_query_rows(q):
    out = []
    for h in range(B_HEADS):
        n = h // (B_HEADS // B_KV_HEADS)
        p = q[:, 128 * (h // 2):128 * (h // 2) + 128]
        if h % 2 != n:
            p = pltpu.roll(p, 64, 1)
        out.append(_half_select(p, n == 0))
    return jnp.concatenate(out, axis=0).astype(BF16)


def _dsa_merge_heads(o, r):
    outs = []
    lo = _lane_lo()
    for j in range(B_HEADS // 2):
        n = j // 2
        oe, oo = o[2 * j * r:(2 * j + 1) * r], o[(2 * j + 1) * r:(2 * j + 2) * r]
        if n == 0:
            outs.append(jnp.where(lo, oe, pltpu.roll(oo, 64, 1)))
        else:
            outs.append(jnp.where(lo, pltpu.roll(oe, 64, 1), oo))
    return jnp.concatenate(outs, axis=1)


def _const_rows(values, r):
    return jnp.concatenate([jnp.full((r, 1), v, F32) for v in values], axis=0)


def _count(mask):
    return jnp.sum(jnp.where(mask, 1.0, 0.0), axis=1, keepdims=True)


def _lane_fold(x, op):
    acc = x[:, 0:128]
    for c in range(1, x.shape[1] // 128):
        acc = op(acc, x[:, 128 * c:128 * (c + 1)])
    return acc


def _store_scores(j, qk, shifts, tq, s_ref, m_ref):
    for r, shift in enumerate(shifts):
        rows = slice(r * tq, (r + 1) * tq)
        s = qk[rows] + shift
        s_ref[j, rows, :] = s
        m_ref[rows, :] = jnp.maximum(m_ref[rows, :], _lane_fold(s, jnp.maximum))


def _softmax_weigh(n_chunks, tc, rows, s_ref, v_ref, m_ref, acc_ref, lacc_ref):
    m = jnp.broadcast_to(jnp.max(m_ref[...], axis=1, keepdims=True), (rows, 128))
    acc_ref[...] = jnp.zeros_like(acc_ref)
    lacc_ref[...] = jnp.zeros_like(lacc_ref)

    def step(j, _):
        s = s_ref[j]
        p = jnp.concatenate([jnp.exp(s[:, 128 * c:128 * (c + 1)] - m) for c in range(tc // 128)], axis=1)
        lacc_ref[...] += _lane_fold(p, jnp.add)
        acc_ref[...] += jnp.dot(p.astype(BF16), v_ref[pl.ds(pl.multiple_of(j * tc, tc), tc), :],
                                preferred_element_type=F32)
        return 0

    lax.fori_loop(0, n_chunks, step, 0)
    return acc_ref[...] / jnp.sum(lacc_ref[...], axis=1, keepdims=True)


def _diff_prompt_kernel(lam_ref, q_ref, k_ref, v_ref, o_ref, s_ref, m_ref, acc_ref, lacc_ref, *, n_real_tiles):
    n = pl.program_id(1)
    i = pl.program_id(2)
    tq, tc = DIFF_Q_TILE, KV_CHUNK

    @pl.when(i >= n_real_tiles)
    def _():
        o_ref[...] = jnp.zeros_like(o_ref)

    @pl.when(i < n_real_tiles)
    def _():
        lam = _diff_lambda(lam_ref)
        qst = _diff_query_rows(q_ref[...].astype(F32), tq)
        slope_g = (jnp.where(n == 0, 2.0 ** -2, 2.0 ** -6), jnp.where(n == 0, 2.0 ** -4, 2.0 ** -8))
        qpos = i * tq + lax.broadcasted_iota(I32, (tq, 1), 0)
        col = lax.broadcasted_iota(I32, (1, tc), 1)
        m_ref[...] = jnp.full_like(m_ref, NEG_INF)

        def score_step(j, masked):
            k0 = pl.multiple_of(j * tc, tc)
            qk = _dot_nt(qst, k_ref[pl.ds(k0, tc), :])
            kpos = (col + k0).astype(F32)
            shifts = [slope_g[r % 2] * kpos for r in range(4)]
            if masked:
                causal = jnp.where(col + k0 <= qpos, 0.0, NEG_INF)
                shifts = [sh + causal for sh in shifts[:2]] * 2
            _store_scores(j, qk, shifts, tq, s_ref, m_ref)

        n_full = (i * tq) // tc

        def full_step(j, _):
            score_step(j, False)
            return 0

        lax.fori_loop(0, n_full, full_step, 0)
        score_step(n_full, True)
        o = _softmax_weigh(n_full + 1, tc, 4 * tq, s_ref, v_ref, m_ref, acc_ref, lacc_ref)
        out_g0 = o[0:tq] - lam * o[2 * tq:3 * tq]
        out_g1 = o[tq:2 * tq] - lam * o[3 * tq:4 * tq]
        o_ref[...] = jnp.concatenate([out_g0, out_g1], axis=1)


def _diff_prompt(p16, lam, t_real):
    nb, tpad, _ = p16.shape
    tq = DIFF_Q_TILE
    kern = functools.partial(_diff_prompt_kernel, n_real_tiles=pl.cdiv(t_real, tq))
    return pl.pallas_call(
        kern,
        grid=(nb, A_KV_HEADS, tpad // tq),
        in_specs=[pl.BlockSpec((8, 128), lambda b, n, i: (0, 0)),
                  pl.BlockSpec((None, tq, 256), lambda b, n, i: (b, i, C_AQ // 256 + n)),
                  pl.BlockSpec((None, tpad, 128), lambda b, n, i: (b, 0, C_AK // 128 + n)),
                  pl.BlockSpec((None, tpad, 128), lambda b, n, i: (b, 0, C_AV // 128 + n))],
        out_specs=pl.BlockSpec((None, tq, 256), lambda b, n, i: (b, i, n)),
        out_shape=jax.ShapeDtypeStruct((nb, tpad, 512), F32),
        scratch_shapes=[pltpu.VMEM((tpad // KV_CHUNK, 4 * tq, KV_CHUNK), F32), pltpu.VMEM((4 * tq, 128), F32),
                        pltpu.VMEM((4 * tq, 128), F32), pltpu.VMEM((4 * tq, 128), F32)],
        compiler_params=_cparams(("arbitrary", "arbitrary", "arbitrary")),
        name="diff_prompt",
    )(lam, p16, p16, p16)


def _dsa_prompt_kernel(bq_ref, iq_ref, iw_ref, bk_ref, bv_ref, ik_ref, o_ref,
                       keys_ref, keys_t_ref, mask_ref, s_ref, m_ref, acc_ref, lacc_ref, *,
                       n_real_tiles, topk, idx_bits):
    i = pl.program_id(1)
    tq, tc = Q_TILE, DSA_CHUNK

    @pl.when(i >= n_real_tiles)
    def _():
        o_ref[...] = jnp.zeros_like(o_ref)

    @pl.when(i < n_real_tiles)
    def _():
        n_chunks = (i * tq) // tc + 1
        col = lax.broadcasted_iota(I32, (1, tc), 1)
        qpos = i * tq + lax.broadcasted_iota(I32, (tq, 1), 0)

        iq = iq_ref[...].astype(F32)
        iqst = jnp.concatenate(
            [_half_select(iq[:, 128 * (h // 2):128 * (h // 2) + 128], h % 2 == 0)
             for h in range(IDX_HEADS)], axis=0).astype(BF16)
        w = iw_ref[...]
        wcol = [jnp.broadcast_to(w[:, h:h + 1], (tq, tc)) for h in range(IDX_HEADS)]

        def index_step(j, _):
            k0 = pl.multiple_of(j * tc, tc)
            logits = _dot_nt(iqst, ik_ref[pl.ds(k0, tc), :])
            sc = jnp.maximum(logits[0:tq], 0.0) * wcol[0]
            for h in range(1, IDX_HEADS):
                sc = sc + jnp.maximum(logits[h * tq:(h + 1) * tq], 0.0) * wcol[h]
            key = jnp.where(col + k0 <= qpos, _sort_key(sc * IDX_SCALE), INT_MIN)
            keys_ref[j] = key
            keys_t_ref[j] = key.T
            return 0

        lax.fori_loop(0, n_chunks, index_step, 0)

        kidx = lax.broadcasted_iota(I32, (tc, 1), 0)

        def count_over_chunks(pred):
            def body(j, acc):
                hit = jnp.where(pred(keys_t_ref[j], kidx + j * tc), 1.0, 0.0)
                parts = [hit[8 * r:8 * (r + 1)] for r in range(tc // 8)]
                while len(parts) > 1:
                    parts = [parts[r] + parts[r + 1] for r in range(0, len(parts), 2)]
                return acc + parts[0]
            acc = lax.fori_loop(0, n_chunks, body, jnp.zeros((8, tq), F32))
            return jnp.sum(acc, axis=0, keepdims=True)

        def bit_step(b, t):
            cand = t + lax.shift_left(jnp.int32(1), 31 - b)
            cnt = count_over_chunks(lambda kc, _: kc >= cand)
            return jnp.where(cnt >= topk, cand, t)

        t = lax.fori_loop(0, 32, bit_step, jnp.full((1, tq), INT_MIN, I32))
        thr_t = jnp.maximum(t, INT_MIN + 1)
        need = topk - count_over_chunks(lambda kc, _: kc > thr_t)
        n_eq = count_over_chunks(lambda kc, _: kc == thr_t)
        excess = n_eq > need

        def tie_bound():
            def tie_step(b, jb):
                cand = jb + lax.shift_left(jnp.int32(1), idx_bits - 1 - b)
                cnt = count_over_chunks(lambda kc, idx: (kc == thr_t) & (idx < cand))
                return jnp.where(cnt < need, cand, jb)
            return lax.fori_loop(0, idx_bits, tie_step, jnp.zeros((1, tq), I32))

        any_excess = jnp.max(jnp.where(excess, 1.0, 0.0)) > 0.0
        big = jnp.full((1, tq), 2 ** 30, I32)
        bound_t = lax.cond(any_excess, lambda: jnp.where(excess, tie_bound(), big), lambda: big)
        to_col = lambda v: jnp.broadcast_to(v, (tq, tq)).T[:, 0:1]
        thr, bound = to_col(thr_t), to_col(bound_t)

        qst = _dsa_query_rows(bq_ref[...].astype(F32))

        m_ref[...] = jnp.full_like(m_ref, NEG_INF)

        def score_step(j, _):
            k0 = pl.multiple_of(j * tc, tc)
            kc = keys_ref[j]
            sel = (kc > thr) | ((kc == thr) & (col + k0 <= bound))
            mask_ref[...] = jnp.where(sel, 0.0, NEG_INF)
            qk = _dot_nt(qst, bk_ref[pl.ds(k0, tc), :])
            kpos = (col + k0).astype(F32)
            _store_scores(j, qk, [2.0 ** -(h + 1) * kpos + mask_ref[...] for h in range(B_HEADS)],
                          tq, s_ref, m_ref)
            return 0

        lax.fori_loop(0, n_chunks, score_step, 0)
        o = _softmax_weigh(n_chunks, tc, B_HEADS * tq, s_ref, bv_ref, m_ref, acc_ref, lacc_ref)
        o_ref[...] = _dsa_merge_heads(o, tq)


def _dsa_prompt(p16, p32, t_real, topk):
    nb, tpad, _ = p16.shape
    nq = tpad // Q_TILE
    kern = functools.partial(_dsa_prompt_kernel, n_real_tiles=pl.cdiv(t_real, Q_TILE), topk=float(topk),
                             idx_bits=max(1, int(tpad).bit_length()))
    whole = lambda c: pl.BlockSpec((None, tpad, 128), lambda b, i: (b, 0, c // 128))
    return pl.pallas_call(
        kern,
        grid=(nb, nq),
        in_specs=[pl.BlockSpec((None, Q_TILE, 512), lambda b, i: (b, i, C_BQ // 512)),
                  pl.BlockSpec((None, Q_TILE, 512), lambda b, i: (b, i, C_IQ // 512)),
                  pl.BlockSpec((None, Q_TILE, 128), lambda b, i: (b, i, C_IW // 128)),
                  whole(C_BK), whole(C_BV), whole(C_IK)],
        out_specs=pl.BlockSpec((None, Q_TILE, 512), lambda b, i: (b, i, 0)),
        out_shape=jax.ShapeDtypeStruct((nb, tpad, 512), F32),
        scratch_shapes=[pltpu.VMEM((tpad // DSA_CHUNK, Q_TILE, DSA_CHUNK), I32),
                        pltpu.VMEM((tpad // DSA_CHUNK, DSA_CHUNK, Q_TILE), I32),
                        pltpu.VMEM((Q_TILE, DSA_CHUNK), F32),
                        pltpu.VMEM((tpad // DSA_CHUNK, B_HEADS * Q_TILE, DSA_CHUNK), F32),
                        pltpu.VMEM((B_HEADS * Q_TILE, 128), F32), pltpu.VMEM((B_HEADS * Q_TILE, 128), F32),
                        pltpu.VMEM((B_HEADS * Q_TILE, 128), F32)],
        compiler_params=_cparams(("arbitrary", "arbitrary")),
        name="dsa_prompt",
    )(p16, p16, p32, p16, p16, p16)


def _merge_kernel(sc_ref, x_ref, ya_ref, yb_ref, az_ref, bz_ref, w_ref, sg_ref, g_ref, b_ref, o_ref, *, alpha):
    one_minus_lam_init = sc_ref[5:6, 0:1]
    ya = ya_ref[...]
    parts = []
    for h in range(A_HEADS):
        o = ya[:, 128 * h:128 * h + 128]
        y = o * lax.rsqrt(jnp.mean(o * o, axis=-1, keepdims=True) + LN_EPS) * sg_ref[...]
        parts.append(y * one_minus_lam_init)
    az = az_ref[...]
    bz = bz_ref[...]
    ya_g = jnp.concatenate(parts, axis=1) * (az * (1.0 / (1.0 + jnp.exp(-az))))
    yb_g = yb_ref[...] * (bz * (1.0 / (1.0 + jnp.exp(-bz))))
    mix = jnp.concatenate([ya_g, yb_g], axis=1).astype(BF16)
    y = alpha * x_ref[...] + jnp.dot(mix, w_ref[...], preferred_element_type=F32)
    mu = jnp.mean(y, axis=-1, keepdims=True)
    yc = y - mu
    var = jnp.mean(yc * yc, axis=-1, keepdims=True)
    o_ref[...] = yc * lax.rsqrt(var + LN_EPS) * g_ref[...] + b_ref[...]


def _merge(x, ya, yb, p32, w_out, layer, sc, sub_g, ln_g, ln_b, alpha, tm, name):
    nb, t, d = x.shape
    row = lambda wd, c: pl.BlockSpec((None, tm, wd), lambda b, i: (b, i, c))
    vec = lambda wd: pl.BlockSpec((1, wd), lambda b, i: (0, 0))
    return pl.pallas_call(
        functools.partial(_merge_kernel, alpha=alpha),
        grid=(nb, t // tm),
        in_specs=[pl.BlockSpec((8, 128), lambda b, i: (0, 0)),
                  row(d, 0), row(512, 0), row(512, 0), row(512, C_AZ // 512), row(512, C_BZ // 512),
                  pl.BlockSpec((None, d, d), lambda b, i: (layer, 0, 0)),
                  vec(128), vec(d), vec(d)],
        out_specs=row(d, 0),
        out_shape=jax.ShapeDtypeStruct((nb, t, d), F32),
        compiler_params=_cparams(("arbitrary", "arbitrary")),
        name=name,
    )(sc, x, ya, yb, p32, p32, w_out, sub_g.reshape(1, 128), ln_g.reshape(1, d), ln_b.reshape(1, d))


def _pad_rows(x, rows):
    return jnp.concatenate([x, jnp.zeros((rows - x.shape[0], x.shape[1]), x.dtype)], axis=0)


def _diff_sample_kernel(pt_ref, lam_ref, q_ref, kn_ref, vn_ref, *rest, n_pg, past):
    k_refs, v_refs = rest[:n_pg], rest[n_pg:2 * n_pg]
    o_ref, m_ref, l_ref, acc_ref = rest[2 * n_pg:]
    pg = pl.program_id(1)
    ds = q_ref.shape[0]
    rows = 2 * A_KV_HEADS * 2 * ds
    span = n_pg * PAGE_SIZE

    @pl.when(pg == 0)
    def _():
        m_ref[...] = jnp.full_like(m_ref, NEG_INF)
        l_ref[...] = jnp.zeros_like(l_ref)
        acc_ref[...] = jnp.zeros_like(acc_ref)

    q = q_ref[...]
    qst = jnp.concatenate([_diff_query_rows(q[:, 256 * n:256 * n + 256], ds) for n in range(A_KV_HEADS)], axis=0)
    slope = _const_rows([2.0 ** (-2 * (2 * n + g + 1)) for n in range(A_KV_HEADS) for _c in range(2)
                         for g in range(2)], ds)
    half = rows // A_KV_HEADS

    def attend(score, weigh, width, k0, mask):
        col = lax.broadcasted_iota(I32, (1, width), 1)
        s = jnp.concatenate([score(n, qst[half * n:half * (n + 1)]) for n in range(A_KV_HEADS)],
                            axis=0) + slope * col.astype(F32)
        if mask is not None:
            s = jnp.where(mask(col), s, NEG_INF)
        cj = slope * k0
        m, l, acc = m_ref[...][:, 0:1], l_ref[...][:, 0:1], acc_ref[...]
        mt = jnp.max(s, axis=1, keepdims=True)
        m_new = jnp.maximum(m, mt + cj)
        p = jnp.exp(s - (m_new - cj))
        alpha = jnp.exp(m - m_new)
        l = alpha * l + jnp.sum(p, axis=1, keepdims=True)
        pv = jnp.concatenate([weigh(n, p[half * n:half * (n + 1)].astype(BF16)) for n in range(A_KV_HEADS)],
                             axis=0)
        m_ref[...] = jnp.broadcast_to(m_new, m_ref.shape)
        l_ref[...] = jnp.broadcast_to(l, l_ref.shape)
        acc_ref[...] = alpha * acc + pv

    def past_score(n, q):
        kt = jnp.concatenate([r[128 * n:128 * n + 128, :] for r in k_refs], axis=1).astype(BF16)
        return jnp.dot(q, kt, preferred_element_type=F32)

    def past_weigh(n, p):
        v = jnp.concatenate([r[pl.ds(n, PAGE_SIZE, stride=A_KV_HEADS), :] for r in v_refs], axis=0).astype(BF16)
        return jnp.dot(p, v, preferred_element_type=F32)

    attend(past_score, past_weigh, span, (pg * span).astype(F32), None)

    @pl.when(pg == past // span - 1)
    def _():
        r = lax.broadcasted_iota(I32, (rows, 1), 0) & (ds - 1)
        kn = _pad_rows(kn_ref[...], 128).astype(BF16)
        vn = _pad_rows(vn_ref[...], 128).astype(BF16)
        attend(lambda n, q: _dot_nt(q, kn[:, 128 * n:128 * n + 128]),
               lambda n, p: jnp.dot(p, vn[:, 128 * n:128 * n + 128], preferred_element_type=F32),
               128, jnp.float32(past), lambda col: col <= r)
        lam = _diff_lambda(lam_ref)
        o = acc_ref[...] / l_ref[...][:, 0:1]
        outs = []
        for n in range(A_KV_HEADS):
            base = half * n
            for g in range(2):
                outs.append(o[base + g * ds:base + (g + 1) * ds]
                            - lam * o[base + (2 + g) * ds:base + (3 + g) * ds])
        o_ref[...] = jnp.concatenate(outs, axis=1)


def _diff_sample(p16s, lam, cache_k, cache_v, pt_flat, layer, n_pages):
    db, ds, _ = p16s.shape
    n_pg = PAGES_PER_STEP
    rows = 2 * A_KV_HEADS * 2 * ds
    past = n_pages * PAGE_SIZE
    kern = functools.partial(_diff_sample_kernel, n_pg=n_pg, past=past)

    def page(r):
        return pl.BlockSpec((None, None, 256, PAGE_SIZE),
                            lambda sb, pg, pt: (layer, pt[sb * n_pages + pg * n_pg + r], 0, 0))

    new = lambda c: pl.BlockSpec((None, ds, 256), lambda sb, pg, pt: (sb, 0, c // 256))
    grid_spec = pltpu.PrefetchScalarGridSpec(
        num_scalar_prefetch=1,
        grid=(db, n_pages // n_pg),
        in_specs=[pl.BlockSpec((8, 128), lambda sb, pg, pt: (0, 0)),
                  pl.BlockSpec((None, ds, 512), lambda sb, pg, pt: (sb, 0, C_AQ // 512)),
                  new(C_AK), new(C_AV)] + [page(r) for r in range(n_pg)] * 2,
        out_specs=pl.BlockSpec((None, ds, 512), lambda sb, pg, pt: (sb, 0, 0)),
        scratch_shapes=[pltpu.VMEM((rows, 128), F32), pltpu.VMEM((rows, 128), F32),
                        pltpu.VMEM((rows, 128), F32)])
    return pl.pallas_call(
        kern, grid_spec=grid_spec,
        out_shape=jax.ShapeDtypeStruct((db, ds, 512), F32),
        compiler_params=_cparams(("arbitrary", "arbitrary")),
        name="diff_sample",
    )(pt_flat, lam, p16s, p16s, p16s, *([cache_k] * n_pg), *([cache_v] * n_pg))


def _dsa_sample_kernel(pt_ref, bq_ref, iq_ref, iw_ref, kn_ref, vn_ref, in_ref, *rest, n_pg, past, topk, idx_bits):
    i_refs, k_refs, v_refs = rest[:n_pg], rest[n_pg:2 * n_pg], rest[2 * n_pg:3 * n_pg]
    o_ref, keys_ref, keys_new_ref, thr_ref, bound_ref, m_ref, l_ref, acc_ref = rest[3 * n_pg:]
    phase, sb, g = pl.program_id(0), pl.program_id(1), pl.program_id(2)
    ds = bq_ref.shape[0]
    rows = B_HEADS * ds
    span = n_pg * PAGE_SIZE
    n_groups = past // span
    cw = keys_ref.shape[2]
    n_sub = span // cw
    all_rows = keys_ref.shape[1]
    row0 = pl.multiple_of(sb * ds, ds)
    r_id = lax.broadcasted_iota(I32, (ds, 1), 0)

    def index_keys(ik, transposed):
        iq = iq_ref[...]
        heads = []
        for h in range(IDX_HEADS):
            p = iq[:, 128 * (h // 2):128 * (h // 2) + 128]
            if h % 2:
                p = pltpu.roll(p, 64, 1)
            heads.append(p[:, 0:64])
        iqst = jnp.concatenate(heads, axis=0).astype(BF16)
        if transposed:
            logits = jnp.dot(iqst, ik.astype(BF16), preferred_element_type=F32)
        else:
            logits = _dot_nt(iqst, ik.astype(BF16))
        w = iw_ref[...]
        sc = jnp.maximum(logits[0:ds], 0.0) * w[:, 0:1]
        for h in range(1, IDX_HEADS):
            sc = sc + jnp.maximum(logits[h * ds:(h + 1) * ds], 0.0) * w[:, h:h + 1]
        return _sort_key(sc * IDX_SCALE)

    @pl.when(phase == 0)
    def _():
        ik = jnp.concatenate([r[...] for r in i_refs], axis=1)
        keys = index_keys(ik, True)
        for c in range(n_sub):
            keys_ref[g * n_sub + c, pl.ds(row0, ds), :] = keys[:, c * cw:(c + 1) * cw]

    @pl.when((phase == 0) & (g == n_groups - 1))
    def _():
        col = lax.broadcasted_iota(I32, (1, 128), 1)
        key_new = index_keys(_pad_rows(in_ref[...][:, 0:64], 128), False)
        keys_new_ref[pl.ds(row0, ds), :] = jnp.where(col <= r_id, key_new, INT_MIN)

    @pl.when((phase == 1) & (sb == 0) & (g == 0))
    def _():
        col = lax.broadcasted_iota(I32, (1, cw), 1)
        col_new = lax.broadcasted_iota(I32, (1, 128), 1) + past

        def count(pred):
            def body(j, acc):
                return acc + _lane_fold(jnp.where(pred(keys_ref[j], col + j * cw), 1.0, 0.0), jnp.add)
            acc = lax.fori_loop(0, keys_ref.shape[0], body, jnp.zeros((all_rows, 128), F32))
            acc = acc + jnp.where(pred(keys_new_ref[...], col_new), 1.0, 0.0)
            return jnp.sum(acc, axis=1, keepdims=True)

        def bit_step(b, t):
            cand = t + lax.shift_left(jnp.int32(1), 31 - b)
            return jnp.where(count(lambda kc, _: kc >= cand) >= topk, cand, t)

        t = lax.fori_loop(0, 32, bit_step, jnp.full((all_rows, 1), INT_MIN, I32))
        thr = jnp.maximum(t, INT_MIN + 1)
        need = topk - count(lambda kc, _: kc > thr)
        excess = count(lambda kc, _: kc == thr) > need

        def tie_bound():
            def tie_step(b, jb):
                cand = jb + lax.shift_left(jnp.int32(1), idx_bits - 1 - b)
                return jnp.where(count(lambda kc, ix: (kc == thr) & (ix < cand)) < need, cand, jb)
            return lax.fori_loop(0, idx_bits, tie_step, jnp.zeros((all_rows, 1), I32))

        big = jnp.full((all_rows, 1), 2 ** 30, I32)
        any_excess = jnp.max(jnp.where(excess, 1.0, 0.0)) > 0.0
        bound = lax.cond(any_excess, lambda: jnp.where(excess, tie_bound(), big), lambda: big)
        thr_ref[...] = jnp.broadcast_to(thr, thr_ref.shape)
        bound_ref[...] = jnp.broadcast_to(bound, bound_ref.shape)

    qst = _dsa_query_rows(bq_ref[...])
    slope = _const_rows([2.0 ** -(h + 1) for h in range(B_HEADS)], ds)

    def attend(qk, weigh, kc, k0):
        width = qk.shape[1]
        col = lax.broadcasted_iota(I32, (1, width), 1)
        thr, bound = thr_ref[pl.ds(row0, ds), :][:, 0:1], bound_ref[pl.ds(row0, ds), :][:, 0:1]
        sel = (kc > thr) | ((kc == thr) & (col + k0 <= bound))
        s = (qk + slope * col.astype(F32)).reshape(B_HEADS, ds, width)
        s = jnp.where(sel[None], s, NEG_INF).reshape(rows, width)
        cj = slope * k0.astype(F32)
        m, l, acc = m_ref[...][:, 0:1], l_ref[...][:, 0:1], acc_ref[...]
        mt = jnp.max(s, axis=1, keepdims=True)
        m_new = jnp.maximum(m, mt + cj)
        m_safe = jnp.where(m_new == NEG_INF, 0.0, m_new)
        p = jnp.exp(s - (m_safe - cj))
        alpha = jnp.exp(m - m_safe)
        m_ref[...] = jnp.broadcast_to(m_new, m_ref.shape)
        l_ref[...] = jnp.broadcast_to(alpha * l + jnp.sum(p, axis=1, keepdims=True), l_ref.shape)
        acc_ref[...] = alpha * acc + weigh(p.astype(BF16))

    @pl.when((phase == 1) & (g == 0))
    def _():
        m_ref[...] = jnp.full_like(m_ref, NEG_INF)
        l_ref[...] = jnp.zeros_like(l_ref)
        acc_ref[...] = jnp.zeros_like(acc_ref)

    @pl.when(phase == 1)
    def _():
        kt = jnp.concatenate([r[...] for r in k_refs], axis=1).astype(BF16)
        vt = jnp.concatenate([r[...] for r in v_refs], axis=1).astype(BF16)
        kc = jnp.concatenate([keys_ref[g * n_sub + c, pl.ds(row0, ds), :] for c in range(n_sub)], axis=1)
        attend(jnp.dot(qst, kt, preferred_element_type=F32), lambda p: _dot_nt(p, vt), kc, g * span)

    @pl.when((phase == 1) & (g == n_groups - 1))
    def _():
        kn = _pad_rows(kn_ref[...], 128).astype(BF16)
        vn = _pad_rows(vn_ref[...], 128).astype(BF16)
        attend(_dot_nt(qst, kn), lambda p: jnp.dot(p, vn, preferred_element_type=F32),
               keys_new_ref[pl.ds(row0, ds), :], jnp.int32(past))
        o_ref[...] = _dsa_merge_heads(acc_ref[...] / l_ref[...][:, 0:1], ds)


def _dsa_sample(p16s, p32s, cache_i, cache_k, cache_v, pt_flat, layer, n_pages, topk):
    db, ds, _ = p16s.shape
    n_pg = PAGES_PER_STEP
    n_groups = n_pages // n_pg
    past = n_pages * PAGE_SIZE
    rows = B_HEADS * ds
    kern = functools.partial(_dsa_sample_kernel, n_pg=n_pg, past=past, topk=float(topk),
                             idx_bits=int(past + 128).bit_length())

    def page(r, feat, first_phase):
        def index(ph, sb, g, pt):
            own = pt[sb * n_pages + g * n_pg + r]
            parked = pt[(db - 1) * n_pages + (n_groups - 1) * n_pg + r] if first_phase else pt[r]
            return (layer, jnp.where(ph == (0 if first_phase else 1), own, parked), 0, 0)
        return pl.BlockSpec((None, None, feat, PAGE_SIZE), index)

    new = lambda wd, c: pl.BlockSpec((None, ds, wd), lambda ph, sb, g, pt: (sb, 0, c // wd))
    cw = min(DSA_CHUNK, n_pg * PAGE_SIZE)
    grid_spec = pltpu.PrefetchScalarGridSpec(
        num_scalar_prefetch=1,
        grid=(2, db, n_groups),
        in_specs=[new(512, C_BQ), new(512, C_IQ), new(128, C_IW), new(128, C_BK), new(128, C_BV), new(128, C_IK)]
                 + [page(r, 64, True) for r in range(n_pg)]
                 + [page(r, 128, False) for r in range(n_pg)] * 2,
        out_specs=pl.BlockSpec((None, ds, 512), lambda ph, sb, g, pt: (jnp.where(ph == 0, 0, sb), 0, 0)),
        scratch_shapes=[pltpu.VMEM((past // cw, db * ds, cw), I32), pltpu.VMEM((db * ds, 128), I32),
                        pltpu.VMEM((db * ds, 128), I32), pltpu.VMEM((db * ds, 128), I32),
                        pltpu.VMEM((rows, 128), F32), pltpu.VMEM((rows, 128), F32), pltpu.VMEM((rows, 128), F32)])
    return pl.pallas_call(
        kern, grid_spec=grid_spec,
        out_shape=jax.ShapeDtypeStruct((db, ds, 512), F32),
        compiler_params=_cparams(("arbitrary", "arbitrary", "arbitrary")),
        name="dsa_sample",
    )(pt_flat, p16s, p16s, p32s, p16s, p16s, p16s,
      *([cache_i] * n_pg), *([cache_k] * n_pg), *([cache_v] * n_pg))


def _regroup_weights(w_in):
    sizes = (512, 256, 256, 512, 512, 128, 128, 512, 64, 8, 512)
    offs = [0]
    for s in sizes:
        offs.append(offs[-1] + s)
    aq, ak, av, az, bq, bk, bv, iq, ik, iw, bz = [w_in[..., offs[j]:offs[j + 1]] for j in range(len(sizes))]
    scale = HEAD_DIM ** -0.5
    pad = jnp.zeros(iw.shape[:-1] + (128 - iw.shape[-1],), w_in.dtype)
    return jnp.concatenate([aq * scale, bq * scale, iq, ak, av, bk, bv, ik, ik, az, bz, iw, pad],
                           axis=-1).astype(BF16)


def _pick_tile(n, candidates):
    for c in candidates:
        if n % c == 0:
            return c
    return n


def kernel(x_prompt, x_sample, cache_diff_k, cache_diff_v, cache_dsa_k, cache_dsa_v, cache_idx_k, page_table,
           meta_tokens, ln_in_g, ln_in_b, w_in, w_out, lambda_q1, lambda_k1, lambda_q2, lambda_k2,
           subln_g, ln_g, ln_b):
    nb, seq, d = x_prompt.shape
    db, ds, _ = x_sample.shape
    depth = w_in.shape[0]
    n_pool = cache_diff_k.shape[1]
    n_pages = page_table.shape[1]
    t_real = seq + N_META
    tpad = -(-t_real // DSA_CHUNK) * DSA_CHUNK
    assert ds == 8 and n_pages % PAGES_PER_STEP == 0 and d == 1024
    alpha = (2 * depth) ** 0.25
    topk_p = min(TOPK_MAX, seq // 4)
    topk_s = min(TOPK_MAX, (n_pages * PAGE_SIZE + ds) // 4)

    meta = jnp.broadcast_to(meta_tokens[None].astype(x_prompt.dtype), (nb, N_META, d))
    xp = jnp.concatenate([meta, x_prompt, jnp.zeros((nb, tpad - t_real, d), x_prompt.dtype)], axis=1)
    xp = _layernorm_rows(xp.reshape(nb * tpad, d), ln_in_g, ln_in_b, KV_CHUNK).reshape(nb, tpad, d)
    xs = _layernorm_rows(x_sample.reshape(db * ds, d), ln_in_g, ln_in_b,
                         _pick_tile(db * ds, (256, 128, 8))).reshape(1, db * ds, d)

    w = _regroup_weights(w_in)
    w_o = w_out.astype(BF16)
    pad64 = lambda v: jnp.pad(v, ((0, 0), (0, 128 - v.shape[-1])))
    ck = cache_diff_k.transpose(0, 1, 3, 4, 5, 2).reshape(depth, n_pool, 256, PAGE_SIZE)
    cv = cache_diff_v.reshape(depth, n_pool, PAGE_SIZE * A_KV_HEADS, 128)
    cbk = cache_dsa_k.transpose(0, 1, 3, 4, 2).reshape(depth, n_pool, 128, PAGE_SIZE)
    cbv = cache_dsa_v.transpose(0, 1, 3, 4, 2).reshape(depth, n_pool, 128, PAGE_SIZE)
    ci = cache_idx_k.transpose(0, 1, 3, 2)
    pt_flat = page_table.reshape(-1).astype(I32)
    tm_p = _pick_tile(tpad, (256, 128))
    tm_s = _pick_tile(db * ds, (256, 128, 8))

    p_rows = [[] for _ in range(5)]
    s_rows = [[] for _ in range(5)]
    for layer in range(depth):
        lam_init = 0.8 - 0.6 * math.exp(-0.3 * layer)
        consts = jnp.stack([jnp.full((128,), lam_init, F32), jnp.full((128,), 1.0 - lam_init, F32),
                            jnp.zeros((128,), F32), jnp.zeros((128,), F32)])
        lam = jnp.concatenate([pad64(jnp.stack([lambda_q1[layer], lambda_k1[layer],
                                                 lambda_q2[layer], lambda_k2[layer]])), consts], axis=0)

        p16, p32, *rows_p = _project(xp, w, layer, t_real, tm_p, BF16, "proj_prompt")
        ya = _diff_prompt(p16, lam, t_real)
        yb = _dsa_prompt(p16, p32, t_real, topk_p)
        xp = _merge(xp, ya, yb, p32, w_o, layer, lam, subln_g[layer], ln_g[layer], ln_b[layer],
                    alpha, tm_p, "merge_prompt")

        p16s, p32s, *rows_s = _project(xs, w, layer, db * ds, tm_s, F32, "proj_sample")
        p16s3 = p16s.reshape(db, ds, W16)
        ya_s = _diff_sample(p16s3, lam, ck, cv, pt_flat, layer, n_pages)
        yb_s = _dsa_sample(p16s3, p32s.reshape(db, ds, W32), ci, cbk, cbv, pt_flat, layer,
                           n_pages, topk_s)
        xs = _merge(xs, ya_s.reshape(1, db * ds, 512), yb_s.reshape(1, db * ds, 512), p32s, w_o, layer, lam,
                    subln_g[layer], ln_g[layer], ln_b[layer], alpha, tm_s, "merge_sample")

        for lst, a in zip(p_rows, rows_p):
            lst.append(a)
        for lst, a in zip(s_rows, rows_s):
            lst.append(a)

    pk, pv, pbk, pbv, pik = [jnp.stack(r) for r in p_rows]
    sk, sv, sbk, sbv, sik = [jnp.stack(r).reshape(depth, db, ds, -1) for r in s_rows]
    return (xp[:, N_META:t_real], xs.reshape(db, ds, d),
            pk.reshape(depth, nb, t_real, A_KV_HEADS, 2, HEAD_DIM),
            pv.reshape(depth, nb, t_real, A_KV_HEADS, 2 * HEAD_DIM),
            pbk.reshape(depth, nb, t_real, B_KV_HEADS, HEAD_DIM),
            pbv.reshape(depth, nb, t_real, B_KV_HEADS, HEAD_DIM),
            pik,
            sk.reshape(depth, db, ds, A_KV_HEADS, 2, HEAD_DIM),
            sv.reshape(depth, db, ds, A_KV_HEADS, 2 * HEAD_DIM),
            sbk.reshape(depth, db, ds, B_KV_HEADS, HEAD_DIM),
            sbv.reshape(depth, db, ds, B_KV_HEADS, HEAD_DIM),
            sik)
```

```python
import functools
import math

import jax
import jax.numpy as jnp
from jax import lax
from jax.experimental import pallas as pl
from jax.experimental.pallas import tpu as pltpu

N_META = 16
HEAD_DIM = 64
A_HEADS = 4
A_KV_HEADS = 2
B_HEADS = 8
B_KV_HEADS = 2
IDX_HEADS = 8
IDX_DIM = 64
IDX_SCALE = (IDX_HEADS * IDX_DIM) ** -0.5
TOPK_MAX = 256
PAGE_SIZE = 128
LN_EPS = 1e-5

Q_TILE = 128
DIFF_Q_TILE = 256
KV_CHUNK = 512
DSA_CHUNK = 512
PAGES_PER_STEP = 32
VMEM_LIMIT_BYTES = 56 * 1024 * 1024

LOG2E = math.log2(math.e)
INT_MIN = -(2 ** 31)
NEG_INF = float("-inf")

C_AQ, C_BQ, C_IQ, C_AK, C_AV, C_BK, C_BV, C_IK = 0, 512, 1024, 1536, 1792, 2048, 2176, 2304
W16 = 2432
C_AZ, C_BZ, C_IW = 0, 512, 1024
W32 = 1152
NCOL = W16 + W32

F32 = jnp.float32
BF16 = jnp.bfloat16
I32 = jnp.int32


def _dot_nt(a, b):
    return lax.dot_general(a, b, (((1,), (1,)), ((), ())), preferred_element_type=F32)


def _cparams(sem):
    return pltpu.CompilerParams(dimension_semantics=sem, vmem_limit_bytes=VMEM_LIMIT_BYTES)


def _sort_key(score):
    bits = lax.bitcast_convert_type(score, I32)
    return bits ^ ((bits >> 31) & 0x7FFFFFFF)


def _lane_lo(width=128):
    return lax.broadcasted_iota(I32, (1, width), 1) < 64


def _ln_kernel(x_ref, g_ref, b_ref, o_ref):
    x = x_ref[...]
    mu = jnp.mean(x, axis=-1, keepdims=True)
    xc = x - mu
    var = jnp.mean(xc * xc, axis=-1, keepdims=True)
    o_ref[...] = xc * lax.rsqrt(var + LN_EPS) * g_ref[...] + b_ref[...]


def _layernorm_rows(x, g, b, tm):
    rows, d = x.shape
    return pl.pallas_call(
        _ln_kernel,
        grid=(rows // tm,),
        in_specs=[pl.BlockSpec((tm, d), lambda i: (i, 0)),
                  pl.BlockSpec((1, d), lambda i: (0, 0)),
                  pl.BlockSpec((1, d), lambda i: (0, 0))],
        out_specs=pl.BlockSpec((tm, d), lambda i: (i, 0)),
        out_shape=jax.ShapeDtypeStruct((rows, d), F32),
        compiler_params=_cparams(("arbitrary",)),
        name="ln_in",
    )(x, g.reshape(1, d), b.reshape(1, d))


def _proj_kernel(x_ref, w_ref, p16_ref, p32_ref, ak_ref, av_ref, bk_ref, bv_ref, ik_ref, *, last_out_tile):
    xb = x_ref[...].astype(BF16)
    res = jnp.dot(xb, w_ref[...], preferred_element_type=F32)
    p16_ref[...] = res[:, :W16].astype(p16_ref.dtype)
    p32_ref[...] = res[:, W16:]

    @pl.when(pl.program_id(1) <= last_out_tile)
    def _():
        ak_ref[...] = res[:, C_AK:C_AK + 256]
        av_ref[...] = res[:, C_AV:C_AV + 256]
        bk_ref[...] = res[:, C_BK:C_BK + 128]
        bv_ref[...] = res[:, C_BV:C_BV + 128]
        ik_ref[...] = res[:, C_IK:C_IK + 64]


def _project(x, w, layer, t_out, tm, p16_dtype, name):
    nb, t, d = x.shape
    last = pl.cdiv(t_out, tm) - 1
    row = lambda wd: pl.BlockSpec((None, tm, wd), lambda b, i: (b, i, 0))
    out = lambda wd: pl.BlockSpec((None, tm, wd), lambda b, i: (b, jnp.minimum(i, last), 0))
    shp = lambda wd: jax.ShapeDtypeStruct((nb, t_out, wd), F32)
    return pl.pallas_call(
        functools.partial(_proj_kernel, last_out_tile=last),
        grid=(nb, t // tm),
        in_specs=[row(d), pl.BlockSpec((None, d, NCOL), lambda b, i: (layer, 0, 0))],
        out_specs=[row(W16), row(W32), out(256), out(256), out(128), out(128), out(64)],
        out_shape=[jax.ShapeDtypeStruct((nb, t, W16), p16_dtype),
                   jax.ShapeDtypeStruct((nb, t, W32), F32),
                   shp(256), shp(256), shp(128), shp(128), shp(64)],
        compiler_params=_cparams(("arbitrary", "arbitrary")),
        name=name,
    )(x, w)


def _diff_lambda(lam_ref):
    r = lam_ref[...]
    e1 = jnp.exp(jnp.sum(r[0:1] * r[1:2], axis=1, keepdims=True))
    e2 = jnp.exp(jnp.sum(r[2:3] * r[3:4], axis=1, keepdims=True))
    return e1 - e2 + r[4:5, 0:1]


def _half_select(x, keep_low):
    lo = _lane_lo()
    return jnp.where(lo, x, 0.0) if keep_low else jnp.where(lo, 0.0, x)


def _diff_query_rows(q, n_rows):
    pieces = [q[:, 0:128], q[:, 128:256]]
    return jnp.concatenate([_half_select(p, True) for p in pieces] +
                           [_half_select(p, False) for p in pieces], axis=0).astype(BF16)


def _dsa_query_rows(q):
    out = []
    for h in range(B_HEADS):
        n = h // (B_HEADS // B_KV_HEADS)
        p = q[:, 128 * (h // 2):128 * (h // 2) + 128]
        if h % 2 != n:
            p = pltpu.roll(p, 64, 1)
        out.append(_half_select(p, n == 0))
    return jnp.concatenate(out, axis=0).astype(BF16)


def _dsa_merge_heads(o, r):
    outs = []
    lo = _lane_lo()
    for j in range(B_HEADS // 2):
        n = j // 2
        oe, oo = o[2 * j * r:(2 * j + 1) * r], o[(2 * j + 1) * r:(2 * j + 2) * r]
        if n == 0:
            outs.append(jnp.where(lo, oe, pltpu.roll(oo, 64, 1)))
        else:
            outs.append(jnp.where(lo, pltpu.roll(oe, 64, 1), oo))
    return jnp.concatenate(outs, axis=1)


def _const_rows(values, r):
    return jnp.concatenate([jnp.full((r, 1), v, F32) for v in values], axis=0)


def _count(mask):
    return jnp.sum(jnp.where(mask, 1.0, 0.0), axis=1, keepdims=True)


def _lane_fold(x, op):
    acc = x[:, 0:128]
    for c in range(1, x.shape[1] // 128):
        acc = op(acc, x[:, 128 * c:128 * (c + 1)])
    return acc


def _for_chunks_paired(n, body):
    def pair(jj, _):
        body(2 * jj, 0)
        body(2 * jj + 1, 1)
        return 0

    lax.fori_loop(0, n // 2, pair, 0)

    @pl.when(n % 2 == 1)
    def _():
        body(n - 1, 0)


def _store_scores(j, qk, shifts, tq, s_ref, m_ref):
    for r, shift in enumerate(shifts):
        rows = slice(r * tq, (r + 1) * tq)
        s = qk[rows] * LOG2E + shift
        s_ref[j, rows, :] = s
        m_ref[rows, :] = jnp.maximum(m_ref[rows, :], _lane_fold(s, jnp.maximum))


def _softmax_weigh(n_chunks, tc, rows, s_ref, v_ref, m_ref, acc_ref, lacc_ref):
    m = jnp.broadcast_to(jnp.max(m_ref[...], axis=1, keepdims=True), (rows, 128))
    acc_ref[...] = jnp.zeros_like(acc_ref)
    lacc_ref[...] = jnp.zeros_like(lacc_ref)

    def weigh(j):
        s = s_ref[j]
        p = jnp.concatenate([jnp.exp2(s[:, 128 * c:128 * (c + 1)] - m) for c in range(tc // 128)], axis=1)
        pv = jnp.dot(p.astype(BF16), v_ref[pl.ds(pl.multiple_of(j * tc, tc), tc), :], preferred_element_type=F32)
        return _lane_fold(p, jnp.add), pv

    def step(j, _slot):
        l, pv = weigh(j)
        lacc_ref[...] += l
        acc_ref[...] += pv

    _for_chunks_paired(n_chunks, step)
    return acc_ref[...] / jnp.sum(lacc_ref[...], axis=1, keepdims=True)


def _diff_prompt_kernel(lam_ref, q_ref, k_ref, v_ref, o_ref, s_ref, m_ref, acc_ref, lacc_ref, *, n_real_tiles):
    n = pl.program_id(1)
    i = pl.program_id(2)
    tq, tc = DIFF_Q_TILE, KV_CHUNK

    @pl.when(i >= n_real_tiles)
    def _():
        o_ref[...] = jnp.zeros_like(o_ref)

    @pl.when(i < n_real_tiles)
    def _():
        lam = _diff_lambda(lam_ref)
        qst = _diff_query_rows(q_ref[...].astype(F32), tq)
        slope_g = (jnp.where(n == 0, 2.0 ** -2, 2.0 ** -6), jnp.where(n == 0, 2.0 ** -4, 2.0 ** -8))
        qpos = i * tq + lax.broadcasted_iota(I32, (tq, 1), 0)
        col = lax.broadcasted_iota(I32, (1, tc), 1)
        m_ref[...] = jnp.full_like(m_ref, NEG_INF)

        def score_step(j, masked):
            k0 = pl.multiple_of(j * tc, tc)
            qk = _dot_nt(qst, k_ref[pl.ds(k0, tc), :])
            kpos = (col + k0).astype(F32)
            shifts = [(slope_g[r % 2] * LOG2E) * kpos for r in range(4)]
            if masked:
                causal = jnp.where(col + k0 <= qpos, 0.0, NEG_INF)
                shifts = [sh + causal for sh in shifts[:2]] * 2
            _store_scores(j, qk, shifts, tq, s_ref, m_ref)

        n_full = (i * tq) // tc
        _for_chunks_paired(n_full, lambda j, _slot: score_step(j, False))
        score_step(n_full, True)
        o = _softmax_weigh(n_full + 1, tc, 4 * tq, s_ref, v_ref, m_ref, acc_ref, lacc_ref)
        out_g0 = o[0:tq] - lam * o[2 * tq:3 * tq]
        out_g1 = o[tq:2 * tq] - lam * o[3 * tq:4 * tq]
        o_ref[...] = jnp.concatenate([out_g0, out_g1], axis=1)


def _diff_prompt(p16, lam, t_real):
    nb, tpad, _ = p16.shape
    tq = DIFF_Q_TILE
    kern = functools.partial(_diff_prompt_kernel, n_real_tiles=pl.cdiv(t_real, tq))
    return pl.pallas_call(
        kern,
        grid=(nb, A_KV_HEADS, tpad // tq),
        in_specs=[pl.BlockSpec((8, 128), lambda b, n, i: (0, 0)),
                  pl.BlockSpec((None, tq, 256), lambda b, n, i: (b, i, C_AQ // 256 + n)),
                  pl.BlockSpec((None, tpad, 128), lambda b, n, i: (b, 0, C_AK // 128 + n)),
                  pl.BlockSpec((None, tpad, 128), lambda b, n, i: (b, 0, C_AV // 128 + n))],
        out_specs=pl.BlockSpec((None, tq, 256), lambda b, n, i: (b, i, n)),
        out_shape=jax.ShapeDtypeStruct((nb, tpad, 512), F32),
        scratch_shapes=[pltpu.VMEM((tpad // KV_CHUNK, 4 * tq, KV_CHUNK), F32), pltpu.VMEM((4 * tq, 128), F32),
                        pltpu.VMEM((4 * tq, 128), F32), pltpu.VMEM((4 * tq, 128), F32)],
        compiler_params=_cparams(("arbitrary", "arbitrary", "arbitrary")),
        name="diff_prompt",
    )(lam, p16, p16, p16)


def _dsa_prompt_kernel(bq_ref, iq_ref, iw_ref, bk_ref, bv_ref, ik_ref, o_ref,
                       keys_ref, keys_t_ref, mask_ref, s_ref, m_ref, acc_ref, lacc_ref, *,
                       n_real_tiles, topk, idx_bits):
    i = pl.program_id(1)
    tq, tc = Q_TILE, DSA_CHUNK

    @pl.when(i >= n_real_tiles)
    def _():
        o_ref[...] = jnp.zeros_like(o_ref)

    @pl.when(i < n_real_tiles)
    def _():
        n_chunks = (i * tq) // tc + 1
        col = lax.broadcasted_iota(I32, (1, tc), 1)
        qpos = i * tq + lax.broadcasted_iota(I32, (tq, 1), 0)

        iq = iq_ref[...].astype(F32)
        iqst = jnp.concatenate(
            [_half_select(iq[:, 128 * (h // 2):128 * (h // 2) + 128], h % 2 == 0)
             for h in range(IDX_HEADS)], axis=0).astype(BF16)
        w = iw_ref[...]
        wcol = [jnp.broadcast_to(w[:, h:h + 1], (tq, tc)) for h in range(IDX_HEADS)]

        def index_step(j, _slot):
            k0 = pl.multiple_of(j * tc, tc)
            logits = _dot_nt(iqst, ik_ref[pl.ds(k0, tc), :])
            sc = jnp.maximum(logits[0:tq], 0.0) * wcol[0]
            for h in range(1, IDX_HEADS):
                sc = sc + jnp.maximum(logits[h * tq:(h + 1) * tq], 0.0) * wcol[h]
            key = jnp.where(col + k0 <= qpos, _sort_key(sc * IDX_SCALE), INT_MIN)
            keys_ref[j] = key
            keys_t_ref[j] = key.T

        _for_chunks_paired(n_chunks, index_step)

        kidx = lax.broadcasted_iota(I32, (tc, 1), 0)

        def count_over_chunks(pred):
            def body(j, acc):
                hit = jnp.where(pred(keys_t_ref[j], kidx + j * tc), 1.0, 0.0)
                parts = [hit[8 * r:8 * (r + 1)] for r in range(tc // 8)]
                while len(parts) > 1:
                    parts = [parts[r] + parts[r + 1] for r in range(0, len(parts), 2)]
                return acc + parts[0]
            acc = lax.fori_loop(0, n_chunks, body, jnp.zeros((8, tq), F32))
            return jnp.sum(acc, axis=0, keepdims=True)

        def bit_step(b, t):
            cand = t + lax.shift_left(jnp.int32(1), 31 - b)
            cnt = count_over_chunks(lambda kc, _: kc >= cand)
            return jnp.where(cnt >= topk, cand, t)

        t = lax.fori_loop(0, 32, bit_step, jnp.full((1, tq), INT_MIN, I32))
        thr_t = jnp.maximum(t, INT_MIN + 1)
        need = topk - count_over_chunks(lambda kc, _: kc > thr_t)
        n_eq = count_over_chunks(lambda kc, _: kc == thr_t)
        excess = n_eq > need

        def tie_bound():
            def tie_step(b, jb):
                cand = jb + lax.shift_left(jnp.int32(1), idx_bits - 1 - b)
                cnt = count_over_chunks(lambda kc, idx: (kc == thr_t) & (idx < cand))
                return jnp.where(cnt < need, cand, jb)
            return lax.fori_loop(0, idx_bits, tie_step, jnp.zeros((1, tq), I32))

        any_excess = jnp.max(jnp.where(excess, 1.0, 0.0)) > 0.0
        big = jnp.full((1, tq), 2 ** 30, I32)
        bound_t = lax.cond(any_excess, lambda: jnp.where(excess, tie_bound(), big), lambda: big)
        to_col = lambda v: jnp.broadcast_to(v, (tq, tq)).T[:, 0:1]
        thr, bound = to_col(thr_t), to_col(bound_t)

        qst = _dsa_query_rows(bq_ref[...].astype(F32))

        m_ref[...] = jnp.full_like(m_ref, NEG_INF)

        def score_step(j, slot):
            k0 = pl.multiple_of(j * tc, tc)
            kc = keys_ref[j]
            sel = (kc > thr) | ((kc == thr) & (col + k0 <= bound))
            mask_ref[slot] = jnp.where(sel, 0.0, NEG_INF)
            qk = _dot_nt(qst, bk_ref[pl.ds(k0, tc), :])
            kpos = (col + k0).astype(F32)
            _store_scores(j, qk, [(2.0 ** -(h + 1) * LOG2E) * kpos + mask_ref[slot] for h in range(B_HEADS)],
                          tq, s_ref, m_ref)

        _for_chunks_paired(n_chunks, score_step)
        o = _softmax_weigh(n_chunks, tc, B_HEADS * tq, s_ref, bv_ref, m_ref, acc_ref, lacc_ref)
        o_ref[...] = _dsa_merge_heads(o, tq)


def _dsa_prompt(p16, p32, t_real, topk):
    nb, tpad, _ = p16.shape
    nq = tpad // Q_TILE
    kern = functools.partial(_dsa_prompt_kernel, n_real_tiles=pl.cdiv(t_real, Q_TILE), topk=float(topk),
                             idx_bits=max(1, int(tpad).bit_length()))
    whole = lambda c: pl.BlockSpec((None, tpad, 128), lambda b, i: (b, 0, c // 128))
    return pl.pallas_call(
        kern,
        grid=(nb, nq),
        in_specs=[pl.BlockSpec((None, Q_TILE, 512), lambda b, i: (b, i, C_BQ // 512)),
                  pl.BlockSpec((None, Q_TILE, 512), lambda b, i: (b, i, C_IQ // 512)),
                  pl.BlockSpec((None, Q_TILE, 128), lambda b, i: (b, i, C_IW // 128)),
                  whole(C_BK), whole(C_BV), whole(C_IK)],
        out_specs=pl.BlockSpec((None, Q_TILE, 512), lambda b, i: (b, i, 0)),
        out_shape=jax.ShapeDtypeStruct((nb, tpad, 512), F32),
        scratch_shapes=[pltpu.VMEM((tpad // DSA_CHUNK, Q_TILE, DSA_CHUNK), I32),
                        pltpu.VMEM((tpad // DSA_CHUNK, DSA_CHUNK, Q_TILE), I32),
                        pltpu.VMEM((2, Q_TILE, DSA_CHUNK), F32),
                        pltpu.VMEM((tpad // DSA_CHUNK, B_HEADS * Q_TILE, DSA_CHUNK), F32),
                        pltpu.VMEM((B_HEADS * Q_TILE, 128), F32), pltpu.VMEM((B_HEADS * Q_TILE, 128), F32),
                        pltpu.VMEM((B_HEADS * Q_TILE, 128), F32)],
        compiler_params=_cparams(("arbitrary", "arbitrary")),
        name="dsa_prompt",
    )(p16, p16, p32, p16, p16, p16)


def _merge_kernel(sc_ref, x_ref, ya_ref, yb_ref, az_ref, bz_ref, w_ref, sg_ref, g_ref, b_ref, o_ref, *, alpha):
    one_minus_lam_init = sc_ref[5:6, 0:1]
    ya = ya_ref[...]
    parts = []
    for h in range(A_HEADS):
        o = ya[:, 128 * h:128 * h + 128]
        y = o * lax.rsqrt(jnp.mean(o * o, axis=-1, keepdims=True) + LN_EPS) * sg_ref[...]
        parts.append(y * one_minus_lam_init)
    az = az_ref[...]
    bz = bz_ref[...]
    ya_g = jnp.concatenate(parts, axis=1) * (az * (1.0 / (1.0 + jnp.exp(-az))))
    yb_g = yb_ref[...] * (bz * (1.0 / (1.0 + jnp.exp(-bz))))
    mix = jnp.concatenate([ya_g, yb_g], axis=1).astype(BF16)
    y = alpha * x_ref[...] + jnp.dot(mix, w_ref[...], preferred_element_type=F32)
    mu = jnp.mean(y, axis=-1, keepdims=True)
    yc = y - mu
    var = jnp.mean(yc * yc, axis=-1, keepdims=True)
    o_ref[...] = yc * lax.rsqrt(var + LN_EPS) * g_ref[...] + b_ref[...]


def _merge(x, ya, yb, p32, w_out, layer, sc, sub_g, ln_g, ln_b, alpha, tm, name):
    nb, t, d = x.shape
    row = lambda wd, c: pl.BlockSpec((None, tm, wd), lambda b, i: (b, i, c))
    vec = lambda wd: pl.BlockSpec((1, wd), lambda b, i: (0, 0))
    return pl.pallas_call(
        functools.partial(_merge_kernel, alpha=alpha),
        grid=(nb, t // tm),
        in_specs=[pl.BlockSpec((8, 128), lambda b, i: (0, 0)),
                  row(d, 0), row(512, 0), row(512, 0), row(512, C_AZ // 512), row(512, C_BZ // 512),
                  pl.BlockSpec((None, d, d), lambda b, i: (layer, 0, 0)),
                  vec(128), vec(d), vec(d)],
        out_specs=row(d, 0),
        out_shape=jax.ShapeDtypeStruct((nb, t, d), F32),
        compiler_params=_cparams(("arbitrary", "arbitrary")),
        name=name,
    )(sc, x, ya, yb, p32, p32, w_out, sub_g.reshape(1, 128), ln_g.reshape(1, d), ln_b.reshape(1, d))


def _pad_rows(x, rows):
    return jnp.concatenate([x, jnp.zeros((rows - x.shape[0], x.shape[1]), x.dtype)], axis=0)


def _diff_sample_kernel(pt_ref, lam_ref, q_ref, kn_ref, vn_ref, *rest, n_pg, past):
    k_refs, v_refs = rest[:n_pg], rest[n_pg:2 * n_pg]
    o_ref, m_ref, l_ref, acc_ref = rest[2 * n_pg:]
    pg = pl.program_id(1)
    ds = q_ref.shape[0]
    rows = 2 * A_KV_HEADS * 2 * ds
    span = n_pg * PAGE_SIZE

    @pl.when(pg == 0)
    def _():
        m_ref[...] = jnp.full_like(m_ref, NEG_INF)
        l_ref[...] = jnp.zeros_like(l_ref)
        acc_ref[...] = jnp.zeros_like(acc_ref)

    q = q_ref[...]
    qst = jnp.concatenate([_diff_query_rows(q[:, 256 * n:256 * n + 256], ds) for n in range(A_KV_HEADS)], axis=0)
    slope = _const_rows([2.0 ** (-2 * (2 * n + g + 1)) for n in range(A_KV_HEADS) for _c in range(2)
                         for g in range(2)], ds)
    half = rows // A_KV_HEADS

    def attend(score, weigh, width, k0, mask):
        col = lax.broadcasted_iota(I32, (1, width), 1)
        s = jnp.concatenate([score(n, qst[half * n:half * (n + 1)]) for n in range(A_KV_HEADS)],
                            axis=0) + slope * col.astype(F32)
        if mask is not None:
            s = jnp.where(mask(col), s, NEG_INF)
        cj = slope * k0
        m, l, acc = m_ref[...][:, 0:1], l_ref[...][:, 0:1], acc_ref[...]
        mt = jnp.max(s, axis=1, keepdims=True)
        m_new = jnp.maximum(m, mt + cj)
        p = jnp.exp(s - (m_new - cj))
        alpha = jnp.exp(m - m_new)
        l = alpha * l + jnp.sum(p, axis=1, keepdims=True)
        pv = jnp.concatenate([weigh(n, p[half * n:half * (n + 1)].astype(BF16)) for n in range(A_KV_HEADS)],
                             axis=0)
        m_ref[...] = jnp.broadcast_to(m_new, m_ref.shape)
        l_ref[...] = jnp.broadcast_to(l, l_ref.shape)
        acc_ref[...] = alpha * acc + pv

    def past_score(n, q):
        kt = jnp.concatenate([r[128 * n:128 * n + 128, :] for r in k_refs], axis=1).astype(BF16)
        return jnp.dot(q, kt, preferred_element_type=F32)

    def past_weigh(n, p):
        v = jnp.concatenate([r[pl.ds(n, PAGE_SIZE, stride=A_KV_HEADS), :] for r in v_refs], axis=0).astype(BF16)
        return jnp.dot(p, v, preferred_element_type=F32)

    attend(past_score, past_weigh, span, (pg * span).astype(F32), None)

    @pl.when(pg == past // span - 1)
    def _():
        r = lax.broadcasted_iota(I32, (rows, 1), 0) & (ds - 1)
        kn = _pad_rows(kn_ref[...], 128).astype(BF16)
        vn = _pad_rows(vn_ref[...], 128).astype(BF16)
        attend(lambda n, q: _dot_nt(q, kn[:, 128 * n:128 * n + 128]),
               lambda n, p: jnp.dot(p, vn[:, 128 * n:128 * n + 128], preferred_element_type=F32),
               128, jnp.float32(past), lambda col: col <= r)
        lam = _diff_lambda(lam_ref)
        o = acc_ref[...] / l_ref[...][:, 0:1]
        outs = []
        for n in range(A_KV_HEADS):
            base = half * n
            for g in range(2):
                outs.append(o[base + g * ds:base + (g + 1) * ds]
                            - lam * o[base + (2 + g) * ds:base + (3 + g) * ds])
        o_ref[...] = jnp.concatenate(outs, axis=1)


def _diff_sample(p16s, lam, cache_k, cache_v, pt_flat, layer, n_pages):
    db, ds, _ = p16s.shape
    n_pg = PAGES_PER_STEP
    rows = 2 * A_KV_HEADS * 2 * ds
    past = n_pages * PAGE_SIZE
    kern = functools.partial(_diff_sample_kernel, n_pg=n_pg, past=past)

    def page(r):
        return pl.BlockSpec((None, None, 256, PAGE_SIZE),
                            lambda sb, pg, pt: (layer, pt[sb * n_pages + pg * n_pg + r], 0, 0))

    new = lambda c: pl.BlockSpec((None, ds, 256), lambda sb, pg, pt: (sb, 0, c // 256))
    grid_spec = pltpu.PrefetchScalarGridSpec(
        num_scalar_prefetch=1,
        grid=(db, n_pages // n_pg),
        in_specs=[pl.BlockSpec((8, 128), lambda sb, pg, pt: (0, 0)),
                  pl.BlockSpec((None, ds, 512), lambda sb, pg, pt: (sb, 0, C_AQ // 512)),
                  new(C_AK), new(C_AV)] + [page(r) for r in range(n_pg)] * 2,
        out_specs=pl.BlockSpec((None, ds, 512), lambda sb, pg, pt: (sb, 0, 0)),
        scratch_shapes=[pltpu.VMEM((rows, 128), F32), pltpu.VMEM((rows, 128), F32),
                        pltpu.VMEM((rows, 128), F32)])
    return pl.pallas_call(
        kern, grid_spec=grid_spec,
        out_shape=jax.ShapeDtypeStruct((db, ds, 512), F32),
        compiler_params=_cparams(("arbitrary", "arbitrary")),
        name="diff_sample",
    )(pt_flat, lam, p16s, p16s, p16s, *([cache_k] * n_pg), *([cache_v] * n_pg))


def _dsa_sample_kernel(pt_ref, bq_ref, iq_ref, iw_ref, kn_ref, vn_ref, in_ref, *rest, n_pg, past, topk, idx_bits):
    i_refs, k_refs, v_refs = rest[:n_pg], rest[n_pg:2 * n_pg], rest[2 * n_pg:3 * n_pg]
    o_ref, keys_ref, keys_new_ref, thr_ref, bound_ref, m_ref, l_ref, acc_ref = rest[3 * n_pg:]
    phase, sb, g = pl.program_id(0), pl.program_id(1), pl.program_id(2)
    ds = bq_ref.shape[0]
    rows = B_HEADS * ds
    span = n_pg * PAGE_SIZE
    n_groups = past // span
    cw = keys_ref.shape[2]
    n_sub = span // cw
    all_rows = keys_ref.shape[1]
    row0 = pl.multiple_of(sb * ds, ds)
    r_id = lax.broadcasted_iota(I32, (ds, 1), 0)

    def index_keys(ik, transposed):
        iq = iq_ref[...]
        heads = []
        for h in range(IDX_HEADS):
            p = iq[:, 128 * (h // 2):128 * (h // 2) + 128]
            if h % 2:
                p = pltpu.roll(p, 64, 1)
            heads.append(p[:, 0:64])
        iqst = jnp.concatenate(heads, axis=0).astype(BF16)
        if transposed:
            logits = jnp.dot(iqst, ik.astype(BF16), preferred_element_type=F32)
        else:
            logits = _dot_nt(iqst, ik.astype(BF16))
        w = iw_ref[...]
        sc = jnp.maximum(logits[0:ds], 0.0) * w[:, 0:1]
        for h in range(1, IDX_HEADS):
            sc = sc + jnp.maximum(logits[h * ds:(h + 1) * ds], 0.0) * w[:, h:h + 1]
        return _sort_key(sc * IDX_SCALE)

    @pl.when(phase == 0)
    def _():
        ik = jnp.concatenate([r[...] for r in i_refs], axis=1)
        keys = index_keys(ik, True)
        for c in range(n_sub):
            keys_ref[g * n_sub + c, pl.ds(row0, ds), :] = keys[:, c * cw:(c + 1) * cw]

    @pl.when((phase == 0) & (g == n_groups - 1))
    def _():
        col = lax.broadcasted_iota(I32, (1, 128), 1)
        key_new = index_keys(_pad_rows(in_ref[...][:, 0:64], 128), False)
        keys_new_ref[pl.ds(row0, ds), :] = jnp.where(col <= r_id, key_new, INT_MIN)

    @pl.when((phase == 1) & (sb == 0) & (g == 0))
    def _():
        col = lax.broadcasted_iota(I32, (1, cw), 1)
        col_new = lax.broadcasted_iota(I32, (1, 128), 1) + past

        def count(pred):
            def body(j, acc):
                return acc + _lane_fold(jnp.where(pred(keys_ref[j], col + j * cw), 1.0, 0.0), jnp.add)
            acc = lax.fori_loop(0, keys_ref.shape[0], body, jnp.zeros((all_rows, 128), F32))
            acc = acc + jnp.where(pred(keys_new_ref[...], col_new), 1.0, 0.0)
            return jnp.sum(acc, axis=1, keepdims=True)

        def bit_step(b, t):
            cand = t + lax.shift_left(jnp.int32(1), 31 - b)
            return jnp.where(count(lambda kc, _: kc >= cand) >= topk, cand, t)

        t = lax.fori_loop(0, 32, bit_step, jnp.full((all_rows, 1), INT_MIN, I32))
        thr = jnp.maximum(t, INT_MIN + 1)
        need = topk - count(lambda kc, _: kc > thr)
        excess = count(lambda kc, _: kc == thr) > need

        def tie_bound():
            def tie_step(b, jb):
                cand = jb + lax.shift_left(jnp.int32(1), idx_bits - 1 - b)
                return jnp.where(count(lambda kc, ix: (kc == thr) & (ix < cand)) < need, cand, jb)
            return lax.fori_loop(0, idx_bits, tie_step, jnp.zeros((all_rows, 1), I32))

        big = jnp.full((all_rows, 1), 2 ** 30, I32)
        any_excess = jnp.max(jnp.where(excess, 1.0, 0.0)) > 0.0
        bound = lax.cond(any_excess, lambda: jnp.where(excess, tie_bound(), big), lambda: big)
        thr_ref[...] = jnp.broadcast_to(thr, thr_ref.shape)
        bound_ref[...] = jnp.broadcast_to(bound, bound_ref.shape)

    qst = _dsa_query_rows(bq_ref[...])
    slope = _const_rows([2.0 ** -(h + 1) for h in range(B_HEADS)], ds)

    def attend(qk, weigh, kc, k0):
        width = qk.shape[1]
        col = lax.broadcasted_iota(I32, (1, width), 1)
        thr, bound = thr_ref[pl.ds(row0, ds), :][:, 0:1], bound_ref[pl.ds(row0, ds), :][:, 0:1]
        sel = (kc > thr) | ((kc == thr) & (col + k0 <= bound))
        s = (qk + slope * col.astype(F32)).reshape(B_HEADS, ds, width)
        s = jnp.where(sel[None], s, NEG_INF).reshape(rows, width)
        cj = slope * k0.astype(F32)
        m, l, acc = m_ref[...][:, 0:1], l_ref[...][:, 0:1], acc_ref[...]
        mt = jnp.max(s, axis=1, keepdims=True)
        m_new = jnp.maximum(m, mt + cj)
        m_safe = jnp.where(m_new == NEG_INF, 0.0, m_new)
        p = jnp.exp(s - (m_safe - cj))
        alpha = jnp.exp(m - m_safe)
        m_ref[...] = jnp.broadcast_to(m_new, m_ref.shape)
        l_ref[...] = jnp.broadcast_to(alpha * l + jnp.sum(p, axis=1, keepdims=True), l_ref.shape)
        acc_ref[...] = alpha * acc + weigh(p.astype(BF16))

    @pl.when((phase == 1) & (g == 0))
    def _():
        m_ref[...] = jnp.full_like(m_ref, NEG_INF)
        l_ref[...] = jnp.zeros_like(l_ref)
        acc_ref[...] = jnp.zeros_like(acc_ref)

    @pl.when(phase == 1)
    def _():
        kt = jnp.concatenate([r[...] for r in k_refs], axis=1).astype(BF16)
        vt = jnp.concatenate([r[...] for r in v_refs], axis=1).astype(BF16)
        kc = jnp.concatenate([keys_ref[g * n_sub + c, pl.ds(row0, ds), :] for c in range(n_sub)], axis=1)
        attend(jnp.dot(qst, kt, preferred_element_type=F32), lambda p: _dot_nt(p, vt), kc, g * span)

    @pl.when((phase == 1) & (g == n_groups - 1))
    def _():
        kn = _pad_rows(kn_ref[...], 128).astype(BF16)
        vn = _pad_rows(vn_ref[...], 128).astype(BF16)
        attend(_dot_nt(qst, kn), lambda p: jnp.dot(p, vn, preferred_element_type=F32),
               keys_new_ref[pl.ds(row0, ds), :], jnp.int32(past))
        o_ref[...] = _dsa_merge_heads(acc_ref[...] / l_ref[...][:, 0:1], ds)


def _dsa_sample(p16s, p32s, cache_i, cache_k, cache_v, pt_flat, layer, n_pages, topk):
    db, ds, _ = p16s.shape
    n_pg = PAGES_PER_STEP
    n_groups = n_pages // n_pg
    past = n_pages * PAGE_SIZE
    rows = B_HEADS * ds
    kern = functools.partial(_dsa_sample_kernel, n_pg=n_pg, past=past, topk=float(topk),
                             idx_bits=int(past + 128).bit_length())

    def page(r, feat, first_phase):
        def index(ph, sb, g, pt):
            own = pt[sb * n_pages + g * n_pg + r]
            parked = pt[(db - 1) * n_pages + (n_groups - 1) * n_pg + r] if first_phase else pt[r]
            return (layer, jnp.where(ph == (0 if first_phase else 1), own, parked), 0, 0)
        return pl.BlockSpec((None, None, feat, PAGE_SIZE), index)

    new = lambda wd, c: pl.BlockSpec((None, ds, wd), lambda ph, sb, g, pt: (sb, 0, c // wd))
    cw = min(DSA_CHUNK, n_pg * PAGE_SIZE)
    grid_spec = pltpu.PrefetchScalarGridSpec(
        num_scalar_prefetch=1,
        grid=(2, db, n_groups),
        in_specs=[new(512, C_BQ), new(512, C_IQ), new(128, C_IW), new(128, C_BK), new(128, C_BV), new(128, C_IK)]
                 + [page(r, 64, True) for r in range(n_pg)]
                 + [page(r, 128, False) for r in range(n_pg)] * 2,
        out_specs=pl.BlockSpec((None, ds, 512), lambda ph, sb, g, pt: (jnp.where(ph == 0, 0, sb), 0, 0)),
        scratch_shapes=[pltpu.VMEM((past // cw, db * ds, cw), I32), pltpu.VMEM((db * ds, 128), I32),
                        pltpu.VMEM((db * ds, 128), I32), pltpu.VMEM((db * ds, 128), I32),
                        pltpu.VMEM((rows, 128), F32), pltpu.VMEM((rows, 128), F32), pltpu.VMEM((rows, 128), F32)])
    return pl.pallas_call(
        kern, grid_spec=grid_spec,
        out_shape=jax.ShapeDtypeStruct((db, ds, 512), F32),
        compiler_params=_cparams(("arbitrary", "arbitrary", "arbitrary")),
        name="dsa_sample",
    )(pt_flat, p16s, p16s, p32s, p16s, p16s, p16s,
      *([cache_i] * n_pg), *([cache_k] * n_pg), *([cache_v] * n_pg))


def _regroup_weights(w_in):
    sizes = (512, 256, 256, 512, 512, 128, 128, 512, 64, 8, 512)
    offs = [0]
    for s in sizes:
        offs.append(offs[-1] + s)
    aq, ak, av, az, bq, bk, bv, iq, ik, iw, bz = [w_in[..., offs[j]:offs[j + 1]] for j in range(len(sizes))]
    scale = HEAD_DIM ** -0.5
    pad = jnp.zeros(iw.shape[:-1] + (128 - iw.shape[-1],), w_in.dtype)
    return jnp.concatenate([aq * scale, bq * scale, iq, ak, av, bk, bv, ik, ik, az, bz, iw, pad],
                           axis=-1).astype(BF16)


def _pick_tile(n, candidates):
    for c in candidates:
        if n % c == 0:
            return c
    return n


def kernel(x_prompt, x_sample, cache_diff_k, cache_diff_v, cache_dsa_k, cache_dsa_v, cache_idx_k, page_table,
           meta_tokens, ln_in_g, ln_in_b, w_in, w_out, lambda_q1, lambda_k1, lambda_q2, lambda_k2,
           subln_g, ln_g, ln_b):
    nb, seq, d = x_prompt.shape
    db, ds, _ = x_sample.shape
    depth = w_in.shape[0]
    n_pool = cache_diff_k.shape[1]
    n_pages = page_table.shape[1]
    t_real = seq + N_META
    tpad = -(-t_real // DSA_CHUNK) * DSA_CHUNK
    assert ds == 8 and n_pages % PAGES_PER_STEP == 0 and d == 1024
    alpha = (2 * depth) ** 0.25
    topk_p = min(TOPK_MAX, seq // 4)
    topk_s = min(TOPK_MAX, (n_pages * PAGE_SIZE + ds) // 4)

    meta = jnp.broadcast_to(meta_tokens[None].astype(x_prompt.dtype), (nb, N_META, d))
    xp = jnp.concatenate([meta, x_prompt, jnp.zeros((nb, tpad - t_real, d), x_prompt.dtype)], axis=1)
    xp = _layernorm_rows(xp.reshape(nb * tpad, d), ln_in_g, ln_in_b, KV_CHUNK).reshape(nb, tpad, d)
    xs = _layernorm_rows(x_sample.reshape(db * ds, d), ln_in_g, ln_in_b,
                         _pick_tile(db * ds, (256, 128, 8))).reshape(1, db * ds, d)

    w = _regroup_weights(w_in)
    w_o = w_out.astype(BF16)
    pad64 = lambda v: jnp.pad(v, ((0, 0), (0, 128 - v.shape[-1])))
    ck = cache_diff_k.transpose(0, 1, 3, 4, 5, 2).reshape(depth, n_pool, 256, PAGE_SIZE)
    cv = cache_diff_v.reshape(depth, n_pool, PAGE_SIZE * A_KV_HEADS, 128)
    cbk = cache_dsa_k.transpose(0, 1, 3, 4, 2).reshape(depth, n_pool, 128, PAGE_SIZE)
    cbv = cache_dsa_v.transpose(0, 1, 3, 4, 2).reshape(depth, n_pool, 128, PAGE_SIZE)
    ci = cache_idx_k.transpose(0, 1, 3, 2)
    pt_flat = page_table.reshape(-1).astype(I32)
    tm_p = _pick_tile(tpad, (256, 128))
    tm_s = _pick_tile(db * ds, (256, 128, 8))

    p_rows = [[] for _ in range(5)]
    s_rows = [[] for _ in range(5)]
    for layer in range(depth):
        lam_init = 0.8 - 0.6 * math.exp(-0.3 * layer)
        consts = jnp.stack([jnp.full((128,), lam_init, F32), jnp.full((128,), 1.0 - lam_init, F32),
                            jnp.zeros((128,), F32), jnp.zeros((128,), F32)])
        lam = jnp.concatenate([pad64(jnp.stack([lambda_q1[layer], lambda_k1[layer],
                                                 lambda_q2[layer], lambda_k2[layer]])), consts], axis=0)

        p16, p32, *rows_p = _project(xp, w, layer, t_real, tm_p, BF16, "proj_prompt")
        ya = _diff_prompt(p16, lam, t_real)
        yb = _dsa_prompt(p16, p32, t_real, topk_p)
        xp = _merge(xp, ya, yb, p32, w_o, layer, lam, subln_g[layer], ln_g[layer], ln_b[layer],
                    alpha, tm_p, "merge_prompt")

        p16s, p32s, *rows_s = _project(xs, w, layer, db * ds, tm_s, F32, "proj_sample")
        p16s3 = p16s.reshape(db, ds, W16)
        ya_s = _diff_sample(p16s3, lam, ck, cv, pt_flat, layer, n_pages)
        yb_s = _dsa_sample(p16s3, p32s.reshape(db, ds, W32), ci, cbk, cbv, pt_flat, layer,
                           n_pages, topk_s)
        xs = _merge(xs, ya_s.reshape(1, db * ds, 512), yb_s.reshape(1, db * ds, 512), p32s, w_o, layer, lam,
                    subln_g[layer], ln_g[layer], ln_b[layer], alpha, tm_s, "merge_sample")

        for lst, a in zip(p_rows, rows_p):
            lst.append(a)
        for lst, a in zip(s_rows, rows_s):
            lst.append(a)

    pk, pv, pbk, pbv, pik = [jnp.stack(r) for r in p_rows]
    sk, sv, sbk, sbv, sik = [jnp.stack(r).reshape(depth, db, ds, -1) for r in s_rows]
    return (xp[:, N_META:t_real], xs.reshape(db, ds, d),
            pk.reshape(depth, nb, t_real, A_KV_HEADS, 2, HEAD_DIM),
            pv.reshape(depth, nb, t_real, A_KV_HEADS, 2 * HEAD_DIM),
            pbk.reshape(depth, nb, t_real, B_KV_HEADS, HEAD_DIM),
            pbv.reshape(depth, nb, t_real, B_KV_HEADS, HEAD_DIM),
            pik,
            sk.reshape(depth, db, ds, A_KV_HEADS, 2, HEAD_DIM),
            sv.reshape(depth, db, ds, A_KV_HEADS, 2 * HEAD_DIM),
            sbk.reshape(depth, db, ds, B_KV_HEADS, HEAD_DIM),
            sbv.reshape(depth, db, ds, B_KV_HEADS, HEAD_DIM),
            sik)
```

```python
import functools
import math

import jax
import jax.numpy as jnp
from jax import lax
from jax.experimental import pallas as pl
from jax.experimental.pallas import tpu as pltpu

N_META = 16
HEAD_DIM = 64
A_HEADS = 4
A_KV_HEADS = 2
B_HEADS = 8
B_KV_HEADS = 2
IDX_HEADS = 8
IDX_DIM = 64
IDX_SCALE = (IDX_HEADS * IDX_DIM) ** -0.5
TOPK_MAX = 256
PAGE_SIZE = 128
LN_EPS = 1e-5

Q_TILE = 128
DIFF_Q_TILE = 256
KV_CHUNK = 512
DSA_CHUNK = 512
PAGES_PER_STEP = 32
VMEM_LIMIT_BYTES = 56 * 1024 * 1024

LOG2E = math.log2(math.e)
INT_MIN = -(2 ** 31)
NEG_INF = float("-inf")

C_AQ, C_BQ, C_IQ, C_AK, C_AV, C_BK, C_BV, C_IK = 0, 512, 1024, 1536, 1792, 2048, 2176, 2304
W16 = 2432
C_AZ, C_BZ, C_IW = 0, 512, 1024
W32 = 1152
NCOL = W16 + W32

F32 = jnp.float32
BF16 = jnp.bfloat16
I32 = jnp.int32


def _dot_nt(a, b):
    return lax.dot_general(a, b, (((1,), (1,)), ((), ())), preferred_element_type=F32)


def _cparams(sem):
    return pltpu.CompilerParams(dimension_semantics=sem, vmem_limit_bytes=VMEM_LIMIT_BYTES)


def _sort_key(score):
    bits = lax.bitcast_convert_type(score, I32)
    return bits ^ ((bits >> 31) & 0x7FFFFFFF)


def _lane_lo(width=128):
    return lax.broadcasted_iota(I32, (1, width), 1) < 64


def _ln_kernel(x_ref, g_ref, b_ref, o_ref):
    x = x_ref[...]
    mu = jnp.mean(x, axis=-1, keepdims=True)
    xc = x - mu
    var = jnp.mean(xc * xc, axis=-1, keepdims=True)
    o_ref[...] = xc * lax.rsqrt(var + LN_EPS) * g_ref[...] + b_ref[...]


def _layernorm_rows(x, g, b, tm):
    rows, d = x.shape
    return pl.pallas_call(
        _ln_kernel,
        grid=(rows // tm,),
        in_specs=[pl.BlockSpec((tm, d), lambda i: (i, 0)),
                  pl.BlockSpec((1, d), lambda i: (0, 0)),
                  pl.BlockSpec((1, d), lambda i: (0, 0))],
        out_specs=pl.BlockSpec((tm, d), lambda i: (i, 0)),
        out_shape=jax.ShapeDtypeStruct((rows, d), F32),
        compiler_params=_cparams(("arbitrary",)),
        name="ln_in",
    )(x, g.reshape(1, d), b.reshape(1, d))


def _proj_kernel(x_ref, w_ref, *rest, last_out_tile):
    p16_ref, p32_ref, ak_ref, av_ref, bk_ref, bv_ref, ik_ref = rest[-7:]
    xb = x_ref[...].astype(BF16)
    res = jnp.dot(xb, w_ref[...], preferred_element_type=F32)
    p16_ref[...] = res[:, :W16].astype(p16_ref.dtype)
    p32_ref[...] = res[:, W16:]

    @pl.when(pl.program_id(1) <= last_out_tile)
    def _():
        tm = res.shape[0]
        ak_ref[...] = res[:, C_AK:C_AK + 256].T
        for n in range(A_KV_HEADS):
            av_ref[pl.ds(n, tm, stride=A_KV_HEADS), :] = res[:, C_AV + 128 * n:C_AV + 128 * (n + 1)]
        bk_ref[...] = res[:, C_BK:C_BK + 128].T
        bv_ref[...] = res[:, C_BV:C_BV + 128].T
        ik_ref[...] = res[:, C_IK:C_IK + 128].T[0:IDX_DIM]


def _project(x, w, layer, depth, stacks, t_out, tm, p16_dtype, name):
    nb, t, d = x.shape
    last = pl.cdiv(t_out, tm) - 1
    row = lambda wd: pl.BlockSpec((None, tm, wd), lambda b, i: (b, i, 0))
    out = lambda ft: pl.BlockSpec((None, None, ft, tm), lambda b, i: (layer, b, 0, jnp.minimum(i, last)))
    shp = lambda ft: jax.ShapeDtypeStruct((depth, nb, ft, t_out), F32)
    out_v = pl.BlockSpec((None, None, A_KV_HEADS * tm, 128), lambda b, i: (layer, b, jnp.minimum(i, last), 0))
    shp_v = jax.ShapeDtypeStruct((depth, nb, A_KV_HEADS * t_out, 128), F32)
    stacks = list(stacks or ())
    return pl.pallas_call(
        functools.partial(_proj_kernel, last_out_tile=last),
        grid=(nb, t // tm),
        in_specs=[row(d), pl.BlockSpec((None, d, NCOL), lambda b, i: (layer, 0, 0))]
                 + [pl.BlockSpec(memory_space=pl.ANY)] * len(stacks),
        out_specs=[row(W16), row(W32), out(256), out_v, out(128), out(128), out(IDX_DIM)],
        out_shape=[jax.ShapeDtypeStruct((nb, t, W16), p16_dtype),
                   jax.ShapeDtypeStruct((nb, t, W32), F32),
                   shp(256), shp_v, shp(128), shp(128), shp(IDX_DIM)],
        input_output_aliases={2 + j: 2 + j for j in range(len(stacks))},
        compiler_params=_cparams(("arbitrary", "arbitrary")),
        name=name,
    )(x, w, *stacks)


def _diff_lambda(lam_ref):
    r = lam_ref[...]
    e1 = jnp.exp(jnp.sum(r[0:1] * r[1:2], axis=1, keepdims=True))
    e2 = jnp.exp(jnp.sum(r[2:3] * r[3:4], axis=1, keepdims=True))
    return e1 - e2 + r[4:5, 0:1]


def _half_select(x, keep_low):
    lo = _lane_lo()
    return jnp.where(lo, x, 0.0) if keep_low else jnp.where(lo, 0.0, x)


def _diff_query_rows(q, n_rows):
    pieces = [q[:, 0:128], q[:, 128:256]]
    return jnp.concatenate([_half_select(p, True) for p in pieces] +
                           [_half_select(p, False) for p in pieces], axis=0).astype(BF16)


def _dsa_query_rows(q):
    out = []
    for h in range(B_HEADS):
        n = h // (B_HEADS // B_KV_HEADS)
        p = q[:, 128 * (h // 2):128 * (h // 2) + 128]
        if h % 2 != n:
            p = pltpu.roll(p, 64, 1)
        out.append(_half_select(p, n == 0))
    return jnp.concatenate(out, axis=0).astype(BF16)


def _dsa_merge_heads(o, r):
    outs = []
    lo = _lane_lo()
    for j in range(B_HEADS // 2):
        n = j // 2
        oe, oo = o[2 * j * r:(2 * j + 1) * r], o[(2 * j + 1) * r:(2 * j + 2) * r]
        if n == 0:
            outs.append(jnp.where(lo, oe, pltpu.roll(oo, 64, 1)))
        else:
            outs.append(jnp.where(lo, pltpu.roll(oe, 64, 1), oo))
    return jnp.concatenate(outs, axis=1)


def _const_rows(values, r):
    return jnp.concatenate([jnp.full((r, 1), v, F32) for v in values], axis=0)


def _count(mask):
    return jnp.sum(jnp.where(mask, 1.0, 0.0), axis=1, keepdims=True)


def _lane_fold(x, op):
    acc = x[:, 0:128]
    for c in range(1, x.shape[1] // 128):
        acc = op(acc, x[:, 128 * c:128 * (c + 1)])
    return acc


def _for_chunks_paired(n, body):
    def pair(jj, _):
        body(2 * jj, 0)
        body(2 * jj + 1, 1)
        return 0

    lax.fori_loop(0, n // 2, pair, 0)

    @pl.when(n % 2 == 1)
    def _():
        body(n - 1, 0)


def _store_scores(j, qk, shifts, tq, s_ref, m_ref):
    for r, shift in enumerate(shifts):
        rows = slice(r * tq, (r + 1) * tq)
        s = qk[rows] * LOG2E + shift
        s_ref[j, rows, :] = s
        m_ref[rows, :] = jnp.maximum(m_ref[rows, :], _lane_fold(s, jnp.maximum))


def _softmax_weigh(n_chunks, tc, rows, s_ref, v_ref, m_ref, acc_ref, lacc_ref):
    m = jnp.broadcast_to(jnp.max(m_ref[...], axis=1, keepdims=True), (rows, 128))
    acc_ref[...] = jnp.zeros_like(acc_ref)
    lacc_ref[...] = jnp.zeros_like(lacc_ref)

    def weigh(j):
        s = s_ref[j]
        p = jnp.concatenate([jnp.exp2(s[:, 128 * c:128 * (c + 1)] - m) for c in range(tc // 128)], axis=1)
        pv = jnp.dot(p.astype(BF16), v_ref[pl.ds(pl.multiple_of(j * tc, tc), tc), :], preferred_element_type=F32)
        return _lane_fold(p, jnp.add), pv

    def step(j, _slot):
        l, pv = weigh(j)
        lacc_ref[...] += l
        acc_ref[...] += pv

    _for_chunks_paired(n_chunks, step)
    return acc_ref[...] / jnp.sum(lacc_ref[...], axis=1, keepdims=True)


def _diff_prompt_kernel(lam_ref, q_ref, k_ref, v_ref, o_ref, s_ref, m_ref, acc_ref, lacc_ref, *, n_real_tiles):
    n = pl.program_id(1)
    i = pl.program_id(2)
    tq, tc = DIFF_Q_TILE, KV_CHUNK

    @pl.when(i >= n_real_tiles)
    def _():
        o_ref[...] = jnp.zeros_like(o_ref)

    @pl.when(i < n_real_tiles)
    def _():
        lam = _diff_lambda(lam_ref)
        qst = _diff_query_rows(q_ref[...].astype(F32), tq)
        slope_g = (jnp.where(n == 0, 2.0 ** -2, 2.0 ** -6), jnp.where(n == 0, 2.0 ** -4, 2.0 ** -8))
        qpos = i * tq + lax.broadcasted_iota(I32, (tq, 1), 0)
        col = lax.broadcasted_iota(I32, (1, tc), 1)
        m_ref[...] = jnp.full_like(m_ref, NEG_INF)

        def score_step(j, masked):
            k0 = pl.multiple_of(j * tc, tc)
            qk = _dot_nt(qst, k_ref[pl.ds(k0, tc), :])
            kpos = (col + k0).astype(F32)
            shifts = [(slope_g[r % 2] * LOG2E) * kpos for r in range(4)]
            if masked:
                causal = jnp.where(col + k0 <= qpos, 0.0, NEG_INF)
                shifts = [sh + causal for sh in shifts[:2]] * 2
            _store_scores(j, qk, shifts, tq, s_ref, m_ref)

        n_full = (i * tq) // tc
        _for_chunks_paired(n_full, lambda j, _slot: score_step(j, False))
        score_step(n_full, True)
        o = _softmax_weigh(n_full + 1, tc, 4 * tq, s_ref, v_ref, m_ref, acc_ref, lacc_ref)
        out_g0 = o[0:tq] - lam * o[2 * tq:3 * tq]
        out_g1 = o[tq:2 * tq] - lam * o[3 * tq:4 * tq]
        o_ref[...] = jnp.concatenate([out_g0, out_g1], axis=1)


def _diff_prompt(p16, lam, t_real):
    nb, tpad, _ = p16.shape
    tq = DIFF_Q_TILE
    kern = functools.partial(_diff_prompt_kernel, n_real_tiles=pl.cdiv(t_real, tq))
    return pl.pallas_call(
        kern,
        grid=(nb, A_KV_HEADS, tpad // tq),
        in_specs=[pl.BlockSpec((8, 128), lambda b, n, i: (0, 0)),
                  pl.BlockSpec((None, tq, 256), lambda b, n, i: (b, i, C_AQ // 256 + n)),
                  pl.BlockSpec((None, tpad, 128), lambda b, n, i: (b, 0, C_AK // 128 + n)),
                  pl.BlockSpec((None, tpad, 128), lambda b, n, i: (b, 0, C_AV // 128 + n))],
        out_specs=pl.BlockSpec((None, tq, 256), lambda b, n, i: (b, i, n)),
        out_shape=jax.ShapeDtypeStruct((nb, tpad, 512), F32),
        scratch_shapes=[pltpu.VMEM((tpad // KV_CHUNK, 4 * tq, KV_CHUNK), F32), pltpu.VMEM((4 * tq, 128), F32),
                        pltpu.VMEM((4 * tq, 128), F32), pltpu.VMEM((4 * tq, 128), F32)],
        compiler_params=_cparams(("arbitrary", "arbitrary", "arbitrary")),
        name="diff_prompt",
    )(lam, p16, p16, p16)


def _dsa_prompt_kernel(bq_ref, iq_ref, iw_ref, bk_ref, bv_ref, ik_ref, o_ref,
                       keys_ref, keys_t_ref, mask_ref, s_ref, m_ref, acc_ref, lacc_ref, *,
                       n_real_tiles, topk, idx_bits):
    i = pl.program_id(1)
    tq, tc = Q_TILE, DSA_CHUNK

    @pl.when(i >= n_real_tiles)
    def _():
        o_ref[...] = jnp.zeros_like(o_ref)

    @pl.when(i < n_real_tiles)
    def _():
        n_chunks = (i * tq) // tc + 1
        col = lax.broadcasted_iota(I32, (1, tc), 1)
        qpos = i * tq + lax.broadcasted_iota(I32, (tq, 1), 0)

        iq = iq_ref[...].astype(F32)
        iqst = jnp.concatenate(
            [_half_select(iq[:, 128 * (h // 2):128 * (h // 2) + 128], h % 2 == 0)
             for h in range(IDX_HEADS)], axis=0).astype(BF16)
        w = iw_ref[...]
        wcol = [jnp.broadcast_to(w[:, h:h + 1], (tq, tc)) for h in range(IDX_HEADS)]

        def index_step(j, _slot):
            k0 = pl.multiple_of(j * tc, tc)
            logits = _dot_nt(iqst, ik_ref[pl.ds(k0, tc), :])
            sc = jnp.maximum(logits[0:tq], 0.0) * wcol[0]
            for h in range(1, IDX_HEADS):
                sc = sc + jnp.maximum(logits[h * tq:(h + 1) * tq], 0.0) * wcol[h]
            key = jnp.where(col + k0 <= qpos, _sort_key(sc * IDX_SCALE), INT_MIN)
            keys_ref[j] = key
            keys_t_ref[j] = key.T

        _for_chunks_paired(n_chunks, index_step)

        kidx = lax.broadcasted_iota(I32, (tc, 1), 0)

        def count_over_chunks(pred):
            def body(j, acc):
                hit = jnp.where(pred(keys_t_ref[j], kidx + j * tc), 1.0, 0.0)
                parts = [hit[8 * r:8 * (r + 1)] for r in range(tc // 8)]
                while len(parts) > 1:
                    parts = [parts[r] + parts[r + 1] for r in range(0, len(parts), 2)]
                return acc + parts[0]
            acc = lax.fori_loop(0, n_chunks, body, jnp.zeros((8, tq), F32))
            return jnp.sum(acc, axis=0, keepdims=True)

        def bit_step(b, carry):
            t, n_ge = carry
            cand = t + lax.shift_left(jnp.int32(1), 31 - b)
            cnt = count_over_chunks(lambda kc, _: kc >= cand)
            return jnp.where(cnt >= topk, cand, t), jnp.where(cnt >= topk, cnt, n_ge)

        t, n_ge = lax.fori_loop(0, 32, bit_step, (jnp.full((1, tq), INT_MIN, I32), jnp.zeros((1, tq), F32)))
        thr_t = jnp.maximum(t, INT_MIN + 1)
        n_gt = count_over_chunks(lambda kc, _: kc > thr_t)
        need = topk - n_gt
        n_eq = jnp.where(t == INT_MIN, 0.0, n_ge - n_gt)
        excess = n_eq > need

        def tie_bound():
            def tie_step(b, jb):
                cand = jb + lax.shift_left(jnp.int32(1), idx_bits - 1 - b)
                cnt = count_over_chunks(lambda kc, idx: (kc == thr_t) & (idx < cand))
                return jnp.where(cnt < need, cand, jb)
            return lax.fori_loop(0, idx_bits, tie_step, jnp.zeros((1, tq), I32))

        any_excess = jnp.max(jnp.where(excess, 1.0, 0.0)) > 0.0
        big = jnp.full((1, tq), 2 ** 30, I32)
        bound_t = lax.cond(any_excess, lambda: jnp.where(excess, tie_bound(), big), lambda: big)
        to_col = lambda v: jnp.broadcast_to(v, (tq, tq)).T[:, 0:1]
        thr, bound = to_col(thr_t), to_col(bound_t)

        qst = _dsa_query_rows(bq_ref[...].astype(F32))

        m_ref[...] = jnp.full_like(m_ref, NEG_INF)

        def score_step(j, slot):
            k0 = pl.multiple_of(j * tc, tc)
            kc = keys_ref[j]
            sel = (kc > thr) | ((kc == thr) & (col + k0 <= bound))
            mask_ref[slot] = jnp.where(sel, 0.0, NEG_INF)
            qk = _dot_nt(qst, bk_ref[pl.ds(k0, tc), :])
            kpos = (col + k0).astype(F32)
            _store_scores(j, qk, [(2.0 ** -(h + 1) * LOG2E) * kpos + mask_ref[slot] for h in range(B_HEADS)],
                          tq, s_ref, m_ref)

        _for_chunks_paired(n_chunks, score_step)
        o = _softmax_weigh(n_chunks, tc, B_HEADS * tq, s_ref, bv_ref, m_ref, acc_ref, lacc_ref)
        o_ref[...] = _dsa_merge_heads(o, tq)


def _dsa_prompt(p16, p32, t_real, topk):
    nb, tpad, _ = p16.shape
    nq = tpad // Q_TILE
    kern = functools.partial(_dsa_prompt_kernel, n_real_tiles=pl.cdiv(t_real, Q_TILE), topk=float(topk),
                             idx_bits=max(1, int(tpad).bit_length()))
    whole = lambda c: pl.BlockSpec((None, tpad, 128), lambda b, i: (b, 0, c // 128))
    return pl.pallas_call(
        kern,
        grid=(nb, nq),
        in_specs=[pl.BlockSpec((None, Q_TILE, 512), lambda b, i: (b, i, C_BQ // 512)),
                  pl.BlockSpec((None, Q_TILE, 512), lambda b, i: (b, i, C_IQ // 512)),
                  pl.BlockSpec((None, Q_TILE, 128), lambda b, i: (b, i, C_IW // 128)),
                  whole(C_BK), whole(C_BV), whole(C_IK)],
        out_specs=pl.BlockSpec((None, Q_TILE, 512), lambda b, i: (b, i, 0)),
        out_shape=jax.ShapeDtypeStruct((nb, tpad, 512), F32),
        scratch_shapes=[pltpu.VMEM((tpad // DSA_CHUNK, Q_TILE, DSA_CHUNK), I32),
                        pltpu.VMEM((tpad // DSA_CHUNK, DSA_CHUNK, Q_TILE), I32),
                        pltpu.VMEM((2, Q_TILE, DSA_CHUNK), F32),
                        pltpu.VMEM((tpad // DSA_CHUNK, B_HEADS * Q_TILE, DSA_CHUNK), F32),
                        pltpu.VMEM((B_HEADS * Q_TILE, 128), F32), pltpu.VMEM((B_HEADS * Q_TILE, 128), F32),
                        pltpu.VMEM((B_HEADS * Q_TILE, 128), F32)],
        compiler_params=_cparams(("arbitrary", "arbitrary")),
        name="dsa_prompt",
    )(p16, p16, p32, p16, p16, p16)


def _merge_kernel(sc_ref, x_ref, ya_ref, yb_ref, az_ref, bz_ref, w_ref, sg_ref, g_ref, b_ref, o_ref, *, alpha):
    one_minus_lam_init = sc_ref[5:6, 0:1]
    ya = ya_ref[...]
    parts = []
    for h in range(A_HEADS):
        o = ya[:, 128 * h:128 * h + 128]
        y = o * lax.rsqrt(jnp.mean(o * o, axis=-1, keepdims=True) + LN_EPS) * sg_ref[...]
        parts.append(y * one_minus_lam_init)
    az = az_ref[...]
    bz = bz_ref[...]
    ya_g = jnp.concatenate(parts, axis=1) * (az * (1.0 / (1.0 + jnp.exp(-az))))
    yb_g = yb_ref[...] * (bz * (1.0 / (1.0 + jnp.exp(-bz))))
    mix = jnp.concatenate([ya_g, yb_g], axis=1).astype(BF16)
    y = alpha * x_ref[...] + jnp.dot(mix, w_ref[...], preferred_element_type=F32)
    mu = jnp.mean(y, axis=-1, keepdims=True)
    yc = y - mu
    var = jnp.mean(yc * yc, axis=-1, keepdims=True)
    o_ref[...] = yc * lax.rsqrt(var + LN_EPS) * g_ref[...] + b_ref[...]


def _merge(x, ya, yb, p32, w_out, layer, sc, sub_g, ln_g, ln_b, alpha, tm, name):
    nb, t, d = x.shape
    row = lambda wd, c: pl.BlockSpec((None, tm, wd), lambda b, i: (b, i, c))
    vec = lambda wd: pl.BlockSpec((1, wd), lambda b, i: (0, 0))
    return pl.pallas_call(
        functools.partial(_merge_kernel, alpha=alpha),
        grid=(nb, t // tm),
        in_specs=[pl.BlockSpec((8, 128), lambda b, i: (0, 0)),
                  row(d, 0), row(512, 0), row(512, 0), row(512, C_AZ // 512), row(512, C_BZ // 512),
                  pl.BlockSpec((None, d, d), lambda b, i: (layer, 0, 0)),
                  vec(128), vec(d), vec(d)],
        out_specs=row(d, 0),
        out_shape=jax.ShapeDtypeStruct((nb, t, d), F32),
        compiler_params=_cparams(("arbitrary", "arbitrary")),
        name=name,
    )(sc, x, ya, yb, p32, p32, w_out, sub_g.reshape(1, 128), ln_g.reshape(1, d), ln_b.reshape(1, d))


def _pad_rows(x, rows):
    return jnp.concatenate([x, jnp.zeros((rows - x.shape[0], x.shape[1]), x.dtype)], axis=0)


def _diff_sample_kernel(pt_ref, lam_ref, q_ref, kn_ref, vn_ref, *rest, n_pg, past):
    k_refs, v_refs = rest[:n_pg], rest[n_pg:2 * n_pg]
    o_ref, m_ref, l_ref, acc_ref = rest[2 * n_pg:]
    pg = pl.program_id(1)
    ds = q_ref.shape[0]
    rows = 2 * A_KV_HEADS * 2 * ds
    span = n_pg * PAGE_SIZE

    @pl.when(pg == 0)
    def _():
        m_ref[...] = jnp.full_like(m_ref, NEG_INF)
        l_ref[...] = jnp.zeros_like(l_ref)
        acc_ref[...] = jnp.zeros_like(acc_ref)

    q = q_ref[...]
    qst = jnp.concatenate([_diff_query_rows(q[:, 256 * n:256 * n + 256], ds) for n in range(A_KV_HEADS)], axis=0)
    slope = _const_rows([2.0 ** (-2 * (2 * n + g + 1)) for n in range(A_KV_HEADS) for _c in range(2)
                         for g in range(2)], ds)
    half = rows // A_KV_HEADS

    def attend(score, weigh, width, k0, mask):
        col = lax.broadcasted_iota(I32, (1, width), 1)
        s = jnp.concatenate([score(n, qst[half * n:half * (n + 1)]) for n in range(A_KV_HEADS)],
                            axis=0) + slope * col.astype(F32)
        if mask is not None:
            s = jnp.where(mask(col), s, NEG_INF)
        cj = slope * k0
        m, l, acc = m_ref[...][:, 0:1], l_ref[...][:, 0:1], acc_ref[...]
        mt = jnp.max(s, axis=1, keepdims=True)
        m_new = jnp.maximum(m, mt + cj)
        p = jnp.exp(s - (m_new - cj))
        alpha = jnp.exp(m - m_new)
        l = alpha * l + jnp.sum(p, axis=1, keepdims=True)
        pv = jnp.concatenate([weigh(n, p[half * n:half * (n + 1)].astype(BF16)) for n in range(A_KV_HEADS)],
                             axis=0)
        m_ref[...] = jnp.broadcast_to(m_new, m_ref.shape)
        l_ref[...] = jnp.broadcast_to(l, l_ref.shape)
        acc_ref[...] = alpha * acc + pv

    def past_score(n, q):
        kt = jnp.concatenate([r[128 * n:128 * n + 128, :] for r in k_refs], axis=1).astype(BF16)
        return jnp.dot(q, kt, preferred_element_type=F32)

    def past_weigh(n, p):
        v = jnp.concatenate([r[pl.ds(n, PAGE_SIZE, stride=A_KV_HEADS), :] for r in v_refs], axis=0).astype(BF16)
        return jnp.dot(p, v, preferred_element_type=F32)

    attend(past_score, past_weigh, span, (pg * span).astype(F32), None)

    @pl.when(pg == past // span - 1)
    def _():
        r = lax.broadcasted_iota(I32, (rows, 1), 0) & (ds - 1)
        kn = _pad_rows(kn_ref[...], 128).astype(BF16)
        vn = _pad_rows(vn_ref[...], 128).astype(BF16)
        attend(lambda n, q: _dot_nt(q, kn[:, 128 * n:128 * n + 128]),
               lambda n, p: jnp.dot(p, vn[:, 128 * n:128 * n + 128], preferred_element_type=F32),
               128, jnp.float32(past), lambda col: col <= r)
        lam = _diff_lambda(lam_ref)
        o = acc_ref[...] / l_ref[...][:, 0:1]
        outs = []
        for n in range(A_KV_HEADS):
            base = half * n
            for g in range(2):
                outs.append(o[base + g * ds:base + (g + 1) * ds]
                            - lam * o[base + (2 + g) * ds:base + (3 + g) * ds])
        o_ref[...] = jnp.concatenate(outs, axis=1)


def _diff_sample(p16s, lam, cache_k, cache_v, pt_flat, n_pages):
    db, ds, _ = p16s.shape
    n_pg = PAGES_PER_STEP
    rows = 2 * A_KV_HEADS * 2 * ds
    past = n_pages * PAGE_SIZE
    kern = functools.partial(_diff_sample_kernel, n_pg=n_pg, past=past)

    def page(r):
        return pl.BlockSpec((None, 256, PAGE_SIZE), lambda sb, pg, pt: (pt[sb * n_pages + pg * n_pg + r], 0, 0))

    new = lambda c: pl.BlockSpec((None, ds, 256), lambda sb, pg, pt: (sb, 0, c // 256))
    grid_spec = pltpu.PrefetchScalarGridSpec(
        num_scalar_prefetch=1,
        grid=(db, n_pages // n_pg),
        in_specs=[pl.BlockSpec((8, 128), lambda sb, pg, pt: (0, 0)),
                  pl.BlockSpec((None, ds, 512), lambda sb, pg, pt: (sb, 0, C_AQ // 512)),
                  new(C_AK), new(C_AV)] + [page(r) for r in range(n_pg)] * 2,
        out_specs=pl.BlockSpec((None, ds, 512), lambda sb, pg, pt: (sb, 0, 0)),
        scratch_shapes=[pltpu.VMEM((rows, 128), F32), pltpu.VMEM((rows, 128), F32),
                        pltpu.VMEM((rows, 128), F32)])
    return pl.pallas_call(
        kern, grid_spec=grid_spec,
        out_shape=jax.ShapeDtypeStruct((db, ds, 512), F32),
        compiler_params=_cparams(("arbitrary", "arbitrary")),
        name="diff_sample",
    )(pt_flat, lam, p16s, p16s, p16s, *([cache_k] * n_pg), *([cache_v] * n_pg))


def _dsa_sample_kernel(pt_ref, bq_ref, iq_ref, iw_ref, kn_ref, vn_ref, in_ref, *rest, n_pg, past, topk, idx_bits):
    i_refs, k_refs, v_refs = rest[:n_pg], rest[n_pg:2 * n_pg], rest[2 * n_pg:3 * n_pg]
    o_ref, keys_ref, keys_new_ref, thr_ref, bound_ref, m_ref, l_ref, acc_ref = rest[3 * n_pg:]
    phase, sb, g = pl.program_id(0), pl.program_id(1), pl.program_id(2)
    ds = bq_ref.shape[0]
    rows = B_HEADS * ds
    span = n_pg * PAGE_SIZE
    n_groups = past // span
    cw = keys_ref.shape[2]
    n_sub = span // cw
    all_rows = keys_ref.shape[1]
    row0 = pl.multiple_of(sb * ds, ds)
    r_id = lax.broadcasted_iota(I32, (ds, 1), 0)

    def index_keys(ik, transposed):
        iq = iq_ref[...]
        heads = []
        for h in range(IDX_HEADS):
            p = iq[:, 128 * (h // 2):128 * (h // 2) + 128]
            if h % 2:
                p = pltpu.roll(p, 64, 1)
            heads.append(p[:, 0:64])
        iqst = jnp.concatenate(heads, axis=0).astype(BF16)
        if transposed:
            logits = jnp.dot(iqst, ik.astype(BF16), preferred_element_type=F32)
        else:
            logits = _dot_nt(iqst, ik.astype(BF16))
        w = iw_ref[...]
        sc = jnp.maximum(logits[0:ds], 0.0) * w[:, 0:1]
        for h in range(1, IDX_HEADS):
            sc = sc + jnp.maximum(logits[h * ds:(h + 1) * ds], 0.0) * w[:, h:h + 1]
        return _sort_key(sc * IDX_SCALE)

    @pl.when(phase == 0)
    def _():
        ik = jnp.concatenate([r[...] for r in i_refs], axis=1)
        keys = index_keys(ik, True)
        for c in range(n_sub):
            keys_ref[g * n_sub + c, pl.ds(row0, ds), :] = keys[:, c * cw:(c + 1) * cw]

    @pl.when((phase == 0) & (g == n_groups - 1))
    def _():
        col = lax.broadcasted_iota(I32, (1, 128), 1)
        key_new = index_keys(_pad_rows(in_ref[...][:, 0:64], 128), False)
        keys_new_ref[pl.ds(row0, ds), :] = jnp.where(col <= r_id, key_new, INT_MIN)

    @pl.when((phase == 1) & (sb == 0) & (g == 0))
    def _():
        col = lax.broadcasted_iota(I32, (1, cw), 1)
        col_new = lax.broadcasted_iota(I32, (1, 128), 1) + past

        def count(pred):
            def body(j, acc):
                return acc + _lane_fold(jnp.where(pred(keys_ref[j], col + j * cw), 1.0, 0.0), jnp.add)
            acc = lax.fori_loop(0, keys_ref.shape[0], body, jnp.zeros((all_rows, 128), F32))
            acc = acc + jnp.where(pred(keys_new_ref[...], col_new), 1.0, 0.0)
            return jnp.sum(acc, axis=1, keepdims=True)

        def bit_step(b, t):
            cand = t + lax.shift_left(jnp.int32(1), 31 - b)
            return jnp.where(count(lambda kc, _: kc >= cand) >= topk, cand, t)

        t = lax.fori_loop(0, 32, bit_step, jnp.full((all_rows, 1), INT_MIN, I32))
        thr = jnp.maximum(t, INT_MIN + 1)
        need = topk - count(lambda kc, _: kc > thr)
        excess = count(lambda kc, _: kc == thr) > need

        def tie_bound():
            def tie_step(b, jb):
                cand = jb + lax.shift_left(jnp.int32(1), idx_bits - 1 - b)
                return jnp.where(count(lambda kc, ix: (kc == thr) & (ix < cand)) < need, cand, jb)
            return lax.fori_loop(0, idx_bits, tie_step, jnp.zeros((all_rows, 1), I32))

        big = jnp.full((all_rows, 1), 2 ** 30, I32)
        any_excess = jnp.max(jnp.where(excess, 1.0, 0.0)) > 0.0
        bound = lax.cond(any_excess, lambda: jnp.where(excess, tie_bound(), big), lambda: big)
        thr_ref[...] = jnp.broadcast_to(thr, thr_ref.shape)
        bound_ref[...] = jnp.broadcast_to(bound, bound_ref.shape)

    qst = _dsa_query_rows(bq_ref[...])
    slope = _const_rows([2.0 ** -(h + 1) for h in range(B_HEADS)], ds)

    def attend(qk, weigh, kc, k0):
        width = qk.shape[1]
        col = lax.broadcasted_iota(I32, (1, width), 1)
        thr, bound = thr_ref[pl.ds(row0, ds), :][:, 0:1], bound_ref[pl.ds(row0, ds), :][:, 0:1]
        sel = (kc > thr) | ((kc == thr) & (col + k0 <= bound))
        s = (qk + slope * col.astype(F32)).reshape(B_HEADS, ds, width)
        s = jnp.where(sel[None], s, NEG_INF).reshape(rows, width)
        cj = slope * k0.astype(F32)
        m, l, acc = m_ref[...][:, 0:1], l_ref[...][:, 0:1], acc_ref[...]
        mt = jnp.max(s, axis=1, keepdims=True)
        m_new = jnp.maximum(m, mt + cj)
        m_safe = jnp.where(m_new == NEG_INF, 0.0, m_new)
        p = jnp.exp(s - (m_safe - cj))
        alpha = jnp.exp(m - m_safe)
        m_ref[...] = jnp.broadcast_to(m_new, m_ref.shape)
        l_ref[...] = jnp.broadcast_to(alpha * l + jnp.sum(p, axis=1, keepdims=True), l_ref.shape)
        acc_ref[...] = alpha * acc + weigh(p.astype(BF16))

    @pl.when((phase == 1) & (g == 0))
    def _():
        m_ref[...] = jnp.full_like(m_ref, NEG_INF)
        l_ref[...] = jnp.zeros_like(l_ref)
        acc_ref[...] = jnp.zeros_like(acc_ref)

    @pl.when(phase == 1)
    def _():
        kt = jnp.concatenate([r[...] for r in k_refs], axis=1).astype(BF16)
        vt = jnp.concatenate([r[...] for r in v_refs], axis=1).astype(BF16)
        kc = jnp.concatenate([keys_ref[g * n_sub + c, pl.ds(row0, ds), :] for c in range(n_sub)], axis=1)
        attend(jnp.dot(qst, kt, preferred_element_type=F32), lambda p: _dot_nt(p, vt), kc, g * span)

    @pl.when((phase == 1) & (g == n_groups - 1))
    def _():
        kn = _pad_rows(kn_ref[...], 128).astype(BF16)
        vn = _pad_rows(vn_ref[...], 128).astype(BF16)
        attend(_dot_nt(qst, kn), lambda p: jnp.dot(p, vn, preferred_element_type=F32),
               keys_new_ref[pl.ds(row0, ds), :], jnp.int32(past))
        o_ref[...] = _dsa_merge_heads(acc_ref[...] / l_ref[...][:, 0:1], ds)


def _dsa_sample(p16s, p32s, cache_i, cache_k, cache_v, pt_flat, n_pages, topk):
    db, ds, _ = p16s.shape
    n_pg = PAGES_PER_STEP
    n_groups = n_pages // n_pg
    past = n_pages * PAGE_SIZE
    rows = B_HEADS * ds
    kern = functools.partial(_dsa_sample_kernel, n_pg=n_pg, past=past, topk=float(topk),
                             idx_bits=int(past + 128).bit_length())

    own = pt_flat.reshape(db, n_groups, n_pg)
    park = lambda blk: jnp.broadcast_to(blk[None, None], own.shape)
    table = jnp.stack([jnp.stack([own, park(own[db - 1, n_groups - 1])]),
                       jnp.stack([park(own[0, 0]), own])]).reshape(-1)

    def page(r, feat, stream):
        def index(ph, sb, g, pt):
            return (pt[(((stream * 2 + ph) * db + sb) * n_groups + g) * n_pg + r], 0, 0)
        return pl.BlockSpec((None, feat, PAGE_SIZE), index)

    new = lambda wd, c: pl.BlockSpec((None, ds, wd), lambda ph, sb, g, pt: (sb, 0, c // wd))
    cw = min(DSA_CHUNK, n_pg * PAGE_SIZE)
    grid_spec = pltpu.PrefetchScalarGridSpec(
        num_scalar_prefetch=1,
        grid=(2, db, n_groups),
        in_specs=[new(512, C_BQ), new(512, C_IQ), new(128, C_IW), new(128, C_BK), new(128, C_BV), new(128, C_IK)]
                 + [page(r, 64, 0) for r in range(n_pg)]
                 + [page(r, 128, 1) for r in range(n_pg)] * 2,
        out_specs=pl.BlockSpec((None, ds, 512), lambda ph, sb, g, pt: (jnp.where(ph == 0, 0, sb), 0, 0)),
        scratch_shapes=[pltpu.VMEM((past // cw, db * ds, cw), I32), pltpu.VMEM((db * ds, 128), I32),
                        pltpu.VMEM((db * ds, 128), I32), pltpu.VMEM((db * ds, 128), I32),
                        pltpu.VMEM((rows, 128), F32), pltpu.VMEM((rows, 128), F32), pltpu.VMEM((rows, 128), F32)])
    return pl.pallas_call(
        kern, grid_spec=grid_spec,
        out_shape=jax.ShapeDtypeStruct((db, ds, 512), F32),
        compiler_params=_cparams(("arbitrary", "arbitrary", "arbitrary")),
        name="dsa_sample",
    )(table, p16s, p16s, p32s, p16s, p16s, p16s,
      *([cache_i] * n_pg), *([cache_k] * n_pg), *([cache_v] * n_pg))


def _regroup_weights(w_in):
    sizes = (512, 256, 256, 512, 512, 128, 128, 512, 64, 8, 512)
    offs = [0]
    for s in sizes:
        offs.append(offs[-1] + s)
    aq, ak, av, az, bq, bk, bv, iq, ik, iw, bz = [w_in[..., offs[j]:offs[j + 1]] for j in range(len(sizes))]
    scale = HEAD_DIM ** -0.5
    pad = jnp.zeros(iw.shape[:-1] + (128 - iw.shape[-1],), w_in.dtype)
    return jnp.concatenate([aq * scale, bq * scale, iq, ak, av, bk, bv, ik, ik, az, bz, iw, pad],
                           axis=-1).astype(BF16)


def _pick_tile(n, candidates):
    for c in candidates:
        if n % c == 0:
            return c
    return n


def kernel(x_prompt, x_sample, cache_diff_k, cache_diff_v, cache_dsa_k, cache_dsa_v, cache_idx_k, page_table,
           meta_tokens, ln_in_g, ln_in_b, w_in, w_out, lambda_q1, lambda_k1, lambda_q2, lambda_k2,
           subln_g, ln_g, ln_b):
    nb, seq, d = x_prompt.shape
    db, ds, _ = x_sample.shape
    depth = w_in.shape[0]
    n_pool = cache_diff_k.shape[1]
    n_pages = page_table.shape[1]
    t_real = seq + N_META
    tpad = -(-t_real // DSA_CHUNK) * DSA_CHUNK
    assert ds == 8 and n_pages % PAGES_PER_STEP == 0 and d == 1024
    alpha = (2 * depth) ** 0.25
    topk_p = min(TOPK_MAX, seq // 4)
    topk_s = min(TOPK_MAX, (n_pages * PAGE_SIZE + ds) // 4)

    meta = jnp.broadcast_to(meta_tokens[None].astype(x_prompt.dtype), (nb, N_META, d))
    xp = jnp.concatenate([meta, x_prompt, jnp.zeros((nb, tpad - t_real, d), x_prompt.dtype)], axis=1)
    xp = _layernorm_rows(xp.reshape(nb * tpad, d), ln_in_g, ln_in_b, KV_CHUNK).reshape(nb, tpad, d)
    xs = _layernorm_rows(x_sample.reshape(db * ds, d), ln_in_g, ln_in_b,
                         _pick_tile(db * ds, (256, 128, 8))).reshape(1, db * ds, d)

    w = _regroup_weights(w_in)
    w_o = w_out.astype(BF16)
    pad64 = lambda v: jnp.pad(v, ((0, 0), (0, 128 - v.shape[-1])))
    ck = cache_diff_k.transpose(0, 1, 3, 4, 5, 2).reshape(depth * n_pool, 256, PAGE_SIZE)
    cv = cache_diff_v.reshape(depth * n_pool, PAGE_SIZE * A_KV_HEADS, 128)
    cbk = cache_dsa_k.transpose(0, 1, 3, 4, 2).reshape(depth * n_pool, 128, PAGE_SIZE)
    cbv = cache_dsa_v.transpose(0, 1, 3, 4, 2).reshape(depth * n_pool, 128, PAGE_SIZE)
    ci = cache_idx_k.transpose(0, 1, 3, 2).reshape(depth * n_pool, IDX_DIM, PAGE_SIZE)
    pt_flat = page_table.reshape(-1).astype(I32)
    tm_p = _pick_tile(tpad, (256, 128))
    tm_s = _pick_tile(db * ds, (256, 128, 8))

    rows_p = rows_s = None
    for layer in range(depth):
        lam_init = 0.8 - 0.6 * math.exp(-0.3 * layer)
        consts = jnp.stack([jnp.full((128,), lam_init, F32), jnp.full((128,), 1.0 - lam_init, F32),
                            jnp.zeros((128,), F32), jnp.zeros((128,), F32)])
        lam = jnp.concatenate([pad64(jnp.stack([lambda_q1[layer], lambda_k1[layer],
                                                 lambda_q2[layer], lambda_k2[layer]])), consts], axis=0)

        p16, p32, *rows_p = _project(xp, w, layer, depth, rows_p, t_real, tm_p, BF16, "proj_prompt")
        ya = _diff_prompt(p16, lam, t_real)
        yb = _dsa_prompt(p16, p32, t_real, topk_p)
        xp = _merge(xp, ya, yb, p32, w_o, layer, lam, subln_g[layer], ln_g[layer], ln_b[layer],
                    alpha, tm_p, "merge_prompt")

        p16s, p32s, *rows_s = _project(xs, w, layer, depth, rows_s, db * ds, tm_s, F32, "proj_sample")
        p16s3 = p16s.reshape(db, ds, W16)
        pt_layer = pt_flat + layer * n_pool
        ya_s = _diff_sample(p16s3, lam, ck, cv, pt_layer, n_pages)
        yb_s = _dsa_sample(p16s3, p32s.reshape(db, ds, W32), ci, cbk, cbv, pt_layer, n_pages, topk_s)
        xs = _merge(xs, ya_s.reshape(1, db * ds, 512), yb_s.reshape(1, db * ds, 512), p32s, w_o, layer, lam,
                    subln_g[layer], ln_g[layer], ln_b[layer], alpha, tm_s, "merge_sample")

    def cache_rows(stacks, lead, t):
        k, v, bk, bv, ik = stacks
        return (k.reshape(depth, -1, A_KV_HEADS, 2, HEAD_DIM, t).transpose(0, 1, 5, 2, 3, 4)
                 .reshape((depth,) + lead + (A_KV_HEADS, 2, HEAD_DIM)),
                v.reshape((depth,) + lead + (A_KV_HEADS, 2 * HEAD_DIM)),
                bk.reshape(depth, -1, B_KV_HEADS, HEAD_DIM, t).transpose(0, 1, 4, 2, 3)
                  .reshape((depth,) + lead + (B_KV_HEADS, HEAD_DIM)),
                bv.reshape(depth, -1, B_KV_HEADS, HEAD_DIM, t).transpose(0, 1, 4, 2, 3)
                  .reshape((depth,) + lead + (B_KV_HEADS, HEAD_DIM)),
                ik.transpose(0, 1, 3, 2).reshape((depth,) + lead + (IDX_DIM,)))

    return ((xp[:, N_META:t_real], xs.reshape(db, ds, d))
            + cache_rows(rows_p, (nb, t_real), t_real) + cache_rows(rows_s, (db, ds), db * ds))
```

```python
import functools
import math

import jax
import jax.numpy as jnp
from jax import lax
from jax.experimental import pallas as pl
from jax.experimental.pallas import tpu as pltpu

N_META = 16
HEAD_DIM = 64
A_HEADS = 4
A_KV_HEADS = 2
B_HEADS = 8
B_KV_HEADS = 2
IDX_HEADS = 8
IDX_DIM = 64
IDX_SCALE = (IDX_HEADS * IDX_DIM) ** -0.5
TOPK_MAX = 256
PAGE_SIZE = 128
LN_EPS = 1e-5

Q_TILE = 128
DIFF_Q_TILE = 256
KV_CHUNK = 512
DSA_CHUNK = 512
PAGES_PER_STEP = 32
VMEM_LIMIT_BYTES = 56 * 1024 * 1024

LOG2E = math.log2(math.e)
INT_MIN = -(2 ** 31)
NEG_INF = float("-inf")

C_AQ, C_BQ, C_IQ, C_AK, C_AV, C_BK, C_BV, C_IK = 0, 512, 1024, 1536, 1792, 2048, 2176, 2304
W16 = 2432
C_AZ, C_BZ, C_IW = 0, 512, 1024
W32 = 1152
NCOL = W16 + W32

F32 = jnp.float32
BF16 = jnp.bfloat16
I32 = jnp.int32


def _dot_nt(a, b):
    return lax.dot_general(a, b, (((1,), (1,)), ((), ())), preferred_element_type=F32)


def _cparams(sem):
    return pltpu.CompilerParams(dimension_semantics=sem, vmem_limit_bytes=VMEM_LIMIT_BYTES)


def _sort_key(score):
    bits = lax.bitcast_convert_type(score, I32)
    return bits ^ ((bits >> 31) & 0x7FFFFFFF)


def _lane_lo(width=128):
    return lax.broadcasted_iota(I32, (1, width), 1) < 64


def _ln_kernel(x_ref, g_ref, b_ref, o_ref):
    x = x_ref[...]
    mu = jnp.mean(x, axis=-1, keepdims=True)
    xc = x - mu
    var = jnp.mean(xc * xc, axis=-1, keepdims=True)
    o_ref[...] = xc * lax.rsqrt(var + LN_EPS) * g_ref[...] + b_ref[...]


def _layernorm_rows(x, g, b, tm):
    rows, d = x.shape
    return pl.pallas_call(
        _ln_kernel,
        grid=(rows // tm,),
        in_specs=[pl.BlockSpec((tm, d), lambda i: (i, 0)),
                  pl.BlockSpec((1, d), lambda i: (0, 0)),
                  pl.BlockSpec((1, d), lambda i: (0, 0))],
        out_specs=pl.BlockSpec((tm, d), lambda i: (i, 0)),
        out_shape=jax.ShapeDtypeStruct((rows, d), F32),
        compiler_params=_cparams(("arbitrary",)),
        name="ln_in",
    )(x, g.reshape(1, d), b.reshape(1, d))


def _proj_kernel(x_ref, w_ref, *rest, last_out_tile):
    p16_ref, p32_ref, ak_ref, av_ref, bk_ref, bv_ref, ik_ref = rest[-7:]
    xb = x_ref[...].astype(BF16)
    res = jnp.dot(xb, w_ref[...], preferred_element_type=F32)
    p16_ref[...] = res[:, :W16].astype(p16_ref.dtype)
    p32_ref[...] = res[:, W16:]

    @pl.when(pl.program_id(1) <= last_out_tile)
    def _():
        tm = res.shape[0]
        ak_ref[...] = res[:, C_AK:C_AK + 256].T
        for n in range(A_KV_HEADS):
            av_ref[pl.ds(n, tm, stride=A_KV_HEADS), :] = res[:, C_AV + 128 * n:C_AV + 128 * (n + 1)]
        bk_ref[...] = res[:, C_BK:C_BK + 128].T
        bv_ref[...] = res[:, C_BV:C_BV + 128].T
        ik_ref[...] = res[:, C_IK:C_IK + 128].T[0:IDX_DIM]


def _project(x, w, layer, depth, stacks, t_out, tm, p16_dtype, name):
    nb, t, d = x.shape
    last = pl.cdiv(t_out, tm) - 1
    row = lambda wd: pl.BlockSpec((None, tm, wd), lambda b, i: (b, i, 0))
    out = lambda ft: pl.BlockSpec((None, None, ft, tm), lambda b, i: (layer, b, 0, jnp.minimum(i, last)))
    shp = lambda ft: jax.ShapeDtypeStruct((depth, nb, ft, t_out), F32)
    out_v = pl.BlockSpec((None, None, A_KV_HEADS * tm, 128), lambda b, i: (layer, b, jnp.minimum(i, last), 0))
    shp_v = jax.ShapeDtypeStruct((depth, nb, A_KV_HEADS * t_out, 128), F32)
    stacks = list(stacks or ())
    return pl.pallas_call(
        functools.partial(_proj_kernel, last_out_tile=last),
        grid=(nb, t // tm),
        in_specs=[row(d), pl.BlockSpec((None, d, NCOL), lambda b, i: (layer, 0, 0))]
                 + [pl.BlockSpec(memory_space=pl.ANY)] * len(stacks),
        out_specs=[row(W16), row(W32), out(256), out_v, out(128), out(128), out(IDX_DIM)],
        out_shape=[jax.ShapeDtypeStruct((nb, t, W16), p16_dtype),
                   jax.ShapeDtypeStruct((nb, t, W32), F32),
                   shp(256), shp_v, shp(128), shp(128), shp(IDX_DIM)],
        input_output_aliases={2 + j: 2 + j for j in range(len(stacks))},
        compiler_params=_cparams(("arbitrary", "arbitrary")),
        name=name,
    )(x, w, *stacks)


def _diff_lambda(lam_ref):
    r = lam_ref[...]
    e1 = jnp.exp(jnp.sum(r[0:1] * r[1:2], axis=1, keepdims=True))
    e2 = jnp.exp(jnp.sum(r[2:3] * r[3:4], axis=1, keepdims=True))
    return e1 - e2 + r[4:5, 0:1]


def _half_select(x, keep_low):
    lo = _lane_lo()
    return jnp.where(lo, x, 0.0) if keep_low else jnp.where(lo, 0.0, x)


def _diff_query_rows(q, n_rows):
    pieces = [q[:, 0:128], q[:, 128:256]]
    return jnp.concatenate([_half_select(p, True) for p in pieces] +
                           [_half_select(p, False) for p in pieces], axis=0).astype(BF16)


def _dsa_query_rows(q):
    out = []
    for h in range(B_HEADS):
        n = h // (B_HEADS // B_KV_HEADS)
        p = q[:, 128 * (h // 2):128 * (h // 2) + 128]
        if h % 2 != n:
            p = pltpu.roll(p, 64, 1)
        out.append(_half_select(p, n == 0))
    return jnp.concatenate(out, axis=0).astype(BF16)


def _dsa_merge_heads(o, r):
    outs = []
    lo = _lane_lo()
    for j in range(B_HEADS // 2):
        n = j // 2
        oe, oo = o[2 * j * r:(2 * j + 1) * r], o[(2 * j + 1) * r:(2 * j + 2) * r]
        if n == 0:
            outs.append(jnp.where(lo, oe, pltpu.roll(oo, 64, 1)))
        else:
            outs.append(jnp.where(lo, pltpu.roll(oe, 64, 1), oo))
    return jnp.concatenate(outs, axis=1)


def _const_rows(values, r):
    return jnp.concatenate([jnp.full((r, 1), v, F32) for v in values], axis=0)


def _count(mask):
    return jnp.sum(jnp.where(mask, 1.0, 0.0), axis=1, keepdims=True)


def _lane_fold(x, op):
    acc = x[:, 0:128]
    for c in range(1, x.shape[1] // 128):
        acc = op(acc, x[:, 128 * c:128 * (c + 1)])
    return acc


def _for_chunks_paired(n, body):
    def pair(jj, _):
        body(2 * jj, 0)
        body(2 * jj + 1, 1)
        return 0

    lax.fori_loop(0, n // 2, pair, 0)

    @pl.when(n % 2 == 1)
    def _():
        body(n - 1, 0)


def _store_scores(j, qk, shifts, tq, s_ref, m_ref):
    for r, shift in enumerate(shifts):
        rows = slice(r * tq, (r + 1) * tq)
        s = qk[rows] * LOG2E + shift
        s_ref[j, rows, :] = s
        m_ref[rows, :] = jnp.maximum(m_ref[rows, :], _lane_fold(s, jnp.maximum))


def _softmax_weigh(n_chunks, tc, rows, s_ref, v_ref, m_ref, acc_ref, lacc_ref):
    m = jnp.broadcast_to(jnp.max(m_ref[...], axis=1, keepdims=True), (rows, 128))
    acc_ref[...] = jnp.zeros_like(acc_ref)
    lacc_ref[...] = jnp.zeros_like(lacc_ref)

    def weigh(j):
        s = s_ref[j]
        p = jnp.concatenate([jnp.exp2(s[:, 128 * c:128 * (c + 1)] - m) for c in range(tc // 128)], axis=1)
        pv = jnp.dot(p.astype(BF16), v_ref[pl.ds(pl.multiple_of(j * tc, tc), tc), :], preferred_element_type=F32)
        return _lane_fold(p, jnp.add), pv

    def step(j, _slot):
        l, pv = weigh(j)
        lacc_ref[...] += l
        acc_ref[...] += pv

    _for_chunks_paired(n_chunks, step)
    return acc_ref[...] / jnp.sum(lacc_ref[...], axis=1, keepdims=True)


def _diff_prompt_kernel(lam_ref, q_ref, k_ref, v_ref, o_ref, s_ref, m_ref, acc_ref, lacc_ref, *, n_real_tiles):
    n = pl.program_id(1)
    i = pl.program_id(2)
    tq, tc = DIFF_Q_TILE, KV_CHUNK

    @pl.when(i >= n_real_tiles)
    def _():
        o_ref[...] = jnp.zeros_like(o_ref)

    @pl.when(i < n_real_tiles)
    def _():
        lam = _diff_lambda(lam_ref)
        qst = _diff_query_rows(q_ref[...].astype(F32), tq)
        slope_g = (jnp.where(n == 0, 2.0 ** -2, 2.0 ** -6), jnp.where(n == 0, 2.0 ** -4, 2.0 ** -8))
        qpos = i * tq + lax.broadcasted_iota(I32, (tq, 1), 0)
        col = lax.broadcasted_iota(I32, (1, tc), 1)
        m_ref[...] = jnp.full_like(m_ref, NEG_INF)

        def score_step(j, masked):
            k0 = pl.multiple_of(j * tc, tc)
            qk = _dot_nt(qst, k_ref[pl.ds(k0, tc), :])
            kpos = (col + k0).astype(F32)
            shifts = [(slope_g[r % 2] * LOG2E) * kpos for r in range(4)]
            if masked:
                causal = jnp.where(col + k0 <= qpos, 0.0, NEG_INF)
                shifts = [sh + causal for sh in shifts[:2]] * 2
            _store_scores(j, qk, shifts, tq, s_ref, m_ref)

        n_full = (i * tq) // tc
        _for_chunks_paired(n_full, lambda j, _slot: score_step(j, False))
        score_step(n_full, True)
        o = _softmax_weigh(n_full + 1, tc, 4 * tq, s_ref, v_ref, m_ref, acc_ref, lacc_ref)
        out_g0 = o[0:tq] - lam * o[2 * tq:3 * tq]
        out_g1 = o[tq:2 * tq] - lam * o[3 * tq:4 * tq]
        o_ref[...] = jnp.concatenate([out_g0, out_g1], axis=1)


def _diff_prompt(p16, lam, t_real):
    nb, tpad, _ = p16.shape
    tq = DIFF_Q_TILE
    kern = functools.partial(_diff_prompt_kernel, n_real_tiles=pl.cdiv(t_real, tq))
    return pl.pallas_call(
        kern,
        grid=(nb, A_KV_HEADS, tpad // tq),
        in_specs=[pl.BlockSpec((8, 128), lambda b, n, i: (0, 0)),
                  pl.BlockSpec((None, tq, 256), lambda b, n, i: (b, i, C_AQ // 256 + n)),
                  pl.BlockSpec((None, tpad, 128), lambda b, n, i: (b, 0, C_AK // 128 + n)),
                  pl.BlockSpec((None, tpad, 128), lambda b, n, i: (b, 0, C_AV // 128 + n))],
        out_specs=pl.BlockSpec((None, tq, 256), lambda b, n, i: (b, i, n)),
        out_shape=jax.ShapeDtypeStruct((nb, tpad, 512), F32),
        scratch_shapes=[pltpu.VMEM((tpad // KV_CHUNK, 4 * tq, KV_CHUNK), F32), pltpu.VMEM((4 * tq, 128), F32),
                        pltpu.VMEM((4 * tq, 128), F32), pltpu.VMEM((4 * tq, 128), F32)],
        compiler_params=_cparams(("arbitrary", "arbitrary", "arbitrary")),
        name="diff_prompt",
    )(lam, p16, p16, p16)


def _dsa_prompt_kernel(bq_ref, iq_ref, iw_ref, bk_ref, bv_ref, ik_ref, o_ref,
                       keys_ref, keys_t_ref, mask_ref, s_ref, m_ref, acc_ref, lacc_ref, *,
                       n_real_tiles, topk, idx_bits):
    i = pl.program_id(1)
    tq, tc = Q_TILE, DSA_CHUNK

    @pl.when(i >= n_real_tiles)
    def _():
        o_ref[...] = jnp.zeros_like(o_ref)

    @pl.when(i < n_real_tiles)
    def _():
        n_chunks = (i * tq) // tc + 1
        col = lax.broadcasted_iota(I32, (1, tc), 1)
        qpos = i * tq + lax.broadcasted_iota(I32, (tq, 1), 0)

        iq = iq_ref[...].astype(F32)
        iqst = jnp.concatenate(
            [_half_select(iq[:, 128 * (h // 2):128 * (h // 2) + 128], h % 2 == 0)
             for h in range(IDX_HEADS)], axis=0).astype(BF16)
        w = iw_ref[...]
        wcol = [jnp.broadcast_to(w[:, h:h + 1], (tq, tc)) for h in range(IDX_HEADS)]

        def index_step(j, _slot):
            k0 = pl.multiple_of(j * tc, tc)
            logits = _dot_nt(iqst, ik_ref[pl.ds(k0, tc), :])
            sc = jnp.maximum(logits[0:tq], 0.0) * wcol[0]
            for h in range(1, IDX_HEADS):
                sc = sc + jnp.maximum(logits[h * tq:(h + 1) * tq], 0.0) * wcol[h]
            key = jnp.where(col + k0 <= qpos, _sort_key(sc * IDX_SCALE), INT_MIN)
            keys_ref[j] = key
            keys_t_ref[j] = key.T

        _for_chunks_paired(n_chunks, index_step)

        kidx = lax.broadcasted_iota(I32, (tc, 1), 0)

        def count_over_chunks(pred):
            def body(j, acc):
                hit = jnp.where(pred(keys_t_ref[j], kidx + j * tc), 1.0, 0.0)
                parts = [hit[8 * r:8 * (r + 1)] for r in range(tc // 8)]
                while len(parts) > 1:
                    parts = [parts[r] + parts[r + 1] for r in range(0, len(parts), 2)]
                return acc + parts[0]
            acc = lax.fori_loop(0, n_chunks, body, jnp.zeros((8, tq), F32))
            return jnp.sum(acc, axis=0, keepdims=True)

        def bit_step(b, carry):
            t, n_ge = carry
            cand = t + lax.shift_left(jnp.int32(1), 31 - b)
            cnt = count_over_chunks(lambda kc, _: kc >= cand)
            return jnp.where(cnt >= topk, cand, t), jnp.where(cnt >= topk, cnt, n_ge)

        t, n_ge = lax.fori_loop(0, 32, bit_step, (jnp.full((1, tq), INT_MIN, I32), jnp.zeros((1, tq), F32)))
        thr_t = jnp.maximum(t, INT_MIN + 1)
        n_gt = count_over_chunks(lambda kc, _: kc > thr_t)
        need = topk - n_gt
        n_eq = jnp.where(t == INT_MIN, 0.0, n_ge - n_gt)
        excess = n_eq > need

        def tie_bound():
            def tie_step(b, jb):
                cand = jb + lax.shift_left(jnp.int32(1), idx_bits - 1 - b)
                cnt = count_over_chunks(lambda kc, idx: (kc == thr_t) & (idx < cand))
                return jnp.where(cnt < need, cand, jb)
            return lax.fori_loop(0, idx_bits, tie_step, jnp.zeros((1, tq), I32))

        any_excess = jnp.max(jnp.where(excess, 1.0, 0.0)) > 0.0
        big = jnp.full((1, tq), 2 ** 30, I32)
        bound_t = lax.cond(any_excess, lambda: jnp.where(excess, tie_bound(), big), lambda: big)
        to_col = lambda v: jnp.broadcast_to(v, (tq, tq)).T[:, 0:1]
        thr, bound = to_col(thr_t), to_col(bound_t)

        qst = _dsa_query_rows(bq_ref[...].astype(F32))

        m_ref[...] = jnp.full_like(m_ref, NEG_INF)

        def score_step(j, slot):
            k0 = pl.multiple_of(j * tc, tc)
            kc = keys_ref[j]
            sel = (kc > thr) | ((kc == thr) & (col + k0 <= bound))
            mask_ref[slot] = jnp.where(sel, 0.0, NEG_INF)
            qk = _dot_nt(qst, bk_ref[pl.ds(k0, tc), :])
            kpos = (col + k0).astype(F32)
            _store_scores(j, qk, [(2.0 ** -(h + 1) * LOG2E) * kpos + mask_ref[slot] for h in range(B_HEADS)],
                          tq, s_ref, m_ref)

        _for_chunks_paired(n_chunks, score_step)
        o = _softmax_weigh(n_chunks, tc, B_HEADS * tq, s_ref, bv_ref, m_ref, acc_ref, lacc_ref)
        o_ref[...] = _dsa_merge_heads(o, tq)


def _dsa_prompt(p16, p32, t_real, topk):
    nb, tpad, _ = p16.shape
    nq = tpad // Q_TILE
    kern = functools.partial(_dsa_prompt_kernel, n_real_tiles=pl.cdiv(t_real, Q_TILE), topk=float(topk),
                             idx_bits=max(1, int(tpad).bit_length()))
    whole = lambda c: pl.BlockSpec((None, tpad, 128), lambda b, i: (b, 0, c // 128))
    return pl.pallas_call(
        kern,
        grid=(nb, nq),
        in_specs=[pl.BlockSpec((None, Q_TILE, 512), lambda b, i: (b, i, C_BQ // 512)),
                  pl.BlockSpec((None, Q_TILE, 512), lambda b, i: (b, i, C_IQ // 512)),
                  pl.BlockSpec((None, Q_TILE, 128), lambda b, i: (b, i, C_IW // 128)),
                  whole(C_BK), whole(C_BV), whole(C_IK)],
        out_specs=pl.BlockSpec((None, Q_TILE, 512), lambda b, i: (b, i, 0)),
        out_shape=jax.ShapeDtypeStruct((nb, tpad, 512), F32),
        scratch_shapes=[pltpu.VMEM((tpad // DSA_CHUNK, Q_TILE, DSA_CHUNK), I32),
                        pltpu.VMEM((tpad // DSA_CHUNK, DSA_CHUNK, Q_TILE), I32),
                        pltpu.VMEM((2, Q_TILE, DSA_CHUNK), F32),
                        pltpu.VMEM((tpad // DSA_CHUNK, B_HEADS * Q_TILE, DSA_CHUNK), F32),
                        pltpu.VMEM((B_HEADS * Q_TILE, 128), F32), pltpu.VMEM((B_HEADS * Q_TILE, 128), F32),
                        pltpu.VMEM((B_HEADS * Q_TILE, 128), F32)],
        compiler_params=_cparams(("arbitrary", "arbitrary")),
        name="dsa_prompt",
    )(p16, p16, p32, p16, p16, p16)


def _merge_kernel(sc_ref, x_ref, ya_ref, yb_ref, az_ref, bz_ref, w_ref, sg_ref, g_ref, b_ref, o_ref, *, alpha):
    one_minus_lam_init = sc_ref[5:6, 0:1]
    ya = ya_ref[...]
    parts = []
    for h in range(A_HEADS):
        o = ya[:, 128 * h:128 * h + 128]
        y = o * lax.rsqrt(jnp.mean(o * o, axis=-1, keepdims=True) + LN_EPS) * sg_ref[...]
        parts.append(y * one_minus_lam_init)
    az = az_ref[...]
    bz = bz_ref[...]
    ya_g = jnp.concatenate(parts, axis=1) * (az * (1.0 / (1.0 + jnp.exp(-az))))
    yb_g = yb_ref[...] * (bz * (1.0 / (1.0 + jnp.exp(-bz))))
    mix = jnp.concatenate([ya_g, yb_g], axis=1).astype(BF16)
    y = alpha * x_ref[...] + jnp.dot(mix, w_ref[...], preferred_element_type=F32)
    mu = jnp.mean(y, axis=-1, keepdims=True)
    yc = y - mu
    var = jnp.mean(yc * yc, axis=-1, keepdims=True)
    o_ref[...] = yc * lax.rsqrt(var + LN_EPS) * g_ref[...] + b_ref[...]


def _merge(x, ya, yb, p32, w_out, layer, sc, sub_g, ln_g, ln_b, alpha, tm, name):
    nb, t, d = x.shape
    row = lambda wd, c: pl.BlockSpec((None, tm, wd), lambda b, i: (b, i, c))
    vec = lambda wd: pl.BlockSpec((1, wd), lambda b, i: (0, 0))
    return pl.pallas_call(
        functools.partial(_merge_kernel, alpha=alpha),
        grid=(nb, t // tm),
        in_specs=[pl.BlockSpec((8, 128), lambda b, i: (0, 0)),
                  row(d, 0), row(512, 0), row(512, 0), row(512, C_AZ // 512), row(512, C_BZ // 512),
                  pl.BlockSpec((None, d, d), lambda b, i: (layer, 0, 0)),
                  vec(128), vec(d), vec(d)],
        out_specs=row(d, 0),
        out_shape=jax.ShapeDtypeStruct((nb, t, d), F32),
        compiler_params=_cparams(("arbitrary", "arbitrary")),
        name=name,
    )(sc, x, ya, yb, p32, p32, w_out, sub_g.reshape(1, 128), ln_g.reshape(1, d), ln_b.reshape(1, d))


def _pad_rows(x, rows):
    return jnp.concatenate([x, jnp.zeros((rows - x.shape[0], x.shape[1]), x.dtype)], axis=0)


def _diff_sample_kernel(pt_ref, lam_ref, q_ref, kn_ref, vn_ref, *rest, n_pg, past):
    k_refs, v_refs = rest[:n_pg], rest[n_pg:2 * n_pg]
    o_ref, m_ref, l_ref, acc_ref = rest[2 * n_pg:]
    pg = pl.program_id(1)
    ds = q_ref.shape[0]
    rows = 2 * A_KV_HEADS * 2 * ds
    span = n_pg * PAGE_SIZE

    @pl.when(pg == 0)
    def _():
        m_ref[...] = jnp.full_like(m_ref, NEG_INF)
        l_ref[...] = jnp.zeros_like(l_ref)
        acc_ref[...] = jnp.zeros_like(acc_ref)

    q = q_ref[...]
    qst = jnp.concatenate([_diff_query_rows(q[:, 256 * n:256 * n + 256], ds) for n in range(A_KV_HEADS)], axis=0)
    slope = _const_rows([2.0 ** (-2 * (2 * n + g + 1)) for n in range(A_KV_HEADS) for _c in range(2)
                         for g in range(2)], ds)
    half = rows // A_KV_HEADS

    def attend(score, weigh, width, k0, mask):
        col = lax.broadcasted_iota(I32, (1, width), 1)
        s = jnp.concatenate([score(n, qst[half * n:half * (n + 1)]) for n in range(A_KV_HEADS)],
                            axis=0) + slope * col.astype(F32)
        if mask is not None:
            s = jnp.where(mask(col), s, NEG_INF)
        cj = slope * k0
        m, l, acc = m_ref[...][:, 0:1], l_ref[...][:, 0:1], acc_ref[...]
        mt = jnp.max(s, axis=1, keepdims=True)
        m_new = jnp.maximum(m, mt + cj)
        p = jnp.exp(s - (m_new - cj))
        alpha = jnp.exp(m - m_new)
        l = alpha * l + jnp.sum(p, axis=1, keepdims=True)
        pv = jnp.concatenate([weigh(n, p[half * n:half * (n + 1)].astype(BF16)) for n in range(A_KV_HEADS)],
                             axis=0)
        m_ref[...] = jnp.broadcast_to(m_new, m_ref.shape)
        l_ref[...] = jnp.broadcast_to(l, l_ref.shape)
        acc_ref[...] = alpha * acc + pv

    def past_score(n, q):
        kt = jnp.concatenate([r[128 * n:128 * n + 128, :] for r in k_refs], axis=1).astype(BF16)
        return jnp.dot(q, kt, preferred_element_type=F32)

    def past_weigh(n, p):
        v = jnp.concatenate([r[pl.ds(n, PAGE_SIZE, stride=A_KV_HEADS), :] for r in v_refs], axis=0).astype(BF16)
        return jnp.dot(p, v, preferred_element_type=F32)

    attend(past_score, past_weigh, span, (pg * span).astype(F32), None)

    @pl.when(pg == past // span - 1)
    def _():
        r = lax.broadcasted_iota(I32, (rows, 1), 0) & (ds - 1)
        kn = _pad_rows(kn_ref[...], 128).astype(BF16)
        vn = _pad_rows(vn_ref[...], 128).astype(BF16)
        attend(lambda n, q: _dot_nt(q, kn[:, 128 * n:128 * n + 128]),
               lambda n, p: jnp.dot(p, vn[:, 128 * n:128 * n + 128], preferred_element_type=F32),
               128, jnp.float32(past), lambda col: col <= r)
        lam = _diff_lambda(lam_ref)
        o = acc_ref[...] / l_ref[...][:, 0:1]
        outs = []
        for n in range(A_KV_HEADS):
            base = half * n
            for g in range(2):
                outs.append(o[base + g * ds:base + (g + 1) * ds]
                            - lam * o[base + (2 + g) * ds:base + (3 + g) * ds])
        o_ref[...] = jnp.concatenate(outs, axis=1)


def _diff_sample(p16s, lam, cache_k, cache_v, pt_flat, n_pages):
    db, ds, _ = p16s.shape
    n_pg = PAGES_PER_STEP
    rows = 2 * A_KV_HEADS * 2 * ds
    past = n_pages * PAGE_SIZE
    kern = functools.partial(_diff_sample_kernel, n_pg=n_pg, past=past)

    def page(r):
        return pl.BlockSpec((None, 256, PAGE_SIZE), lambda sb, pg, pt: (pt[sb * n_pages + pg * n_pg + r], 0, 0))

    new = lambda c: pl.BlockSpec((None, ds, 256), lambda sb, pg, pt: (sb, 0, c // 256))
    grid_spec = pltpu.PrefetchScalarGridSpec(
        num_scalar_prefetch=1,
        grid=(db, n_pages // n_pg),
        in_specs=[pl.BlockSpec((8, 128), lambda sb, pg, pt: (0, 0)),
                  pl.BlockSpec((None, ds, 512), lambda sb, pg, pt: (sb, 0, C_AQ // 512)),
                  new(C_AK), new(C_AV)] + [page(r) for r in range(n_pg)] * 2,
        out_specs=pl.BlockSpec((None, ds, 512), lambda sb, pg, pt: (sb, 0, 0)),
        scratch_shapes=[pltpu.VMEM((rows, 128), F32), pltpu.VMEM((rows, 128), F32),
                        pltpu.VMEM((rows, 128), F32)])
    return pl.pallas_call(
        kern, grid_spec=grid_spec,
        out_shape=jax.ShapeDtypeStruct((db, ds, 512), F32),
        compiler_params=_cparams(("arbitrary", "arbitrary")),
        name="diff_sample",
    )(pt_flat, lam, p16s, p16s, p16s, *([cache_k] * n_pg), *([cache_v] * n_pg))


def _dsa_sample_index_kernel(pt_ref, iq_ref, iw_ref, in_ref, *rest, n_pg, past, topk, idx_bits):
    i_refs = rest[:n_pg]
    keys_ref, keys_new_ref, thr_ref, bound_ref = rest[n_pg:]
    sb, g = pl.program_id(0), pl.program_id(1)
    ds = iq_ref.shape[0]
    span = n_pg * PAGE_SIZE
    n_groups = past // span
    cw = keys_ref.shape[2]
    n_sub = span // cw
    all_rows = keys_ref.shape[1]
    row0 = pl.multiple_of(sb * ds, ds)
    r_id = lax.broadcasted_iota(I32, (ds, 1), 0)

    def index_keys(ik, transposed):
        iq = iq_ref[...]
        heads = []
        for h in range(IDX_HEADS):
            p = iq[:, 128 * (h // 2):128 * (h // 2) + 128]
            if h % 2:
                p = pltpu.roll(p, 64, 1)
            heads.append(p[:, 0:64])
        iqst = jnp.concatenate(heads, axis=0).astype(BF16)
        if transposed:
            logits = jnp.dot(iqst, ik.astype(BF16), preferred_element_type=F32)
        else:
            logits = _dot_nt(iqst, ik.astype(BF16))
        w = iw_ref[...]
        sc = jnp.maximum(logits[0:ds], 0.0) * w[:, 0:1]
        for h in range(1, IDX_HEADS):
            sc = sc + jnp.maximum(logits[h * ds:(h + 1) * ds], 0.0) * w[:, h:h + 1]
        return _sort_key(sc * IDX_SCALE)

    ik = jnp.concatenate([r[...] for r in i_refs], axis=1)
    keys = index_keys(ik, True)
    for c in range(n_sub):
        keys_ref[g * n_sub + c, pl.ds(row0, ds), :] = keys[:, c * cw:(c + 1) * cw]

    @pl.when(g == n_groups - 1)
    def _():
        col = lax.broadcasted_iota(I32, (1, 128), 1)
        key_new = index_keys(_pad_rows(in_ref[...][:, 0:64], 128), False)
        keys_new_ref[pl.ds(row0, ds), :] = jnp.where(col <= r_id, key_new, INT_MIN)

    @pl.when((sb == pl.num_programs(0) - 1) & (g == n_groups - 1))
    def _():
        col = lax.broadcasted_iota(I32, (1, cw), 1)
        col_new = lax.broadcasted_iota(I32, (1, 128), 1) + past

        def count(pred):
            def body(j, acc):
                return acc + _lane_fold(jnp.where(pred(keys_ref[j], col + j * cw), 1.0, 0.0), jnp.add)
            acc = lax.fori_loop(0, keys_ref.shape[0], body, jnp.zeros((all_rows, 128), F32))
            acc = acc + jnp.where(pred(keys_new_ref[...], col_new), 1.0, 0.0)
            return jnp.sum(acc, axis=1, keepdims=True)

        def bit_step(b, t):
            cand = t + lax.shift_left(jnp.int32(1), 31 - b)
            return jnp.where(count(lambda kc, _: kc >= cand) >= topk, cand, t)

        t = lax.fori_loop(0, 32, bit_step, jnp.full((all_rows, 1), INT_MIN, I32))
        thr = jnp.maximum(t, INT_MIN + 1)
        need = topk - count(lambda kc, _: kc > thr)
        excess = count(lambda kc, _: kc == thr) > need

        def tie_bound():
            def tie_step(b, jb):
                cand = jb + lax.shift_left(jnp.int32(1), idx_bits - 1 - b)
                return jnp.where(count(lambda kc, ix: (kc == thr) & (ix < cand)) < need, cand, jb)
            return lax.fori_loop(0, idx_bits, tie_step, jnp.zeros((all_rows, 1), I32))

        big = jnp.full((all_rows, 1), 2 ** 30, I32)
        any_excess = jnp.max(jnp.where(excess, 1.0, 0.0)) > 0.0
        bound = lax.cond(any_excess, lambda: jnp.where(excess, tie_bound(), big), lambda: big)
        thr_ref[...] = jnp.broadcast_to(thr, thr_ref.shape)
        bound_ref[...] = jnp.broadcast_to(bound, bound_ref.shape)


def _dsa_sample_attend_kernel(pt_ref, bq_ref, kn_ref, vn_ref, keys_ref, keys_new_ref, thr_ref, bound_ref, *rest,
                              n_pg, past):
    k_refs, v_refs = rest[:n_pg], rest[n_pg:2 * n_pg]
    o_ref, m_ref, l_ref, acc_ref = rest[2 * n_pg:]
    g = pl.program_id(1)
    ds = bq_ref.shape[0]
    rows = B_HEADS * ds
    span = n_pg * PAGE_SIZE
    n_groups = past // span
    qst = _dsa_query_rows(bq_ref[...])
    slope = _const_rows([2.0 ** -(h + 1) for h in range(B_HEADS)], ds)

    def attend(qk, weigh, kc, k0):
        width = qk.shape[1]
        col = lax.broadcasted_iota(I32, (1, width), 1)
        thr, bound = thr_ref[...][:, 0:1], bound_ref[...][:, 0:1]
        sel = (kc > thr) | ((kc == thr) & (col + k0 <= bound))
        s = (qk + slope * col.astype(F32)).reshape(B_HEADS, ds, width)
        s = jnp.where(sel[None], s, NEG_INF).reshape(rows, width)
        cj = slope * k0.astype(F32)
        m, l, acc = m_ref[...][:, 0:1], l_ref[...][:, 0:1], acc_ref[...]
        mt = jnp.max(s, axis=1, keepdims=True)
        m_new = jnp.maximum(m, mt + cj)
        m_safe = jnp.where(m_new == NEG_INF, 0.0, m_new)
        p = jnp.exp(s - (m_safe - cj))
        alpha = jnp.exp(m - m_safe)
        m_ref[...] = jnp.broadcast_to(m_new, m_ref.shape)
        l_ref[...] = jnp.broadcast_to(alpha * l + jnp.sum(p, axis=1, keepdims=True), l_ref.shape)
        acc_ref[...] = alpha * acc + weigh(p.astype(BF16))

    @pl.when(g == 0)
    def _():
        m_ref[...] = jnp.full_like(m_ref, NEG_INF)
        l_ref[...] = jnp.zeros_like(l_ref)
        acc_ref[...] = jnp.zeros_like(acc_ref)

    kt = jnp.concatenate([r[...] for r in k_refs], axis=1).astype(BF16)
    vt = jnp.concatenate([r[...] for r in v_refs], axis=1).astype(BF16)
    kc = jnp.concatenate([keys_ref[c] for c in range(keys_ref.shape[0])], axis=1)
    attend(jnp.dot(qst, kt, preferred_element_type=F32), lambda p: _dot_nt(p, vt), kc, g * span)

    @pl.when(g == n_groups - 1)
    def _():
        kn = _pad_rows(kn_ref[...], 128).astype(BF16)
        vn = _pad_rows(vn_ref[...], 128).astype(BF16)
        attend(_dot_nt(qst, kn), lambda p: jnp.dot(p, vn, preferred_element_type=F32),
               keys_new_ref[...], jnp.int32(past))
        o_ref[...] = _dsa_merge_heads(acc_ref[...] / l_ref[...][:, 0:1], ds)


def _dsa_sample(p16s, p32s, cache_i, cache_k, cache_v, pt_flat, n_pages, topk):
    db, ds, _ = p16s.shape
    n_pg = PAGES_PER_STEP
    n_groups = n_pages // n_pg
    past = n_pages * PAGE_SIZE
    rows = B_HEADS * ds
    cw = min(DSA_CHUNK, n_pg * PAGE_SIZE)
    n_sub = n_pg * PAGE_SIZE // cw

    def page(r, feat):
        return pl.BlockSpec((None, feat, PAGE_SIZE), lambda sb, g, pt: (pt[sb * n_pages + g * n_pg + r], 0, 0))

    new = lambda wd, c: pl.BlockSpec((None, ds, wd), lambda sb, g, pt: (sb, 0, c // wd))
    whole = lambda shape: pl.BlockSpec(shape, lambda sb, g, pt: (0,) * len(shape))

    key_shapes = [(past // cw, db * ds, cw), (db * ds, 128), (db * ds, 128), (db * ds, 128)]
    keys, keys_new, thr, bound = pl.pallas_call(
        functools.partial(_dsa_sample_index_kernel, n_pg=n_pg, past=past, topk=float(topk),
                          idx_bits=int(past + 128).bit_length()),
        grid_spec=pltpu.PrefetchScalarGridSpec(
            num_scalar_prefetch=1,
            grid=(db, n_groups),
            in_specs=[new(512, C_IQ), new(128, C_IW), new(128, C_IK)] + [page(r, 64) for r in range(n_pg)],
            out_specs=[whole(s) for s in key_shapes]),
        out_shape=[jax.ShapeDtypeStruct(s, I32) for s in key_shapes],
        compiler_params=_cparams(("arbitrary", "arbitrary")),
        name="dsa_sample_index",
    )(pt_flat, p16s, p32s, p16s, *([cache_i] * n_pg))

    per_seq = pl.BlockSpec((ds, 128), lambda sb, g, pt: (sb, 0))
    return pl.pallas_call(
        functools.partial(_dsa_sample_attend_kernel, n_pg=n_pg, past=past),
        grid_spec=pltpu.PrefetchScalarGridSpec(
            num_scalar_prefetch=1,
            grid=(db, n_groups),
            in_specs=[new(512, C_BQ), new(128, C_BK), new(128, C_BV),
                      pl.BlockSpec((n_sub, ds, cw), lambda sb, g, pt: (g, sb, 0)), per_seq, per_seq, per_seq]
                     + [page(r, 128) for r in range(n_pg)] * 2,
            out_specs=pl.BlockSpec((None, ds, 512), lambda sb, g, pt: (sb, 0, 0)),
            scratch_shapes=[pltpu.VMEM((rows, 128), F32), pltpu.VMEM((rows, 128), F32),
                            pltpu.VMEM((rows, 128), F32)]),
        out_shape=jax.ShapeDtypeStruct((db, ds, 512), F32),
        compiler_params=_cparams(("arbitrary", "arbitrary")),
        name="dsa_sample_attend",
    )(pt_flat, p16s, p16s, p16s, keys, keys_new, thr, bound, *([cache_k] * n_pg), *([cache_v] * n_pg))


def _regroup_weights(w_in):
    sizes = (512, 256, 256, 512, 512, 128, 128, 512, 64, 8, 512)
    offs = [0]
    for s in sizes:
        offs.append(offs[-1] + s)
    aq, ak, av, az, bq, bk, bv, iq, ik, iw, bz = [w_in[..., offs[j]:offs[j + 1]] for j in range(len(sizes))]
    scale = HEAD_DIM ** -0.5
    pad = jnp.zeros(iw.shape[:-1] + (128 - iw.shape[-1],), w_in.dtype)
    return jnp.concatenate([aq * scale, bq * scale, iq, ak, av, bk, bv, ik, ik, az, bz, iw, pad],
                           axis=-1).astype(BF16)


def _pick_tile(n, candidates):
    for c in candidates:
        if n % c == 0:
            return c
    return n


def kernel(x_prompt, x_sample, cache_diff_k, cache_diff_v, cache_dsa_k, cache_dsa_v, cache_idx_k, page_table,
           meta_tokens, ln_in_g, ln_in_b, w_in, w_out, lambda_q1, lambda_k1, lambda_q2, lambda_k2,
           subln_g, ln_g, ln_b):
    nb, seq, d = x_prompt.shape
    db, ds, _ = x_sample.shape
    depth = w_in.shape[0]
    n_pool = cache_diff_k.shape[1]
    n_pages = page_table.shape[1]
    t_real = seq + N_META
    tpad = -(-t_real // DSA_CHUNK) * DSA_CHUNK
    assert ds == 8 and n_pages % PAGES_PER_STEP == 0 and d == 1024
    alpha = (2 * depth) ** 0.25
    topk_p = min(TOPK_MAX, seq // 4)
    topk_s = min(TOPK_MAX, (n_pages * PAGE_SIZE + ds) // 4)

    meta = jnp.broadcast_to(meta_tokens[None].astype(x_prompt.dtype), (nb, N_META, d))
    xp = jnp.concatenate([meta, x_prompt, jnp.zeros((nb, tpad - t_real, d), x_prompt.dtype)], axis=1)
    xp = _layernorm_rows(xp.reshape(nb * tpad, d), ln_in_g, ln_in_b, KV_CHUNK).reshape(nb, tpad, d)
    xs = _layernorm_rows(x_sample.reshape(db * ds, d), ln_in_g, ln_in_b,
                         _pick_tile(db * ds, (256, 128, 8))).reshape(1, db * ds, d)

    w = _regroup_weights(w_in)
    w_o = w_out.astype(BF16)
    pad64 = lambda v: jnp.pad(v, ((0, 0), (0, 128 - v.shape[-1])))
    ck = cache_diff_k.transpose(0, 1, 3, 4, 5, 2).reshape(depth * n_pool, 256, PAGE_SIZE)
    cv = cache_diff_v.reshape(depth * n_pool, PAGE_SIZE * A_KV_HEADS, 128)
    cbk = cache_dsa_k.transpose(0, 1, 3, 4, 2).reshape(depth * n_pool, 128, PAGE_SIZE)
    cbv = cache_dsa_v.transpose(0, 1, 3, 4, 2).reshape(depth * n_pool, 128, PAGE_SIZE)
    ci = cache_idx_k.transpose(0, 1, 3, 2).reshape(depth * n_pool, IDX_DIM, PAGE_SIZE)
    pt_flat = page_table.reshape(-1).astype(I32)
    tm_p = _pick_tile(tpad, (512, 256, 128))
    tm_s = _pick_tile(db * ds, (256, 128, 8))

    rows_p = rows_s = None
    for layer in range(depth):
        lam_init = 0.8 - 0.6 * math.exp(-0.3 * layer)
        consts = jnp.stack([jnp.full((128,), lam_init, F32), jnp.full((128,), 1.0 - lam_init, F32),
                            jnp.zeros((128,), F32), jnp.zeros((128,), F32)])
        lam = jnp.concatenate([pad64(jnp.stack([lambda_q1[layer], lambda_k1[layer],
                                                 lambda_q2[layer], lambda_k2[layer]])), consts], axis=0)

        p16, p32, *rows_p = _project(xp, w, layer, depth, rows_p, t_real, tm_p, BF16, "proj_prompt")
        ya = _diff_prompt(p16, lam, t_real)
        yb = _dsa_prompt(p16, p32, t_real, topk_p)
        xp = _merge(xp, ya, yb, p32, w_o, layer, lam, subln_g[layer], ln_g[layer], ln_b[layer],
                    alpha, tm_p, "merge_prompt")

        p16s, p32s, *rows_s = _project(xs, w, layer, depth, rows_s, db * ds, tm_s, F32, "proj_sample")
        p16s3 = p16s.reshape(db, ds, W16)
        pt_layer = pt_flat + layer * n_pool
        ya_s = _diff_sample(p16s3, lam, ck, cv, pt_layer, n_pages)
        yb_s = _dsa_sample(p16s3, p32s.reshape(db, ds, W32), ci, cbk, cbv, pt_layer, n_pages, topk_s)
        xs = _merge(xs, ya_s.reshape(1, db * ds, 512), yb_s.reshape(1, db * ds, 512), p32s, w_o, layer, lam,
                    subln_g[layer], ln_g[layer], ln_b[layer], alpha, tm_s, "merge_sample")

    def cache_rows(stacks, lead, t):
        k, v, bk, bv, ik = stacks
        return (k.reshape(depth, -1, A_KV_HEADS, 2, HEAD_DIM, t).transpose(0, 1, 5, 2, 3, 4)
                 .reshape((depth,) + lead + (A_KV_HEADS, 2, HEAD_DIM)),
                v.reshape((depth,) + lead + (A_KV_HEADS, 2 * HEAD_DIM)),
                bk.reshape(depth, -1, B_KV_HEADS, HEAD_DIM, t).transpose(0, 1, 4, 2, 3)
                  .reshape((depth,) + lead + (B_KV_HEADS, HEAD_DIM)),
                bv.reshape(depth, -1, B_KV_HEADS, HEAD_DIM, t).transpose(0, 1, 4, 2, 3)
                  .reshape((depth,) + lead + (B_KV_HEADS, HEAD_DIM)),
                ik.transpose(0, 1, 3, 2).reshape((depth,) + lead + (IDX_DIM,)))

    return ((xp[:, N_META:t_real], xs.reshape(db, ds, d))
            + cache_rows(rows_p, (nb, t_real), t_real) + cache_rows(rows_s, (db, ds), db * ds))
```

```python
import functools
import math

import jax
import jax.numpy as jnp
from jax import lax
from jax.experimental import pallas as pl
from jax.experimental.pallas import tpu as pltpu

N_META = 16
HEAD_DIM = 64
A_HEADS = 4
A_KV_HEADS = 2
B_HEADS = 8
B_KV_HEADS = 2
IDX_HEADS = 8
IDX_DIM = 64
IDX_SCALE = (IDX_HEADS * IDX_DIM) ** -0.5
TOPK_MAX = 256
PAGE_SIZE = 128
LN_EPS = 1e-5

Q_TILE = 128
DIFF_Q_TILE = 256
KV_CHUNK = 512
DSA_CHUNK = 512
PAGES_PER_STEP = 32
VMEM_LIMIT_BYTES = 56 * 1024 * 1024

LOG2E = math.log2(math.e)
INT_MIN = -(2 ** 31)
NEG_INF = float("-inf")

C_AQ, C_BQ, C_IQ, C_AK, C_AV, C_BK, C_BV, C_IK = 0, 512, 1024, 1536, 1792, 2048, 2176, 2304
W16 = 2432
C_AZ, C_BZ, C_IW = 0, 512, 1024
W32 = 1152
NCOL = W16 + W32

F32 = jnp.float32
BF16 = jnp.bfloat16
I32 = jnp.int32


def _dot_nt(a, b):
    return lax.dot_general(a, b, (((1,), (1,)), ((), ())), preferred_element_type=F32)


def _cparams(sem):
    return pltpu.CompilerParams(dimension_semantics=sem, vmem_limit_bytes=VMEM_LIMIT_BYTES)


def _sort_key(score):
    bits = lax.bitcast_convert_type(score, I32)
    return bits ^ ((bits >> 31) & 0x7FFFFFFF)


def _lane_lo(width=128):
    return lax.broadcasted_iota(I32, (1, width), 1) < 64


def _ln_kernel(x_ref, g_ref, b_ref, o_ref):
    x = x_ref[...]
    mu = jnp.mean(x, axis=-1, keepdims=True)
    xc = x - mu
    var = jnp.mean(xc * xc, axis=-1, keepdims=True)
    o_ref[...] = xc * lax.rsqrt(var + LN_EPS) * g_ref[...] + b_ref[...]


def _layernorm_rows(x, g, b, tm):
    rows, d = x.shape
    return pl.pallas_call(
        _ln_kernel,
        grid=(rows // tm,),
        in_specs=[pl.BlockSpec((tm, d), lambda i: (i, 0)),
                  pl.BlockSpec((1, d), lambda i: (0, 0)),
                  pl.BlockSpec((1, d), lambda i: (0, 0))],
        out_specs=pl.BlockSpec((tm, d), lambda i: (i, 0)),
        out_shape=jax.ShapeDtypeStruct((rows, d), F32),
        compiler_params=_cparams(("arbitrary",)),
        name="ln_in",
    )(x, g.reshape(1, d), b.reshape(1, d))


def _ln_prompt_kernel(meta_ref, prev_ref, cur_ref, g_ref, b_ref, o_ref, *, n_src_tiles):
    i = pl.program_id(1)
    top = jnp.where(i == 0, meta_ref[...], jnp.where(i <= n_src_tiles, prev_ref[...], 0.0))
    body = jnp.where(i < n_src_tiles, cur_ref[...][0:cur_ref.shape[0] - N_META], 0.0)
    x = jnp.concatenate([top, body], axis=0)
    mu = jnp.mean(x, axis=-1, keepdims=True)
    xc = x - mu
    var = jnp.mean(xc * xc, axis=-1, keepdims=True)
    o_ref[...] = xc * lax.rsqrt(var + LN_EPS) * g_ref[...] + b_ref[...]


def _layernorm_prompt(x_prompt, meta, g, b, tpad, tm):
    nb, seq, d = x_prompt.shape
    n_src = seq // tm
    per = tm // N_META
    return pl.pallas_call(
        functools.partial(_ln_prompt_kernel, n_src_tiles=n_src),
        grid=(nb, tpad // tm),
        in_specs=[pl.BlockSpec((N_META, d), lambda bi, i: (0, 0)),
                  pl.BlockSpec((None, N_META, d),
                               lambda bi, i: (bi, jnp.clip(i * per - 1, 0, seq // N_META - 1), 0)),
                  pl.BlockSpec((None, tm, d), lambda bi, i: (bi, jnp.minimum(i, n_src - 1), 0)),
                  pl.BlockSpec((1, d), lambda bi, i: (0, 0)),
                  pl.BlockSpec((1, d), lambda bi, i: (0, 0))],
        out_specs=pl.BlockSpec((None, tm, d), lambda bi, i: (bi, i, 0)),
        out_shape=jax.ShapeDtypeStruct((nb, tpad, d), F32),
        compiler_params=_cparams(("arbitrary", "arbitrary")),
        name="ln_in_prompt",
    )(meta, x_prompt, x_prompt, g.reshape(1, d), b.reshape(1, d))


def _proj_kernel(x_ref, w_ref, *rest, last_out_tile):
    p16_ref, p32_ref, ak_ref, av_ref, bk_ref, bv_ref, ik_ref = rest[-7:]
    xb = x_ref[...].astype(BF16)
    res = jnp.dot(xb, w_ref[...], preferred_element_type=F32)
    p16_ref[...] = res[:, :W16].astype(p16_ref.dtype)
    p32_ref[...] = res[:, W16:]

    @pl.when(pl.program_id(1) <= last_out_tile)
    def _():
        tm = res.shape[0]
        ak_ref[...] = res[:, C_AK:C_AK + 256].T
        for n in range(A_KV_HEADS):
            av_ref[pl.ds(n, tm, stride=A_KV_HEADS), :] = res[:, C_AV + 128 * n:C_AV + 128 * (n + 1)]
        bk_ref[...] = res[:, C_BK:C_BK + 128].T
        bv_ref[...] = res[:, C_BV:C_BV + 128].T
        ik_ref[...] = res[:, C_IK:C_IK + 128].T[0:IDX_DIM]


def _project(x, w, layer, depth, stacks, t_out, tm, p16_dtype, name):
    nb, t, d = x.shape
    last = pl.cdiv(t_out, tm) - 1
    row = lambda wd: pl.BlockSpec((None, tm, wd), lambda b, i: (b, i, 0))
    out = lambda ft: pl.BlockSpec((None, None, ft, tm), lambda b, i: (layer, b, 0, jnp.minimum(i, last)))
    shp = lambda ft: jax.ShapeDtypeStruct((depth, nb, ft, t_out), F32)
    out_v = pl.BlockSpec((None, None, A_KV_HEADS * tm, 128), lambda b, i: (layer, b, jnp.minimum(i, last), 0))
    shp_v = jax.ShapeDtypeStruct((depth, nb, A_KV_HEADS * t_out, 128), F32)
    stacks = list(stacks or ())
    return pl.pallas_call(
        functools.partial(_proj_kernel, last_out_tile=last),
        grid=(nb, t // tm),
        in_specs=[row(d), pl.BlockSpec((None, d, NCOL), lambda b, i: (layer, 0, 0))]
                 + [pl.BlockSpec(memory_space=pl.ANY)] * len(stacks),
        out_specs=[row(W16), row(W32), out(256), out_v, out(128), out(128), out(IDX_DIM)],
        out_shape=[jax.ShapeDtypeStruct((nb, t, W16), p16_dtype),
                   jax.ShapeDtypeStruct((nb, t, W32), F32),
                   shp(256), shp_v, shp(128), shp(128), shp(IDX_DIM)],
        input_output_aliases={2 + j: 2 + j for j in range(len(stacks))},
        compiler_params=_cparams(("arbitrary", "arbitrary")),
        name=name,
    )(x, w, *stacks)


def _diff_lambda(lam_ref):
    r = lam_ref[...]
    e1 = jnp.exp(jnp.sum(r[0:1] * r[1:2], axis=1, keepdims=True))
    e2 = jnp.exp(jnp.sum(r[2:3] * r[3:4], axis=1, keepdims=True))
    return e1 - e2 + r[4:5, 0:1]


def _half_select(x, keep_low):
    lo = _lane_lo()
    return jnp.where(lo, x, 0.0) if keep_low else jnp.where(lo, 0.0, x)


def _diff_query_rows(q, n_rows):
    pieces = [q[:, 0:128], q[:, 128:256]]
    return jnp.concatenate([_half_select(p, True) for p in pieces] +
                           [_half_select(p, False) for p in pieces], axis=0).astype(BF16)


def _dsa_query_rows(q):
    out = []
    for h in range(B_HEADS):
        n = h // (B_HEADS // B_KV_HEADS)
        p = q[:, 128 * (h // 2):128 * (h // 2) + 128]
        if h % 2 != n:
            p = pltpu.roll(p, 64, 1)
        out.append(_half_select(p, n == 0))
    return jnp.concatenate(out, axis=0).astype(BF16)


def _dsa_merge_heads(o, r):
    outs = []
    lo = _lane_lo()
    for j in range(B_HEADS // 2):
        n = j // 2
        oe, oo = o[2 * j * r:(2 * j + 1) * r], o[(2 * j + 1) * r:(2 * j + 2) * r]
        if n == 0:
            outs.append(jnp.where(lo, oe, pltpu.roll(oo, 64, 1)))
        else:
            outs.append(jnp.where(lo, pltpu.roll(oe, 64, 1), oo))
    return jnp.concatenate(outs, axis=1)


def _const_rows(values, r):
    return jnp.concatenate([jnp.full((r, 1), v, F32) for v in values], axis=0)


def _count(mask):
    return jnp.sum(jnp.where(mask, 1.0, 0.0), axis=1, keepdims=True)


def _lane_fold(x, op):
    acc = x[:, 0:128]
    for c in range(1, x.shape[1] // 128):
        acc = op(acc, x[:, 128 * c:128 * (c + 1)])
    return acc


def _for_chunks_paired(n, body):
    def pair(jj, _):
        body(2 * jj, 0)
        body(2 * jj + 1, 1)
        return 0

    lax.fori_loop(0, n // 2, pair, 0)

    @pl.when(n % 2 == 1)
    def _():
        body(n - 1, 0)


def _store_scores(j, qk, shifts, tq, s_ref, m_ref):
    for r, shift in enumerate(shifts):
        rows = slice(r * tq, (r + 1) * tq)
        s = qk[rows] * LOG2E + shift
        s_ref[j, rows, :] = s
        m_ref[rows, :] = jnp.maximum(m_ref[rows, :], _lane_fold(s, jnp.maximum))


def _softmax_weigh(n_chunks, tc, rows, s_ref, v_ref, m_ref, acc_ref, lacc_ref):
    m = jnp.broadcast_to(jnp.max(m_ref[...], axis=1, keepdims=True), (rows, 128))
    acc_ref[...] = jnp.zeros_like(acc_ref)
    lacc_ref[...] = jnp.zeros_like(lacc_ref)

    def weigh(j):
        s = s_ref[j]
        p = jnp.concatenate([jnp.exp2(s[:, 128 * c:128 * (c + 1)] - m) for c in range(tc // 128)], axis=1)
        pv = jnp.dot(p.astype(BF16), v_ref[pl.ds(pl.multiple_of(j * tc, tc), tc), :], preferred_element_type=F32)
        return _lane_fold(p, jnp.add), pv

    def step(j, _slot):
        l, pv = weigh(j)
        lacc_ref[...] += l
        acc_ref[...] += pv

    _for_chunks_paired(n_chunks, step)
    return acc_ref[...] / jnp.sum(lacc_ref[...], axis=1, keepdims=True)


def _diff_prompt_kernel(lam_ref, q_ref, k_ref, v_ref, o_ref, s_ref, m_ref, acc_ref, lacc_ref, *, n_real_tiles):
    n = pl.program_id(1)
    i = pl.program_id(2)
    tq, tc = DIFF_Q_TILE, KV_CHUNK

    @pl.when(i >= n_real_tiles)
    def _():
        o_ref[...] = jnp.zeros_like(o_ref)

    @pl.when(i < n_real_tiles)
    def _():
        lam = _diff_lambda(lam_ref)
        qst = _diff_query_rows(q_ref[...].astype(F32), tq)
        slope_g = (jnp.where(n == 0, 2.0 ** -2, 2.0 ** -6), jnp.where(n == 0, 2.0 ** -4, 2.0 ** -8))
        qpos = i * tq + lax.broadcasted_iota(I32, (tq, 1), 0)
        col = lax.broadcasted_iota(I32, (1, tc), 1)
        m_ref[...] = jnp.full_like(m_ref, NEG_INF)

        def score_step(j, masked):
            k0 = pl.multiple_of(j * tc, tc)
            qk = _dot_nt(qst, k_ref[pl.ds(k0, tc), :])
            kpos = (col + k0).astype(F32)
            shifts = [(slope_g[r % 2] * LOG2E) * kpos for r in range(4)]
            if masked:
                causal = jnp.where(col + k0 <= qpos, 0.0, NEG_INF)
                shifts = [sh + causal for sh in shifts[:2]] * 2
            _store_scores(j, qk, shifts, tq, s_ref, m_ref)

        n_full = (i * tq) // tc
        _for_chunks_paired(n_full, lambda j, _slot: score_step(j, False))
        score_step(n_full, True)
        o = _softmax_weigh(n_full + 1, tc, 4 * tq, s_ref, v_ref, m_ref, acc_ref, lacc_ref)
        out_g0 = o[0:tq] - lam * o[2 * tq:3 * tq]
        out_g1 = o[tq:2 * tq] - lam * o[3 * tq:4 * tq]
        o_ref[...] = jnp.concatenate([out_g0, out_g1], axis=1)


def _diff_prompt(p16, lam, t_real):
    nb, tpad, _ = p16.shape
    tq = DIFF_Q_TILE
    kern = functools.partial(_diff_prompt_kernel, n_real_tiles=pl.cdiv(t_real, tq))
    return pl.pallas_call(
        kern,
        grid=(nb, A_KV_HEADS, tpad // tq),
        in_specs=[pl.BlockSpec((8, 128), lambda b, n, i: (0, 0)),
                  pl.BlockSpec((None, tq, 256), lambda b, n, i: (b, i, C_AQ // 256 + n)),
                  pl.BlockSpec((None, tpad, 128), lambda b, n, i: (b, 0, C_AK // 128 + n)),
                  pl.BlockSpec((None, tpad, 128), lambda b, n, i: (b, 0, C_AV // 128 + n))],
        out_specs=pl.BlockSpec((None, tq, 256), lambda b, n, i: (b, i, n)),
        out_shape=jax.ShapeDtypeStruct((nb, tpad, 512), F32),
        scratch_shapes=[pltpu.VMEM((tpad // KV_CHUNK, 4 * tq, KV_CHUNK), F32), pltpu.VMEM((4 * tq, 128), F32),
                        pltpu.VMEM((4 * tq, 128), F32), pltpu.VMEM((4 * tq, 128), F32)],
        compiler_params=_cparams(("arbitrary", "arbitrary", "arbitrary")),
        name="diff_prompt",
    )(lam, p16, p16, p16)


def _dsa_prompt_kernel(bq_ref, iq_ref, iw_ref, bk_ref, bv_ref, ik_ref, o_ref,
                       keys_ref, keys_t_ref, mask_ref, s_ref, m_ref, acc_ref, lacc_ref, *,
                       n_real_tiles, topk, idx_bits):
    i = pl.program_id(1)
    tq, tc = Q_TILE, DSA_CHUNK

    @pl.when(i >= n_real_tiles)
    def _():
        o_ref[...] = jnp.zeros_like(o_ref)

    @pl.when(i < n_real_tiles)
    def _():
        n_chunks = (i * tq) // tc + 1
        col = lax.broadcasted_iota(I32, (1, tc), 1)
        qpos = i * tq + lax.broadcasted_iota(I32, (tq, 1), 0)

        iq = iq_ref[...].astype(F32)
        iqst = jnp.concatenate(
            [_half_select(iq[:, 128 * (h // 2):128 * (h // 2) + 128], h % 2 == 0)
             for h in range(IDX_HEADS)], axis=0).astype(BF16)
        w = iw_ref[...]
        wcol = [jnp.broadcast_to(w[:, h:h + 1], (tq, tc)) for h in range(IDX_HEADS)]

        def index_step(j, _slot):
            k0 = pl.multiple_of(j * tc, tc)
            logits = _dot_nt(iqst, ik_ref[pl.ds(k0, tc), :])
            sc = jnp.maximum(logits[0:tq], 0.0) * wcol[0]
            for h in range(1, IDX_HEADS):
                sc = sc + jnp.maximum(logits[h * tq:(h + 1) * tq], 0.0) * wcol[h]
            key = jnp.where(col + k0 <= qpos, _sort_key(sc * IDX_SCALE), INT_MIN)
            keys_ref[j] = key
            keys_t_ref[j] = key.T

        _for_chunks_paired(n_chunks, index_step)

        kidx = lax.broadcasted_iota(I32, (tc, 1), 0)

        def count_over_chunks(pred):
            def body(j, acc):
                hit = jnp.where(pred(keys_t_ref[j], kidx + j * tc), 1.0, 0.0)
                parts = [hit[8 * r:8 * (r + 1)] for r in range(tc // 8)]
                while len(parts) > 1:
                    parts = [parts[r] + parts[r + 1] for r in range(0, len(parts), 2)]
                return acc + parts[0]
            acc = lax.fori_loop(0, n_chunks, body, jnp.zeros((8, tq), F32))
            return jnp.sum(acc, axis=0, keepdims=True)

        def bit_step(b, carry):
            t, n_ge = carry
            cand = t + lax.shift_left(jnp.int32(1), 31 - b)
            cnt = count_over_chunks(lambda kc, _: kc >= cand)
            return jnp.where(cnt >= topk, cand, t), jnp.where(cnt >= topk, cnt, n_ge)

        t, n_ge = lax.fori_loop(0, 32, bit_step, (jnp.full((1, tq), INT_MIN, I32), jnp.zeros((1, tq), F32)))
        thr_t = jnp.maximum(t, INT_MIN + 1)
        n_gt = count_over_chunks(lambda kc, _: kc > thr_t)
        need = topk - n_gt
        n_eq = jnp.where(t == INT_MIN, 0.0, n_ge - n_gt)
        excess = n_eq > need

        def tie_bound():
            def tie_step(b, jb):
                cand = jb + lax.shift_left(jnp.int32(1), idx_bits - 1 - b)
                cnt = count_over_chunks(lambda kc, idx: (kc == thr_t) & (idx < cand))
                return jnp.where(cnt < need, cand, jb)
            return lax.fori_loop(0, idx_bits, tie_step, jnp.zeros((1, tq), I32))

        any_excess = jnp.max(jnp.where(excess, 1.0, 0.0)) > 0.0
        big = jnp.full((1, tq), 2 ** 30, I32)
        bound_t = lax.cond(any_excess, lambda: jnp.where(excess, tie_bound(), big), lambda: big)
        to_col = lambda v: jnp.broadcast_to(v, (tq, tq)).T[:, 0:1]
        thr, bound = to_col(thr_t), to_col(bound_t)

        qst = _dsa_query_rows(bq_ref[...].astype(F32))

        m_ref[...] = jnp.full_like(m_ref, NEG_INF)

        def score_step(j, slot):
            k0 = pl.multiple_of(j * tc, tc)
            kc = keys_ref[j]
            sel = (kc > thr) | ((kc == thr) & (col + k0 <= bound))
            mask_ref[slot] = jnp.where(sel, 0.0, NEG_INF)
            qk = _dot_nt(qst, bk_ref[pl.ds(k0, tc), :])
            kpos = (col + k0).astype(F32)
            _store_scores(j, qk, [(2.0 ** -(h + 1) * LOG2E) * kpos + mask_ref[slot] for h in range(B_HEADS)],
                          tq, s_ref, m_ref)

        _for_chunks_paired(n_chunks, score_step)
        o = _softmax_weigh(n_chunks, tc, B_HEADS * tq, s_ref, bv_ref, m_ref, acc_ref, lacc_ref)
        o_ref[...] = _dsa_merge_heads(o, tq)


def _dsa_prompt(p16, p32, t_real, topk):
    nb, tpad, _ = p16.shape
    nq = tpad // Q_TILE
    kern = functools.partial(_dsa_prompt_kernel, n_real_tiles=pl.cdiv(t_real, Q_TILE), topk=float(topk),
                             idx_bits=max(1, int(tpad).bit_length()))
    whole = lambda c: pl.BlockSpec((None, tpad, 128), lambda b, i: (b, 0, c // 128))
    return pl.pallas_call(
        kern,
        grid=(nb, nq),
        in_specs=[pl.BlockSpec((None, Q_TILE, 512), lambda b, i: (b, i, C_BQ // 512)),
                  pl.BlockSpec((None, Q_TILE, 512), lambda b, i: (b, i, C_IQ // 512)),
                  pl.BlockSpec((None, Q_TILE, 128), lambda b, i: (b, i, C_IW // 128)),
                  whole(C_BK), whole(C_BV), whole(C_IK)],
        out_specs=pl.BlockSpec((None, Q_TILE, 512), lambda b, i: (b, i, 0)),
        out_shape=jax.ShapeDtypeStruct((nb, tpad, 512), F32),
        scratch_shapes=[pltpu.VMEM((tpad // DSA_CHUNK, Q_TILE, DSA_CHUNK), I32),
                        pltpu.VMEM((tpad // DSA_CHUNK, DSA_CHUNK, Q_TILE), I32),
                        pltpu.VMEM((2, Q_TILE, DSA_CHUNK), F32),
                        pltpu.VMEM((tpad // DSA_CHUNK, B_HEADS * Q_TILE, DSA_CHUNK), F32),
                        pltpu.VMEM((B_HEADS * Q_TILE, 128), F32), pltpu.VMEM((B_HEADS * Q_TILE, 128), F32),
                        pltpu.VMEM((B_HEADS * Q_TILE, 128), F32)],
        compiler_params=_cparams(("arbitrary", "arbitrary")),
        name="dsa_prompt",
    )(p16, p16, p32, p16, p16, p16)


def _merge_kernel(sc_ref, x_ref, ya_ref, yb_ref, az_ref, bz_ref, w_ref, sg_ref, g_ref, b_ref, o_ref, *, alpha):
    one_minus_lam_init = sc_ref[5:6, 0:1]
    ya = ya_ref[...]
    parts = []
    for h in range(A_HEADS):
        o = ya[:, 128 * h:128 * h + 128]
        y = o * lax.rsqrt(jnp.mean(o * o, axis=-1, keepdims=True) + LN_EPS) * sg_ref[...]
        parts.append(y * one_minus_lam_init)
    az = az_ref[...]
    bz = bz_ref[...]
    ya_g = jnp.concatenate(parts, axis=1) * (az * (1.0 / (1.0 + jnp.exp(-az))))
    yb_g = yb_ref[...] * (bz * (1.0 / (1.0 + jnp.exp(-bz))))
    mix = jnp.concatenate([ya_g, yb_g], axis=1).astype(BF16)
    y = alpha * x_ref[...] + jnp.dot(mix, w_ref[...], preferred_element_type=F32)
    mu = jnp.mean(y, axis=-1, keepdims=True)
    yc = y - mu
    var = jnp.mean(yc * yc, axis=-1, keepdims=True)
    o_ref[...] = yc * lax.rsqrt(var + LN_EPS) * g_ref[...] + b_ref[...]


def _merge(x, ya, yb, p32, w_out, layer, sc, sub_g, ln_g, ln_b, alpha, tm, name):
    nb, t, d = x.shape
    row = lambda wd, c: pl.BlockSpec((None, tm, wd), lambda b, i: (b, i, c))
    vec = lambda wd: pl.BlockSpec((1, wd), lambda b, i: (0, 0))
    return pl.pallas_call(
        functools.partial(_merge_kernel, alpha=alpha),
        grid=(nb, t // tm),
        in_specs=[pl.BlockSpec((8, 128), lambda b, i: (0, 0)),
                  row(d, 0), row(512, 0), row(512, 0), row(512, C_AZ // 512), row(512, C_BZ // 512),
                  pl.BlockSpec((None, d, d), lambda b, i: (layer, 0, 0)),
                  vec(128), vec(d), vec(d)],
        out_specs=row(d, 0),
        out_shape=jax.ShapeDtypeStruct((nb, t, d), F32),
        compiler_params=_cparams(("arbitrary", "arbitrary")),
        name=name,
    )(sc, x, ya, yb, p32, p32, w_out, sub_g.reshape(1, 128), ln_g.reshape(1, d), ln_b.reshape(1, d))


def _pad_rows(x, rows):
    return jnp.concatenate([x, jnp.zeros((rows - x.shape[0], x.shape[1]), x.dtype)], axis=0)


def _diff_sample_kernel(pt_ref, lam_ref, q_ref, kn_ref, vn_ref, *rest, n_pg, past):
    k_refs, v_refs = rest[:n_pg], rest[n_pg:2 * n_pg]
    o_ref, m_ref, l_ref, acc_ref = rest[2 * n_pg:]
    pg = pl.program_id(1)
    ds = q_ref.shape[0]
    rows = 2 * A_KV_HEADS * 2 * ds
    span = n_pg * PAGE_SIZE

    @pl.when(pg == 0)
    def _():
        m_ref[...] = jnp.full_like(m_ref, NEG_INF)
        l_ref[...] = jnp.zeros_like(l_ref)
        acc_ref[...] = jnp.zeros_like(acc_ref)

    q = q_ref[...]
    qst = jnp.concatenate([_diff_query_rows(q[:, 256 * n:256 * n + 256], ds) for n in range(A_KV_HEADS)], axis=0)
    slope = _const_rows([2.0 ** (-2 * (2 * n + g + 1)) for n in range(A_KV_HEADS) for _c in range(2)
                         for g in range(2)], ds)
    half = rows // A_KV_HEADS

    def attend(score, weigh, width, k0, mask):
        col = lax.broadcasted_iota(I32, (1, width), 1)
        s = jnp.concatenate([score(n, qst[half * n:half * (n + 1)]) for n in range(A_KV_HEADS)],
                            axis=0) + slope * col.astype(F32)
        if mask is not None:
            s = jnp.where(mask(col), s, NEG_INF)
        cj = slope * k0
        m, l, acc = m_ref[...][:, 0:1], l_ref[...][:, 0:1], acc_ref[...]
        mt = jnp.max(s, axis=1, keepdims=True)
        m_new = jnp.maximum(m, mt + cj)
        p = jnp.exp(s - (m_new - cj))
        alpha = jnp.exp(m - m_new)
        l = alpha * l + jnp.sum(p, axis=1, keepdims=True)
        pv = jnp.concatenate([weigh(n, p[half * n:half * (n + 1)].astype(BF16)) for n in range(A_KV_HEADS)],
                             axis=0)
        m_ref[...] = jnp.broadcast_to(m_new, m_ref.shape)
        l_ref[...] = jnp.broadcast_to(l, l_ref.shape)
        acc_ref[...] = alpha * acc + pv

    def past_score(n, q):
        kt = jnp.concatenate([r[128 * n:128 * n + 128, :] for r in k_refs], axis=1).astype(BF16)
        return jnp.dot(q, kt, preferred_element_type=F32)

    def past_weigh(n, p):
        v = jnp.concatenate([r[pl.ds(n, PAGE_SIZE, stride=A_KV_HEADS), :] for r in v_refs], axis=0).astype(BF16)
        return jnp.dot(p, v, preferred_element_type=F32)

    attend(past_score, past_weigh, span, (pg * span).astype(F32), None)

    @pl.when(pg == past // span - 1)
    def _():
        r = lax.broadcasted_iota(I32, (rows, 1), 0) & (ds - 1)
        kn = _pad_rows(kn_ref[...], 128).astype(BF16)
        vn = _pad_rows(vn_ref[...], 128).astype(BF16)
        attend(lambda n, q: _dot_nt(q, kn[:, 128 * n:128 * n + 128]),
               lambda n, p: jnp.dot(p, vn[:, 128 * n:128 * n + 128], preferred_element_type=F32),
               128, jnp.float32(past), lambda col: col <= r)
        lam = _diff_lambda(lam_ref)
        o = acc_ref[...] / l_ref[...][:, 0:1]
        outs = []
        for n in range(A_KV_HEADS):
            base = half * n
            for g in range(2):
                outs.append(o[base + g * ds:base + (g + 1) * ds]
                            - lam * o[base + (2 + g) * ds:base + (3 + g) * ds])
        o_ref[...] = jnp.concatenate(outs, axis=1)


def _diff_sample(p16s, lam, cache_k, cache_v, pt_flat, n_pages):
    db, ds, _ = p16s.shape
    n_pg = PAGES_PER_STEP
    rows = 2 * A_KV_HEADS * 2 * ds
    past = n_pages * PAGE_SIZE
    kern = functools.partial(_diff_sample_kernel, n_pg=n_pg, past=past)

    def page(r):
        return pl.BlockSpec((None, 256, PAGE_SIZE), lambda sb, pg, pt: (pt[sb * n_pages + pg * n_pg + r], 0, 0))

    new = lambda c: pl.BlockSpec((None, ds, 256), lambda sb, pg, pt: (sb, 0, c // 256))
    grid_spec = pltpu.PrefetchScalarGridSpec(
        num_scalar_prefetch=1,
        grid=(db, n_pages // n_pg),
        in_specs=[pl.BlockSpec((8, 128), lambda sb, pg, pt: (0, 0)),
                  pl.BlockSpec((None, ds, 512), lambda sb, pg, pt: (sb, 0, C_AQ // 512)),
                  new(C_AK), new(C_AV)] + [page(r) for r in range(n_pg)] * 2,
        out_specs=pl.BlockSpec((None, ds, 512), lambda sb, pg, pt: (sb, 0, 0)),
        scratch_shapes=[pltpu.VMEM((rows, 128), F32), pltpu.VMEM((rows, 128), F32),
                        pltpu.VMEM((rows, 128), F32)])
    return pl.pallas_call(
        kern, grid_spec=grid_spec,
        out_shape=jax.ShapeDtypeStruct((db, ds, 512), F32),
        compiler_params=_cparams(("arbitrary", "arbitrary")),
        name="diff_sample",
    )(pt_flat, lam, p16s, p16s, p16s, *([cache_k] * n_pg), *([cache_v] * n_pg))


def _dsa_sample_index_kernel(pt_ref, iq_ref, iw_ref, in_ref, *rest, n_pg, past, topk, idx_bits):
    i_refs = rest[:n_pg]
    keys_ref, keys_new_ref, thr_ref, bound_ref = rest[n_pg:]
    sb, g = pl.program_id(0), pl.program_id(1)
    ds = iq_ref.shape[0]
    span = n_pg * PAGE_SIZE
    n_groups = past // span
    cw = keys_ref.shape[2]
    n_sub = span // cw
    all_rows = keys_ref.shape[1]
    row0 = pl.multiple_of(sb * ds, ds)
    r_id = lax.broadcasted_iota(I32, (ds, 1), 0)

    def index_keys(ik, transposed):
        iq = iq_ref[...]
        heads = []
        for h in range(IDX_HEADS):
            p = iq[:, 128 * (h // 2):128 * (h // 2) + 128]
            if h % 2:
                p = pltpu.roll(p, 64, 1)
            heads.append(p[:, 0:64])
        iqst = jnp.concatenate(heads, axis=0).astype(BF16)
        if transposed:
            logits = jnp.dot(iqst, ik.astype(BF16), preferred_element_type=F32)
        else:
            logits = _dot_nt(iqst, ik.astype(BF16))
        w = iw_ref[...]
        sc = jnp.maximum(logits[0:ds], 0.0) * w[:, 0:1]
        for h in range(1, IDX_HEADS):
            sc = sc + jnp.maximum(logits[h * ds:(h + 1) * ds], 0.0) * w[:, h:h + 1]
        return _sort_key(sc * IDX_SCALE)

    ik = jnp.concatenate([r[...] for r in i_refs], axis=1)
    keys = index_keys(ik, True)
    for c in range(n_sub):
        keys_ref[g * n_sub + c, pl.ds(row0, ds), :] = keys[:, c * cw:(c + 1) * cw]

    @pl.when(g == n_groups - 1)
    def _():
        col = lax.broadcasted_iota(I32, (1, 128), 1)
        key_new = index_keys(_pad_rows(in_ref[...][:, 0:64], 128), False)
        keys_new_ref[pl.ds(row0, ds), :] = jnp.where(col <= r_id, key_new, INT_MIN)

    @pl.when((sb == pl.num_programs(0) - 1) & (g == n_groups - 1))
    def _():
        col = lax.broadcasted_iota(I32, (1, cw), 1)
        col_new = lax.broadcasted_iota(I32, (1, 128), 1) + past

        def count(pred):
            def body(j, acc):
                return acc + _lane_fold(jnp.where(pred(keys_ref[j], col + j * cw), 1.0, 0.0), jnp.add)
            acc = lax.fori_loop(0, keys_ref.shape[0], body, jnp.zeros((all_rows, 128), F32))
            acc = acc + jnp.where(pred(keys_new_ref[...], col_new), 1.0, 0.0)
            return jnp.sum(acc, axis=1, keepdims=True)

        def bit_step(b, t):
            cand = t + lax.shift_left(jnp.int32(1), 31 - b)
            return jnp.where(count(lambda kc, _: kc >= cand) >= topk, cand, t)

        t = lax.fori_loop(0, 32, bit_step, jnp.full((all_rows, 1), INT_MIN, I32))
        thr = jnp.maximum(t, INT_MIN + 1)
        need = topk - count(lambda kc, _: kc > thr)
        excess = count(lambda kc, _: kc == thr) > need

        def tie_bound():
            def tie_step(b, jb):
                cand = jb + lax.shift_left(jnp.int32(1), idx_bits - 1 - b)
                return jnp.where(count(lambda kc, ix: (kc == thr) & (ix < cand)) < need, cand, jb)
            return lax.fori_loop(0, idx_bits, tie_step, jnp.zeros((all_rows, 1), I32))

        big = jnp.full((all_rows, 1), 2 ** 30, I32)
        any_excess = jnp.max(jnp.where(excess, 1.0, 0.0)) > 0.0
        bound = lax.cond(any_excess, lambda: jnp.where(excess, tie_bound(), big), lambda: big)
        thr_ref[...] = jnp.broadcast_to(thr, thr_ref.shape)
        bound_ref[...] = jnp.broadcast_to(bound, bound_ref.shape)


def _dsa_sample_attend_kernel(pt_ref, bq_ref, kn_ref, vn_ref, keys_ref, keys_new_ref, thr_ref, bound_ref, *rest,
                              n_pg, past):
    k_refs, v_refs = rest[:n_pg], rest[n_pg:2 * n_pg]
    o_ref, m_ref, l_ref, acc_ref = rest[2 * n_pg:]
    g = pl.program_id(1)
    ds = bq_ref.shape[0]
    rows = B_HEADS * ds
    span = n_pg * PAGE_SIZE
    n_groups = past // span
    qst = _dsa_query_rows(bq_ref[...])
    slope = _const_rows([2.0 ** -(h + 1) for h in range(B_HEADS)], ds)

    def attend(qk, weigh, kc, k0):
        width = qk.shape[1]
        col = lax.broadcasted_iota(I32, (1, width), 1)
        thr, bound = thr_ref[...][:, 0:1], bound_ref[...][:, 0:1]
        sel = (kc > thr) | ((kc == thr) & (col + k0 <= bound))
        s = (qk + slope * col.astype(F32)).reshape(B_HEADS, ds, width)
        s = jnp.where(sel[None], s, NEG_INF).reshape(rows, width)
        cj = slope * k0.astype(F32)
        m, l, acc = m_ref[...][:, 0:1], l_ref[...][:, 0:1], acc_ref[...]
        mt = jnp.max(s, axis=1, keepdims=True)
        m_new = jnp.maximum(m, mt + cj)
        m_safe = jnp.where(m_new == NEG_INF, 0.0, m_new)
        p = jnp.exp(s - (m_safe - cj))
        alpha = jnp.exp(m - m_safe)
        m_ref[...] = jnp.broadcast_to(m_new, m_ref.shape)
        l_ref[...] = jnp.broadcast_to(alpha * l + jnp.sum(p, axis=1, keepdims=True), l_ref.shape)
        acc_ref[...] = alpha * acc + weigh(p.astype(BF16))

    @pl.when(g == 0)
    def _():
        m_ref[...] = jnp.full_like(m_ref, NEG_INF)
        l_ref[...] = jnp.zeros_like(l_ref)
        acc_ref[...] = jnp.zeros_like(acc_ref)

    kt = jnp.concatenate([r[...] for r in k_refs], axis=1).astype(BF16)
    vt = jnp.concatenate([r[...] for r in v_refs], axis=1).astype(BF16)
    kc = jnp.concatenate([keys_ref[c] for c in range(keys_ref.shape[0])], axis=1)
    attend(jnp.dot(qst, kt, preferred_element_type=F32), lambda p: _dot_nt(p, vt), kc, g * span)

    @pl.when(g == n_groups - 1)
    def _():
        kn = _pad_rows(kn_ref[...], 128).astype(BF16)
        vn = _pad_rows(vn_ref[...], 128).astype(BF16)
        attend(_dot_nt(qst, kn), lambda p: jnp.dot(p, vn, preferred_element_type=F32),
               keys_new_ref[...], jnp.int32(past))
        o_ref[...] = _dsa_merge_heads(acc_ref[...] / l_ref[...][:, 0:1], ds)


def _dsa_sample(p16s, p32s, cache_i, cache_k, cache_v, pt_flat, n_pages, topk):
    db, ds, _ = p16s.shape
    n_pg = PAGES_PER_STEP
    n_groups = n_pages // n_pg
    past = n_pages * PAGE_SIZE
    rows = B_HEADS * ds
    cw = min(DSA_CHUNK, n_pg * PAGE_SIZE)
    n_sub = n_pg * PAGE_SIZE // cw

    def page(r, feat):
        return pl.BlockSpec((None, feat, PAGE_SIZE), lambda sb, g, pt: (pt[sb * n_pages + g * n_pg + r], 0, 0))

    new = lambda wd, c: pl.BlockSpec((None, ds, wd), lambda sb, g, pt: (sb, 0, c // wd))
    whole = lambda shape: pl.BlockSpec(shape, lambda sb, g, pt: (0,) * len(shape))

    key_shapes = [(past // cw, db * ds, cw), (db * ds, 128), (db * ds, 128), (db * ds, 128)]
    keys, keys_new, thr, bound = pl.pallas_call(
        functools.partial(_dsa_sample_index_kernel, n_pg=n_pg, past=past, topk=float(topk),
                          idx_bits=int(past + 128).bit_length()),
        grid_spec=pltpu.PrefetchScalarGridSpec(
            num_scalar_prefetch=1,
            grid=(db, n_groups),
            in_specs=[new(512, C_IQ), new(128, C_IW), new(128, C_IK)] + [page(r, 64) for r in range(n_pg)],
            out_specs=[whole(s) for s in key_shapes]),
        out_shape=[jax.ShapeDtypeStruct(s, I32) for s in key_shapes],
        compiler_params=_cparams(("arbitrary", "arbitrary")),
        name="dsa_sample_index",
    )(pt_flat, p16s, p32s, p16s, *([cache_i] * n_pg))

    per_seq = pl.BlockSpec((ds, 128), lambda sb, g, pt: (sb, 0))
    return pl.pallas_call(
        functools.partial(_dsa_sample_attend_kernel, n_pg=n_pg, past=past),
        grid_spec=pltpu.PrefetchScalarGridSpec(
            num_scalar_prefetch=1,
            grid=(db, n_groups),
            in_specs=[new(512, C_BQ), new(128, C_BK), new(128, C_BV),
                      pl.BlockSpec((n_sub, ds, cw), lambda sb, g, pt: (g, sb, 0)), per_seq, per_seq, per_seq]
                     + [page(r, 128) for r in range(n_pg)] * 2,
            out_specs=pl.BlockSpec((None, ds, 512), lambda sb, g, pt: (sb, 0, 0)),
            scratch_shapes=[pltpu.VMEM((rows, 128), F32), pltpu.VMEM((rows, 128), F32),
                            pltpu.VMEM((rows, 128), F32)]),
        out_shape=jax.ShapeDtypeStruct((db, ds, 512), F32),
        compiler_params=_cparams(("arbitrary", "arbitrary")),
        name="dsa_sample_attend",
    )(pt_flat, p16s, p16s, p16s, keys, keys_new, thr, bound, *([cache_k] * n_pg), *([cache_v] * n_pg))


_W_IN = dict(aq=(0, 512), ak=(512, 256), av=(768, 256), az=(1024, 512), bq=(1536, 512), bk=(2048, 128),
             bv=(2176, 128), iq=(2304, 512), ik=(2816, 64), iw=(2880, 8), bz=(2888, 512))
_W_ORDER = ((("aq",), HEAD_DIM ** -0.5), (("bq",), HEAD_DIM ** -0.5), (("iq",), 1.0), (("ak",), 1.0), (("av",), 1.0),
            (("bk",), 1.0), (("bv",), 1.0), (("ik", "ik"), 1.0), (("az",), 1.0), (("bz",), 1.0), (("iw",), 1.0))


def _regroup_kernel(wt_ref, o_ref):
    dst = 0
    for names, scale in _W_ORDER:
        rows = jnp.concatenate([wt_ref[_W_IN[n][0]:_W_IN[n][0] + _W_IN[n][1], :] for n in names], axis=0)
        width = -(-rows.shape[0] // 128) * 128
        if rows.shape[0] < width:
            rows = jnp.concatenate([rows, jnp.zeros((width - rows.shape[0], rows.shape[1]), F32)], axis=0)
        o_ref[:, dst:dst + width] = (rows * scale).T.astype(BF16)
        dst += width


def _regroup_weights(w_in):
    depth, d, n_in = w_in.shape
    return pl.pallas_call(
        _regroup_kernel,
        grid=(depth,),
        in_specs=[pl.BlockSpec((None, n_in, d), lambda l: (l, 0, 0))],
        out_specs=pl.BlockSpec((None, d, NCOL), lambda l: (l, 0, 0)),
        out_shape=jax.ShapeDtypeStruct((depth, d, NCOL), BF16),
        compiler_params=_cparams(("arbitrary",)),
        name="regroup_w_in",
    )(w_in.transpose(0, 2, 1))


def _pick_tile(n, candidates):
    for c in candidates:
        if n % c == 0:
            return c
    return n


def kernel(x_prompt, x_sample, cache_diff_k, cache_diff_v, cache_dsa_k, cache_dsa_v, cache_idx_k, page_table,
           meta_tokens, ln_in_g, ln_in_b, w_in, w_out, lambda_q1, lambda_k1, lambda_q2, lambda_k2,
           subln_g, ln_g, ln_b):
    nb, seq, d = x_prompt.shape
    db, ds, _ = x_sample.shape
    depth = w_in.shape[0]
    n_pool = cache_diff_k.shape[1]
    n_pages = page_table.shape[1]
    t_real = seq + N_META
    tpad = -(-t_real // DSA_CHUNK) * DSA_CHUNK
    assert ds == 8 and n_pages % PAGES_PER_STEP == 0 and d == 1024
    alpha = (2 * depth) ** 0.25
    topk_p = min(TOPK_MAX, seq // 4)
    topk_s = min(TOPK_MAX, (n_pages * PAGE_SIZE + ds) // 4)

    tm_ln = next((c for c in (512, 256, 128, 64, 32) if seq % c == 0 and tpad % c == 0), None)
    if tm_ln is not None:
        xp = _layernorm_prompt(x_prompt, meta_tokens.astype(x_prompt.dtype), ln_in_g, ln_in_b, tpad, tm_ln)
    else:
        meta = jnp.broadcast_to(meta_tokens[None].astype(x_prompt.dtype), (nb, N_META, d))
        xp = jnp.concatenate([meta, x_prompt, jnp.zeros((nb, tpad - t_real, d), x_prompt.dtype)], axis=1)
        xp = _layernorm_rows(xp.reshape(nb * tpad, d), ln_in_g, ln_in_b, KV_CHUNK).reshape(nb, tpad, d)
    xs = _layernorm_rows(x_sample.reshape(db * ds, d), ln_in_g, ln_in_b,
                         _pick_tile(db * ds, (256, 128, 8))).reshape(1, db * ds, d)

    w = _regroup_weights(w_in)
    w_o = w_out.astype(BF16)
    pad64 = lambda v: jnp.pad(v, ((0, 0), (0, 128 - v.shape[-1])))
    ck = cache_diff_k.transpose(0, 1, 3, 4, 5, 2).reshape(depth * n_pool, 256, PAGE_SIZE)
    cv = cache_diff_v.reshape(depth * n_pool, PAGE_SIZE * A_KV_HEADS, 128)
    cbk = cache_dsa_k.transpose(0, 1, 3, 4, 2).reshape(depth * n_pool, 128, PAGE_SIZE)
    cbv = cache_dsa_v.transpose(0, 1, 3, 4, 2).reshape(depth * n_pool, 128, PAGE_SIZE)
    ci = cache_idx_k.transpose(0, 1, 3, 2).reshape(depth * n_pool, IDX_DIM, PAGE_SIZE)
    pt_flat = page_table.reshape(-1).astype(I32)
    tm_p = _pick_tile(tpad, (512, 256, 128))
    tm_s = _pick_tile(db * ds, (256, 128, 8))

    rows_p = rows_s = None
    for layer in range(depth):
        lam_init = 0.8 - 0.6 * math.exp(-0.3 * layer)
        consts = jnp.stack([jnp.full((128,), lam_init, F32), jnp.full((128,), 1.0 - lam_init, F32),
                            jnp.zeros((128,), F32), jnp.zeros((128,), F32)])
        lam = jnp.concatenate([pad64(jnp.stack([lambda_q1[layer], lambda_k1[layer],
                                                 lambda_q2[layer], lambda_k2[layer]])), consts], axis=0)

        p16, p32, *rows_p = _project(xp, w, layer, depth, rows_p, t_real, tm_p, BF16, "proj_prompt")
        ya = _diff_prompt(p16, lam, t_real)
        yb = _dsa_prompt(p16, p32, t_real, topk_p)
        xp = _merge(xp, ya, yb, p32, w_o, layer, lam, subln_g[layer], ln_g[layer], ln_b[layer],
                    alpha, tm_p, "merge_prompt")

        p16s, p32s, *rows_s = _project(xs, w, layer, depth, rows_s, db * ds, tm_s, F32, "proj_sample")
        p16s3 = p16s.reshape(db, ds, W16)
        pt_layer = pt_flat + layer * n_pool
        ya_s = _diff_sample(p16s3, lam, ck, cv, pt_layer, n_pages)
        yb_s = _dsa_sample(p16s3, p32s.reshape(db, ds, W32), ci, cbk, cbv, pt_layer, n_pages, topk_s)
        xs = _merge(xs, ya_s.reshape(1, db * ds, 512), yb_s.reshape(1, db * ds, 512), p32s, w_o, layer, lam,
                    subln_g[layer], ln_g[layer], ln_b[layer], alpha, tm_s, "merge_sample")

    def cache_rows(stacks, lead, t):
        k, v, bk, bv, ik = stacks
        return (k.reshape(depth, -1, A_KV_HEADS, 2, HEAD_DIM, t).transpose(0, 1, 5, 2, 3, 4)
                 .reshape((depth,) + lead + (A_KV_HEADS, 2, HEAD_DIM)),
                v.reshape((depth,) + lead + (A_KV_HEADS, 2 * HEAD_DIM)),
                bk.reshape(depth, -1, B_KV_HEADS, HEAD_DIM, t).transpose(0, 1, 4, 2, 3)
                  .reshape((depth,) + lead + (B_KV_HEADS, HEAD_DIM)),
                bv.reshape(depth, -1, B_KV_HEADS, HEAD_DIM, t).transpose(0, 1, 4, 2, 3)
                  .reshape((depth,) + lead + (B_KV_HEADS, HEAD_DIM)),
                ik.transpose(0, 1, 3, 2).reshape((depth,) + lead + (IDX_DIM,)))

    return ((xp[:, N_META:t_real], xs.reshape(db, ds, d))
            + cache_rows(rows_p, (nb, t_real), t_real) + cache_rows(rows_s, (db, ds), db * ds))
```

```python
import functools
import math

import jax
import jax.numpy as jnp
from jax import lax
from jax.experimental import pallas as pl
from jax.experimental.pallas import tpu as pltpu

N_META = 16
HEAD_DIM = 64
A_HEADS = 4
A_KV_HEADS = 2
B_HEADS = 8
B_KV_HEADS = 2
IDX_HEADS = 8
IDX_DIM = 64
IDX_SCALE = (IDX_HEADS * IDX_DIM) ** -0.5
TOPK_MAX = 256
PAGE_SIZE = 128
LN_EPS = 1e-5

Q_TILE = 128
DIFF_Q_TILE = 256
KV_CHUNK = 512
DSA_CHUNK = 512
PAGES_PER_STEP = 32
VMEM_LIMIT_BYTES = 56 * 1024 * 1024

LOG2E = math.log2(math.e)
INT_MIN = -(2 ** 31)
NEG_INF = float("-inf")

C_AQ, C_BQ, C_IQ, C_AK, C_AV, C_BK, C_BV, C_IK = 0, 512, 1024, 1536, 1792, 2048, 2176, 2304
W16 = 2432
C_AZ, C_BZ, C_IW = 0, 512, 1024
W32 = 1152
NCOL = W16 + W32

F32 = jnp.float32
BF16 = jnp.bfloat16
I32 = jnp.int32


def _dot_nt(a, b):
    return lax.dot_general(a, b, (((1,), (1,)), ((), ())), preferred_element_type=F32)


def _cparams(sem):
    return pltpu.CompilerParams(dimension_semantics=sem, vmem_limit_bytes=VMEM_LIMIT_BYTES)


def _sort_key(score):
    bits = lax.bitcast_convert_type(score, I32)
    return bits ^ ((bits >> 31) & 0x7FFFFFFF)


def _lane_lo(width=128):
    return lax.broadcasted_iota(I32, (1, width), 1) < 64


def _ln_kernel(x_ref, g_ref, b_ref, o_ref):
    x = x_ref[...]
    mu = jnp.mean(x, axis=-1, keepdims=True)
    xc = x - mu
    var = jnp.mean(xc * xc, axis=-1, keepdims=True)
    o_ref[...] = xc * lax.rsqrt(var + LN_EPS) * g_ref[...] + b_ref[...]


def _layernorm_rows(x, g, b, tm):
    rows, d = x.shape
    return pl.pallas_call(
        _ln_kernel,
        grid=(rows // tm,),
        in_specs=[pl.BlockSpec((tm, d), lambda i: (i, 0)),
                  pl.BlockSpec((1, d), lambda i: (0, 0)),
                  pl.BlockSpec((1, d), lambda i: (0, 0))],
        out_specs=pl.BlockSpec((tm, d), lambda i: (i, 0)),
        out_shape=jax.ShapeDtypeStruct((rows, d), F32),
        compiler_params=_cparams(("arbitrary",)),
        name="ln_in",
    )(x, g.reshape(1, d), b.reshape(1, d))


def _ln_prompt_kernel(meta_ref, prev_ref, cur_ref, g_ref, b_ref, o_ref, *, n_src_tiles):
    i = pl.program_id(1)
    top = jnp.where(i == 0, meta_ref[...], jnp.where(i <= n_src_tiles, prev_ref[...], 0.0))
    body = jnp.where(i < n_src_tiles, cur_ref[...][0:cur_ref.shape[0] - N_META], 0.0)
    x = jnp.concatenate([top, body], axis=0)
    mu = jnp.mean(x, axis=-1, keepdims=True)
    xc = x - mu
    var = jnp.mean(xc * xc, axis=-1, keepdims=True)
    o_ref[...] = xc * lax.rsqrt(var + LN_EPS) * g_ref[...] + b_ref[...]


def _layernorm_prompt(x_prompt, meta, g, b, tpad, tm):
    nb, seq, d = x_prompt.shape
    n_src = seq // tm
    per = tm // N_META
    return pl.pallas_call(
        functools.partial(_ln_prompt_kernel, n_src_tiles=n_src),
        grid=(nb, tpad // tm),
        in_specs=[pl.BlockSpec((N_META, d), lambda bi, i: (0, 0)),
                  pl.BlockSpec((None, N_META, d),
                               lambda bi, i: (bi, jnp.clip(i * per - 1, 0, seq // N_META - 1), 0)),
                  pl.BlockSpec((None, tm, d), lambda bi, i: (bi, jnp.minimum(i, n_src - 1), 0)),
                  pl.BlockSpec((1, d), lambda bi, i: (0, 0)),
                  pl.BlockSpec((1, d), lambda bi, i: (0, 0))],
        out_specs=pl.BlockSpec((None, tm, d), lambda bi, i: (bi, i, 0)),
        out_shape=jax.ShapeDtypeStruct((nb, tpad, d), F32),
        compiler_params=_cparams(("arbitrary", "arbitrary")),
        name="ln_in_prompt",
    )(meta, x_prompt, x_prompt, g.reshape(1, d), b.reshape(1, d))


def _proj_kernel(x_ref, w_ref, *rest, last_out_tile):
    p16_ref, p32_ref, ak_ref, av_ref, bk_ref, bv_ref, ik_ref = rest[-7:]
    xb = x_ref[...].astype(BF16)
    res = jnp.dot(xb, w_ref[...], preferred_element_type=F32)
    p16_ref[...] = res[:, :W16].astype(p16_ref.dtype)
    p32_ref[...] = res[:, W16:]

    @pl.when(pl.program_id(1) <= last_out_tile)
    def _():
        tm = res.shape[0]
        ak_ref[...] = res[:, C_AK:C_AK + 256].T
        for n in range(A_KV_HEADS):
            av_ref[pl.ds(n, tm, stride=A_KV_HEADS), :] = res[:, C_AV + 128 * n:C_AV + 128 * (n + 1)]
        bk_ref[...] = res[:, C_BK:C_BK + 128].T
        bv_ref[...] = res[:, C_BV:C_BV + 128].T
        ik_ref[...] = res[:, C_IK:C_IK + 128].T[0:IDX_DIM]


def _project(x, w, layer, depth, stacks, t_out, tm, p16_dtype, name):
    nb, t, d = x.shape
    last = pl.cdiv(t_out, tm) - 1
    row = lambda wd: pl.BlockSpec((None, tm, wd), lambda b, i: (b, i, 0))
    out = lambda ft: pl.BlockSpec((None, None, ft, tm), lambda b, i: (layer, b, 0, jnp.minimum(i, last)))
    shp = lambda ft: jax.ShapeDtypeStruct((depth, nb, ft, t_out), F32)
    out_v = pl.BlockSpec((None, None, A_KV_HEADS * tm, 128), lambda b, i: (layer, b, jnp.minimum(i, last), 0))
    shp_v = jax.ShapeDtypeStruct((depth, nb, A_KV_HEADS * t_out, 128), F32)
    stacks = list(stacks or ())
    return pl.pallas_call(
        functools.partial(_proj_kernel, last_out_tile=last),
        grid=(nb, t // tm),
        in_specs=[row(d), pl.BlockSpec((None, d, NCOL), lambda b, i: (layer, 0, 0))]
                 + [pl.BlockSpec(memory_space=pl.ANY)] * len(stacks),
        out_specs=[row(W16), row(W32), out(256), out_v, out(128), out(128), out(IDX_DIM)],
        out_shape=[jax.ShapeDtypeStruct((nb, t, W16), p16_dtype),
                   jax.ShapeDtypeStruct((nb, t, W32), F32),
                   shp(256), shp_v, shp(128), shp(128), shp(IDX_DIM)],
        input_output_aliases={2 + j: 2 + j for j in range(len(stacks))},
        compiler_params=_cparams(("arbitrary", "arbitrary")),
        name=name,
    )(x, w, *stacks)


def _diff_lambda(lam_ref):
    r = lam_ref[...]
    e1 = jnp.exp(jnp.sum(r[0:1] * r[1:2], axis=1, keepdims=True))
    e2 = jnp.exp(jnp.sum(r[2:3] * r[3:4], axis=1, keepdims=True))
    return e1 - e2 + r[4:5, 0:1]


def _half_select(x, keep_low):
    lo = _lane_lo()
    return jnp.where(lo, x, 0.0) if keep_low else jnp.where(lo, 0.0, x)


def _diff_query_rows(q, n_rows):
    pieces = [q[:, 0:128], q[:, 128:256]]
    return jnp.concatenate([_half_select(p, True) for p in pieces] +
                           [_half_select(p, False) for p in pieces], axis=0).astype(BF16)


def _dsa_query_rows(q):
    out = []
    for h in range(B_HEADS):
        n = h // (B_HEADS // B_KV_HEADS)
        p = q[:, 128 * (h // 2):128 * (h // 2) + 128]
        if h % 2 != n:
            p = pltpu.roll(p, 64, 1)
        out.append(_half_select(p, n == 0))
    return jnp.concatenate(out, axis=0).astype(BF16)


def _dsa_merge_heads(o, r):
    outs = []
    lo = _lane_lo()
    for j in range(B_HEADS // 2):
        n = j // 2
        oe, oo = o[2 * j * r:(2 * j + 1) * r], o[(2 * j + 1) * r:(2 * j + 2) * r]
        if n == 0:
            outs.append(jnp.where(lo, oe, pltpu.roll(oo, 64, 1)))
        else:
            outs.append(jnp.where(lo, pltpu.roll(oe, 64, 1), oo))
    return jnp.concatenate(outs, axis=1)


def _const_rows(values, r):
    return jnp.concatenate([jnp.full((r, 1), v, F32) for v in values], axis=0)


def _count(mask):
    return jnp.sum(jnp.where(mask, 1.0, 0.0), axis=1, keepdims=True)


def _lane_fold(x, op):
    acc = x[:, 0:128]
    for c in range(1, x.shape[1] // 128):
        acc = op(acc, x[:, 128 * c:128 * (c + 1)])
    return acc


def _for_chunks_paired(n, body):
    def pair(jj, _):
        body(2 * jj, 0)
        body(2 * jj + 1, 1)
        return 0

    lax.fori_loop(0, n // 2, pair, 0)

    @pl.when(n % 2 == 1)
    def _():
        body(n - 1, 0)


def _store_scores(j, qk, shifts, tq, s_ref, m_ref):
    for r, shift in enumerate(shifts):
        rows = slice(r * tq, (r + 1) * tq)
        s = qk[rows] * LOG2E + shift
        s_ref[j, rows, :] = s
        m_ref[rows, :] = jnp.maximum(m_ref[rows, :], _lane_fold(s, jnp.maximum))


def _softmax_weigh(n_chunks, tc, rows, s_ref, v_ref, m_ref, acc_ref, lacc_ref):
    m = jnp.broadcast_to(jnp.max(m_ref[...], axis=1, keepdims=True), (rows, 128))
    acc_ref[...] = jnp.zeros_like(acc_ref)
    lacc_ref[...] = jnp.zeros_like(lacc_ref)

    def weigh(j):
        s = s_ref[j]
        p = jnp.concatenate([jnp.exp2(s[:, 128 * c:128 * (c + 1)] - m) for c in range(tc // 128)], axis=1)
        pv = jnp.dot(p.astype(BF16), v_ref[pl.ds(pl.multiple_of(j * tc, tc), tc), :], preferred_element_type=F32)
        return _lane_fold(p, jnp.add), pv

    def step(j, _slot):
        l, pv = weigh(j)
        lacc_ref[...] += l
        acc_ref[...] += pv

    _for_chunks_paired(n_chunks, step)
    return acc_ref[...] / jnp.sum(lacc_ref[...], axis=1, keepdims=True)


def _diff_prompt_kernel(lam_ref, q_ref, k_ref, v_ref, o_ref, s_ref, m_ref, acc_ref, lacc_ref, *, n_real_tiles):
    n = pl.program_id(1)
    i = pl.program_id(2)
    tq, tc = DIFF_Q_TILE, KV_CHUNK

    @pl.when(i >= n_real_tiles)
    def _():
        o_ref[...] = jnp.zeros_like(o_ref)

    @pl.when(i < n_real_tiles)
    def _():
        lam = _diff_lambda(lam_ref)
        qst = _diff_query_rows(q_ref[...].astype(F32), tq)
        slope_g = (jnp.where(n == 0, 2.0 ** -2, 2.0 ** -6), jnp.where(n == 0, 2.0 ** -4, 2.0 ** -8))
        qpos = i * tq + lax.broadcasted_iota(I32, (tq, 1), 0)
        col = lax.broadcasted_iota(I32, (1, tc), 1)
        m_ref[...] = jnp.full_like(m_ref, NEG_INF)

        def score_step(j, masked):
            k0 = pl.multiple_of(j * tc, tc)
            qk = _dot_nt(qst, k_ref[pl.ds(k0, tc), :])
            kpos = (col + k0).astype(F32)
            shifts = [(slope_g[r % 2] * LOG2E) * kpos for r in range(4)]
            if masked:
                causal = jnp.where(col + k0 <= qpos, 0.0, NEG_INF)
                shifts = [sh + causal for sh in shifts[:2]] * 2
            _store_scores(j, qk, shifts, tq, s_ref, m_ref)

        n_full = (i * tq) // tc
        _for_chunks_paired(n_full, lambda j, _slot: score_step(j, False))
        score_step(n_full, True)
        o = _softmax_weigh(n_full + 1, tc, 4 * tq, s_ref, v_ref, m_ref, acc_ref, lacc_ref)
        out_g0 = o[0:tq] - lam * o[2 * tq:3 * tq]
        out_g1 = o[tq:2 * tq] - lam * o[3 * tq:4 * tq]
        o_ref[...] = jnp.concatenate([out_g0, out_g1], axis=1)


def _diff_prompt(p16, lam, t_real):
    nb, tpad, _ = p16.shape
    tq = DIFF_Q_TILE
    kern = functools.partial(_diff_prompt_kernel, n_real_tiles=pl.cdiv(t_real, tq))
    return pl.pallas_call(
        kern,
        grid=(nb, A_KV_HEADS, tpad // tq),
        in_specs=[pl.BlockSpec((8, 128), lambda b, n, i: (0, 0)),
                  pl.BlockSpec((None, tq, 256), lambda b, n, i: (b, i, C_AQ // 256 + n)),
                  pl.BlockSpec((None, tpad, 128), lambda b, n, i: (b, 0, C_AK // 128 + n)),
                  pl.BlockSpec((None, tpad, 128), lambda b, n, i: (b, 0, C_AV // 128 + n))],
        out_specs=pl.BlockSpec((None, tq, 256), lambda b, n, i: (b, i, n)),
        out_shape=jax.ShapeDtypeStruct((nb, tpad, 512), F32),
        scratch_shapes=[pltpu.VMEM((tpad // KV_CHUNK, 4 * tq, KV_CHUNK), F32), pltpu.VMEM((4 * tq, 128), F32),
                        pltpu.VMEM((4 * tq, 128), F32), pltpu.VMEM((4 * tq, 128), F32)],
        compiler_params=_cparams(("arbitrary", "arbitrary", "arbitrary")),
        name="diff_prompt",
    )(lam, p16, p16, p16)


def _dsa_prompt_kernel(bq_ref, iq_ref, iw_ref, bk_ref, bv_ref, ik_ref, o_ref,
                       keys_ref, keys_t_ref, mask_ref, s_ref, m_ref, acc_ref, lacc_ref, *,
                       n_real_tiles, topk, idx_bits):
    i = pl.program_id(1)
    tq, tc = Q_TILE, DSA_CHUNK

    @pl.when(i >= n_real_tiles)
    def _():
        o_ref[...] = jnp.zeros_like(o_ref)

    @pl.when(i < n_real_tiles)
    def _():
        n_chunks = (i * tq) // tc + 1
        col = lax.broadcasted_iota(I32, (1, tc), 1)
        qpos = i * tq + lax.broadcasted_iota(I32, (tq, 1), 0)

        iq = iq_ref[...].astype(F32)
        iqst = jnp.concatenate(
            [_half_select(iq[:, 128 * (h // 2):128 * (h // 2) + 128], h % 2 == 0)
             for h in range(IDX_HEADS)], axis=0).astype(BF16)
        w = iw_ref[...]
        wcol = [jnp.broadcast_to(w[:, h:h + 1], (tq, 128)) for h in range(IDX_HEADS)]

        def index_step(j, _slot):
            k0 = pl.multiple_of(j * tc, tc)
            logits = _dot_nt(iqst, ik_ref[pl.ds(k0, tc), :])
            blocks = []
            for c in range(tc // 128):
                cols = slice(128 * c, 128 * (c + 1))
                acc = jnp.maximum(logits[0:tq, cols], 0.0) * wcol[0]
                for h in range(1, IDX_HEADS):
                    acc = acc + jnp.maximum(logits[h * tq:(h + 1) * tq, cols], 0.0) * wcol[h]
                blocks.append(acc)
            sc = jnp.concatenate(blocks, axis=1)
            key = jnp.where(col + k0 <= qpos, _sort_key(sc * IDX_SCALE), INT_MIN)
            keys_ref[j] = key
            keys_t_ref[j] = key.T

        _for_chunks_paired(n_chunks, index_step)

        kidx = lax.broadcasted_iota(I32, (tc, 1), 0)

        def count_over_chunks(pred):
            def body(j, acc):
                hit = jnp.where(pred(keys_t_ref[j], kidx + j * tc), 1.0, 0.0)
                parts = [hit[8 * r:8 * (r + 1)] for r in range(tc // 8)]
                while len(parts) > 1:
                    parts = [parts[r] + parts[r + 1] for r in range(0, len(parts), 2)]
                return acc + parts[0]
            acc = lax.fori_loop(0, n_chunks, body, jnp.zeros((8, tq), F32))
            return jnp.sum(acc, axis=0, keepdims=True)

        def bit_step(b, carry):
            t, n_ge = carry
            cand = t + lax.shift_left(jnp.int32(1), 31 - b)
            cnt = count_over_chunks(lambda kc, _: kc >= cand)
            return jnp.where(cnt >= topk, cand, t), jnp.where(cnt >= topk, cnt, n_ge)

        t, n_ge = lax.fori_loop(0, 32, bit_step, (jnp.full((1, tq), INT_MIN, I32), jnp.zeros((1, tq), F32)))
        thr_t = jnp.maximum(t, INT_MIN + 1)
        n_gt = count_over_chunks(lambda kc, _: kc > thr_t)
        need = topk - n_gt
        n_eq = jnp.where(t == INT_MIN, 0.0, n_ge - n_gt)
        excess = n_eq > need

        def tie_bound():
            def tie_step(b, jb):
                cand = jb + lax.shift_left(jnp.int32(1), idx_bits - 1 - b)
                cnt = count_over_chunks(lambda kc, idx: (kc == thr_t) & (idx < cand))
                return jnp.where(cnt < need, cand, jb)
            return lax.fori_loop(0, idx_bits, tie_step, jnp.zeros((1, tq), I32))

        any_excess = jnp.max(jnp.where(excess, 1.0, 0.0)) > 0.0
        big = jnp.full((1, tq), 2 ** 30, I32)
        bound_t = lax.cond(any_excess, lambda: jnp.where(excess, tie_bound(), big), lambda: big)
        to_col = lambda v: jnp.broadcast_to(v, (tq, tq)).T[:, 0:1]
        thr, bound = to_col(thr_t), to_col(bound_t)

        qst = _dsa_query_rows(bq_ref[...].astype(F32))

        m_ref[...] = jnp.full_like(m_ref, NEG_INF)

        def score_step(j, slot):
            k0 = pl.multiple_of(j * tc, tc)
            kc = keys_ref[j]
            sel = (kc > thr) | ((kc == thr) & (col + k0 <= bound))
            mask_ref[slot] = jnp.where(sel, 0.0, NEG_INF)
            qk = _dot_nt(qst, bk_ref[pl.ds(k0, tc), :])
            kpos = (col + k0).astype(F32)
            _store_scores(j, qk, [(2.0 ** -(h + 1) * LOG2E) * kpos + mask_ref[slot] for h in range(B_HEADS)],
                          tq, s_ref, m_ref)

        _for_chunks_paired(n_chunks, score_step)
        o = _softmax_weigh(n_chunks, tc, B_HEADS * tq, s_ref, bv_ref, m_ref, acc_ref, lacc_ref)
        o_ref[...] = _dsa_merge_heads(o, tq)


def _dsa_prompt(p16, p32, t_real, topk):
    nb, tpad, _ = p16.shape
    nq = tpad // Q_TILE
    kern = functools.partial(_dsa_prompt_kernel, n_real_tiles=pl.cdiv(t_real, Q_TILE), topk=float(topk),
                             idx_bits=max(1, int(tpad).bit_length()))
    whole = lambda c: pl.BlockSpec((None, tpad, 128), lambda b, i: (b, 0, c // 128))
    return pl.pallas_call(
        kern,
        grid=(nb, nq),
        in_specs=[pl.BlockSpec((None, Q_TILE, 512), lambda b, i: (b, i, C_BQ // 512)),
                  pl.BlockSpec((None, Q_TILE, 512), lambda b, i: (b, i, C_IQ // 512)),
                  pl.BlockSpec((None, Q_TILE, 128), lambda b, i: (b, i, C_IW // 128)),
                  whole(C_BK), whole(C_BV), whole(C_IK)],
        out_specs=pl.BlockSpec((None, Q_TILE, 512), lambda b, i: (b, i, 0)),
        out_shape=jax.ShapeDtypeStruct((nb, tpad, 512), F32),
        scratch_shapes=[pltpu.VMEM((tpad // DSA_CHUNK, Q_TILE, DSA_CHUNK), I32),
                        pltpu.VMEM((tpad // DSA_CHUNK, DSA_CHUNK, Q_TILE), I32),
                        pltpu.VMEM((2, Q_TILE, DSA_CHUNK), F32),
                        pltpu.VMEM((tpad // DSA_CHUNK, B_HEADS * Q_TILE, DSA_CHUNK), F32),
                        pltpu.VMEM((B_HEADS * Q_TILE, 128), F32), pltpu.VMEM((B_HEADS * Q_TILE, 128), F32),
                        pltpu.VMEM((B_HEADS * Q_TILE, 128), F32)],
        compiler_params=_cparams(("arbitrary", "arbitrary")),
        name="dsa_prompt",
    )(p16, p16, p32, p16, p16, p16)


def _merge_kernel(sc_ref, x_ref, ya_ref, yb_ref, az_ref, bz_ref, w_ref, sg_ref, g_ref, b_ref, o_ref, *carry,
                  alpha, drop_rows):
    one_minus_lam_init = sc_ref[5:6, 0:1]
    ya = ya_ref[...]
    parts = []
    for h in range(A_HEADS):
        o = ya[:, 128 * h:128 * h + 128]
        y = o * lax.rsqrt(jnp.mean(o * o, axis=-1, keepdims=True) + LN_EPS) * sg_ref[...]
        parts.append(y * one_minus_lam_init)
    az = az_ref[...]
    bz = bz_ref[...]
    ya_g = jnp.concatenate(parts, axis=1) * (az * (1.0 / (1.0 + jnp.exp(-az))))
    yb_g = yb_ref[...] * (bz * (1.0 / (1.0 + jnp.exp(-bz))))
    mix = jnp.concatenate([ya_g, yb_g], axis=1).astype(BF16)
    y = alpha * x_ref[...] + jnp.dot(mix, w_ref[...], preferred_element_type=F32)
    mu = jnp.mean(y, axis=-1, keepdims=True)
    yc = y - mu
    var = jnp.mean(yc * yc, axis=-1, keepdims=True)
    out = yc * lax.rsqrt(var + LN_EPS) * g_ref[...] + b_ref[...]
    if drop_rows == 0:
        o_ref[...] = out
    else:
        (tail_ref,) = carry

        @pl.when(pl.program_id(1) > 0)
        def _():
            o_ref[...] = jnp.concatenate([tail_ref[...], out[0:drop_rows]], axis=0)

        tail_ref[...] = out[drop_rows:]


def _merge(x, ya, yb, p32, w_out, layer, sc, sub_g, ln_g, ln_b, alpha, tm, name, out_rows=None):
    nb, t, d = x.shape
    row = lambda wd, c: pl.BlockSpec((None, tm, wd), lambda b, i: (b, i, c))
    vec = lambda wd: pl.BlockSpec((1, wd), lambda b, i: (0, 0))
    if out_rows is None:
        out_spec, out_shape, scratch, drop = row(d, 0), (nb, t, d), [], 0
    else:
        last = out_rows // tm - 1
        out_spec = pl.BlockSpec((None, tm, d), lambda b, i: (b, jnp.clip(i - 1, 0, last), 0))
        out_shape, scratch, drop = (nb, out_rows, d), [pltpu.VMEM((tm - N_META, d), F32)], N_META
    return pl.pallas_call(
        functools.partial(_merge_kernel, alpha=alpha, drop_rows=drop),
        grid=(nb, t // tm),
        in_specs=[pl.BlockSpec((8, 128), lambda b, i: (0, 0)),
                  row(d, 0), row(512, 0), row(512, 0), row(512, C_AZ // 512), row(512, C_BZ // 512),
                  pl.BlockSpec((None, d, d), lambda b, i: (layer, 0, 0)),
                  vec(128), vec(d), vec(d)],
        out_specs=out_spec,
        out_shape=jax.ShapeDtypeStruct(out_shape, F32),
        scratch_shapes=scratch,
        compiler_params=_cparams(("arbitrary", "arbitrary")),
        name=name,
    )(sc, x, ya, yb, p32, p32, w_out, sub_g.reshape(1, 128), ln_g.reshape(1, d), ln_b.reshape(1, d))


def _pad_rows(x, rows):
    return jnp.concatenate([x, jnp.zeros((rows - x.shape[0], x.shape[1]), x.dtype)], axis=0)


def _diff_sample_kernel(pt_ref, lam_ref, q_ref, kn_ref, vn_ref, *rest, n_pg, past):
    k_refs, v_refs = rest[:n_pg], rest[n_pg:2 * n_pg]
    o_ref, m_ref, l_ref, acc_ref = rest[2 * n_pg:]
    pg = pl.program_id(1)
    ds = q_ref.shape[0]
    rows = 2 * A_KV_HEADS * 2 * ds
    span = n_pg * PAGE_SIZE

    @pl.when(pg == 0)
    def _():
        m_ref[...] = jnp.full_like(m_ref, NEG_INF)
        l_ref[...] = jnp.zeros_like(l_ref)
        acc_ref[...] = jnp.zeros_like(acc_ref)

    q = q_ref[...]
    qst = jnp.concatenate([_diff_query_rows(q[:, 256 * n:256 * n + 256], ds) for n in range(A_KV_HEADS)], axis=0)
    slope = _const_rows([2.0 ** (-2 * (2 * n + g + 1)) for n in range(A_KV_HEADS) for _c in range(2)
                         for g in range(2)], ds)
    half = rows // A_KV_HEADS

    def attend(score, weigh, width, k0, mask):
        col = lax.broadcasted_iota(I32, (1, width), 1)
        s = jnp.concatenate([score(n, qst[half * n:half * (n + 1)]) for n in range(A_KV_HEADS)],
                            axis=0) + slope * col.astype(F32)
        if mask is not None:
            s = jnp.where(mask(col), s, NEG_INF)
        cj = slope * k0
        m, l, acc = m_ref[...][:, 0:1], l_ref[...][:, 0:1], acc_ref[...]
        mt = jnp.max(s, axis=1, keepdims=True)
        m_new = jnp.maximum(m, mt + cj)
        p = jnp.exp(s - (m_new - cj))
        alpha = jnp.exp(m - m_new)
        l = alpha * l + jnp.sum(p, axis=1, keepdims=True)
        pv = jnp.concatenate([weigh(n, p[half * n:half * (n + 1)].astype(BF16)) for n in range(A_KV_HEADS)],
                             axis=0)
        m_ref[...] = jnp.broadcast_to(m_new, m_ref.shape)
        l_ref[...] = jnp.broadcast_to(l, l_ref.shape)
        acc_ref[...] = alpha * acc + pv

    def past_score(n, q):
        kt = jnp.concatenate([r[128 * n:128 * n + 128, :] for r in k_refs], axis=1).astype(BF16)
        return jnp.dot(q, kt, preferred_element_type=F32)

    def past_weigh(n, p):
        v = jnp.concatenate([r[pl.ds(n, PAGE_SIZE, stride=A_KV_HEADS), :] for r in v_refs], axis=0).astype(BF16)
        return jnp.dot(p, v, preferred_element_type=F32)

    attend(past_score, past_weigh, span, (pg * span).astype(F32), None)

    @pl.when(pg == past // span - 1)
    def _():
        r = lax.broadcasted_iota(I32, (rows, 1), 0) & (ds - 1)
        kn = _pad_rows(kn_ref[...], 128).astype(BF16)
        vn = _pad_rows(vn_ref[...], 128).astype(BF16)
        attend(lambda n, q: _dot_nt(q, kn[:, 128 * n:128 * n + 128]),
               lambda n, p: jnp.dot(p, vn[:, 128 * n:128 * n + 128], preferred_element_type=F32),
               128, jnp.float32(past), lambda col: col <= r)
        lam = _diff_lambda(lam_ref)
        o = acc_ref[...] / l_ref[...][:, 0:1]
        outs = []
        for n in range(A_KV_HEADS):
            base = half * n
            for g in range(2):
                outs.append(o[base + g * ds:base + (g + 1) * ds]
                            - lam * o[base + (2 + g) * ds:base + (3 + g) * ds])
        o_ref[...] = jnp.concatenate(outs, axis=1)


def _diff_sample(p16s, lam, cache_k, cache_v, pt_flat, n_pages):
    db, ds, _ = p16s.shape
    n_pg = PAGES_PER_STEP
    rows = 2 * A_KV_HEADS * 2 * ds
    past = n_pages * PAGE_SIZE
    kern = functools.partial(_diff_sample_kernel, n_pg=n_pg, past=past)

    def page(r):
        return pl.BlockSpec((None, 256, PAGE_SIZE), lambda sb, pg, pt: (pt[sb * n_pages + pg * n_pg + r], 0, 0))

    new = lambda c: pl.BlockSpec((None, ds, 256), lambda sb, pg, pt: (sb, 0, c // 256))
    grid_spec = pltpu.PrefetchScalarGridSpec(
        num_scalar_prefetch=1,
        grid=(db, n_pages // n_pg),
        in_specs=[pl.BlockSpec((8, 128), lambda sb, pg, pt: (0, 0)),
                  pl.BlockSpec((None, ds, 512), lambda sb, pg, pt: (sb, 0, C_AQ // 512)),
                  new(C_AK), new(C_AV)] + [page(r) for r in range(n_pg)] * 2,
        out_specs=pl.BlockSpec((None, ds, 512), lambda sb, pg, pt: (sb, 0, 0)),
        scratch_shapes=[pltpu.VMEM((rows, 128), F32), pltpu.VMEM((rows, 128), F32),
                        pltpu.VMEM((rows, 128), F32)])
    return pl.pallas_call(
        kern, grid_spec=grid_spec,
        out_shape=jax.ShapeDtypeStruct((db, ds, 512), F32),
        compiler_params=_cparams(("arbitrary", "arbitrary")),
        name="diff_sample",
    )(pt_flat, lam, p16s, p16s, p16s, *([cache_k] * n_pg), *([cache_v] * n_pg))


def _dsa_sample_index_kernel(pt_ref, iq_ref, iw_ref, in_ref, *rest, n_pg, past, topk, idx_bits):
    i_refs = rest[:n_pg]
    keys_ref, keys_new_ref, thr_ref, bound_ref = rest[n_pg:]
    sb, g = pl.program_id(0), pl.program_id(1)
    ds = iq_ref.shape[0]
    span = n_pg * PAGE_SIZE
    n_groups = past // span
    cw = keys_ref.shape[2]
    n_sub = span // cw
    all_rows = keys_ref.shape[1]
    row0 = pl.multiple_of(sb * ds, ds)
    r_id = lax.broadcasted_iota(I32, (ds, 1), 0)

    def index_keys(ik, transposed):
        iq = iq_ref[...]
        heads = []
        for h in range(IDX_HEADS):
            p = iq[:, 128 * (h // 2):128 * (h // 2) + 128]
            if h % 2:
                p = pltpu.roll(p, 64, 1)
            heads.append(p[:, 0:64])
        iqst = jnp.concatenate(heads, axis=0).astype(BF16)
        if transposed:
            logits = jnp.dot(iqst, ik.astype(BF16), preferred_element_type=F32)
        else:
            logits = _dot_nt(iqst, ik.astype(BF16))
        w = iw_ref[...]
        sc = jnp.maximum(logits[0:ds], 0.0) * w[:, 0:1]
        for h in range(1, IDX_HEADS):
            sc = sc + jnp.maximum(logits[h * ds:(h + 1) * ds], 0.0) * w[:, h:h + 1]
        return _sort_key(sc * IDX_SCALE)

    ik = jnp.concatenate([r[...] for r in i_refs], axis=1)
    keys = index_keys(ik, True)
    for c in range(n_sub):
        keys_ref[g * n_sub + c, pl.ds(row0, ds), :] = keys[:, c * cw:(c + 1) * cw]

    @pl.when(g == n_groups - 1)
    def _():
        col = lax.broadcasted_iota(I32, (1, 128), 1)
        key_new = index_keys(_pad_rows(in_ref[...][:, 0:64], 128), False)
        keys_new_ref[pl.ds(row0, ds), :] = jnp.where(col <= r_id, key_new, INT_MIN)

    @pl.when((sb == pl.num_programs(0) - 1) & (g == n_groups - 1))
    def _():
        col = lax.broadcasted_iota(I32, (1, cw), 1)
        col_new = lax.broadcasted_iota(I32, (1, 128), 1) + past

        def count(pred):
            def body(j, acc):
                return acc + _lane_fold(jnp.where(pred(keys_ref[j], col + j * cw), 1.0, 0.0), jnp.add)
            acc = lax.fori_loop(0, keys_ref.shape[0], body, jnp.zeros((all_rows, 128), F32))
            acc = acc + jnp.where(pred(keys_new_ref[...], col_new), 1.0, 0.0)
            return jnp.sum(acc, axis=1, keepdims=True)

        def bit_step(b, t):
            cand = t + lax.shift_left(jnp.int32(1), 31 - b)
            return jnp.where(count(lambda kc, _: kc >= cand) >= topk, cand, t)

        t = lax.fori_loop(0, 32, bit_step, jnp.full((all_rows, 1), INT_MIN, I32))
        thr = jnp.maximum(t, INT_MIN + 1)
        need = topk - count(lambda kc, _: kc > thr)
        excess = count(lambda kc, _: kc == thr) > need

        def tie_bound():
            def tie_step(b, jb):
                cand = jb + lax.shift_left(jnp.int32(1), idx_bits - 1 - b)
                return jnp.where(count(lambda kc, ix: (kc == thr) & (ix < cand)) < need, cand, jb)
            return lax.fori_loop(0, idx_bits, tie_step, jnp.zeros((all_rows, 1), I32))

        big = jnp.full((all_rows, 1), 2 ** 30, I32)
        any_excess = jnp.max(jnp.where(excess, 1.0, 0.0)) > 0.0
        bound = lax.cond(any_excess, lambda: jnp.where(excess, tie_bound(), big), lambda: big)
        thr_ref[...] = jnp.broadcast_to(thr, thr_ref.shape)
        bound_ref[...] = jnp.broadcast_to(bound, bound_ref.shape)


def _dsa_sample_attend_kernel(pt_ref, bq_ref, kn_ref, vn_ref, keys_ref, keys_new_ref, thr_ref, bound_ref, *rest,
                              n_pg, past):
    k_refs, v_refs = rest[:n_pg], rest[n_pg:2 * n_pg]
    o_ref, m_ref, l_ref, acc_ref = rest[2 * n_pg:]
    g = pl.program_id(1)
    ds = bq_ref.shape[0]
    rows = B_HEADS * ds
    span = n_pg * PAGE_SIZE
    n_groups = past // span
    qst = _dsa_query_rows(bq_ref[...])
    slope = _const_rows([2.0 ** -(h + 1) for h in range(B_HEADS)], ds)

    def attend(qk, weigh, kc, k0):
        width = qk.shape[1]
        col = lax.broadcasted_iota(I32, (1, width), 1)
        thr, bound = thr_ref[...][:, 0:1], bound_ref[...][:, 0:1]
        sel = (kc > thr) | ((kc == thr) & (col + k0 <= bound))
        s = (qk + slope * col.astype(F32)).reshape(B_HEADS, ds, width)
        s = jnp.where(sel[None], s, NEG_INF).reshape(rows, width)
        cj = slope * k0.astype(F32)
        m, l, acc = m_ref[...][:, 0:1], l_ref[...][:, 0:1], acc_ref[...]
        mt = jnp.max(s, axis=1, keepdims=True)
        m_new = jnp.maximum(m, mt + cj)
        m_safe = jnp.where(m_new == NEG_INF, 0.0, m_new)
        p = jnp.exp(s - (m_safe - cj))
        alpha = jnp.exp(m - m_safe)
        m_ref[...] = jnp.broadcast_to(m_new, m_ref.shape)
        l_ref[...] = jnp.broadcast_to(alpha * l + jnp.sum(p, axis=1, keepdims=True), l_ref.shape)
        acc_ref[...] = alpha * acc + weigh(p.astype(BF16))

    @pl.when(g == 0)
    def _():
        m_ref[...] = jnp.full_like(m_ref, NEG_INF)
        l_ref[...] = jnp.zeros_like(l_ref)
        acc_ref[...] = jnp.zeros_like(acc_ref)

    kt = jnp.concatenate([r[...] for r in k_refs], axis=1).astype(BF16)
    vt = jnp.concatenate([r[...] for r in v_refs], axis=1).astype(BF16)
    kc = jnp.concatenate([keys_ref[c] for c in range(keys_ref.shape[0])], axis=1)
    attend(jnp.dot(qst, kt, preferred_element_type=F32), lambda p: _dot_nt(p, vt), kc, g * span)

    @pl.when(g == n_groups - 1)
    def _():
        kn = _pad_rows(kn_ref[...], 128).astype(BF16)
        vn = _pad_rows(vn_ref[...], 128).astype(BF16)
        attend(_dot_nt(qst, kn), lambda p: jnp.dot(p, vn, preferred_element_type=F32),
               keys_new_ref[...], jnp.int32(past))
        o_ref[...] = _dsa_merge_heads(acc_ref[...] / l_ref[...][:, 0:1], ds)


def _dsa_sample(p16s, p32s, cache_i, cache_k, cache_v, pt_flat, n_pages, topk):
    db, ds, _ = p16s.shape
    n_pg = PAGES_PER_STEP
    n_groups = n_pages // n_pg
    past = n_pages * PAGE_SIZE
    rows = B_HEADS * ds
    cw = min(DSA_CHUNK, n_pg * PAGE_SIZE)
    n_sub = n_pg * PAGE_SIZE // cw

    def page(r, feat):
        return pl.BlockSpec((None, feat, PAGE_SIZE), lambda sb, g, pt: (pt[sb * n_pages + g * n_pg + r], 0, 0))

    new = lambda wd, c: pl.BlockSpec((None, ds, wd), lambda sb, g, pt: (sb, 0, c // wd))
    whole = lambda shape: pl.BlockSpec(shape, lambda sb, g, pt: (0,) * len(shape))

    key_shapes = [(past // cw, db * ds, cw), (db * ds, 128), (db * ds, 128), (db * ds, 128)]
    keys, keys_new, thr, bound = pl.pallas_call(
        functools.partial(_dsa_sample_index_kernel, n_pg=n_pg, past=past, topk=float(topk),
                          idx_bits=int(past + 128).bit_length()),
        grid_spec=pltpu.PrefetchScalarGridSpec(
            num_scalar_prefetch=1,
            grid=(db, n_groups),
            in_specs=[new(512, C_IQ), new(128, C_IW), new(128, C_IK)] + [page(r, 64) for r in range(n_pg)],
            out_specs=[whole(s) for s in key_shapes]),
        out_shape=[jax.ShapeDtypeStruct(s, I32) for s in key_shapes],
        compiler_params=_cparams(("arbitrary", "arbitrary")),
        name="dsa_sample_index",
    )(pt_flat, p16s, p32s, p16s, *([cache_i] * n_pg))

    per_seq = pl.BlockSpec((ds, 128), lambda sb, g, pt: (sb, 0))
    return pl.pallas_call(
        functools.partial(_dsa_sample_attend_kernel, n_pg=n_pg, past=past),
        grid_spec=pltpu.PrefetchScalarGridSpec(
            num_scalar_prefetch=1,
            grid=(db, n_groups),
            in_specs=[new(512, C_BQ), new(128, C_BK), new(128, C_BV),
                      pl.BlockSpec((n_sub, ds, cw), lambda sb, g, pt: (g, sb, 0)), per_seq, per_seq, per_seq]
                     + [page(r, 128) for r in range(n_pg)] * 2,
            out_specs=pl.BlockSpec((None, ds, 512), lambda sb, g, pt: (sb, 0, 0)),
            scratch_shapes=[pltpu.VMEM((rows, 128), F32), pltpu.VMEM((rows, 128), F32),
                            pltpu.VMEM((rows, 128), F32)]),
        out_shape=jax.ShapeDtypeStruct((db, ds, 512), F32),
        compiler_params=_cparams(("arbitrary", "arbitrary")),
        name="dsa_sample_attend",
    )(pt_flat, p16s, p16s, p16s, keys, keys_new, thr, bound, *([cache_k] * n_pg), *([cache_v] * n_pg))


_W_IN = dict(aq=(0, 512), ak=(512, 256), av=(768, 256), az=(1024, 512), bq=(1536, 512), bk=(2048, 128),
             bv=(2176, 128), iq=(2304, 512), ik=(2816, 64), iw=(2880, 8), bz=(2888, 512))
_W_ORDER = ((("aq",), HEAD_DIM ** -0.5), (("bq",), HEAD_DIM ** -0.5), (("iq",), 1.0), (("ak",), 1.0), (("av",), 1.0),
            (("bk",), 1.0), (("bv",), 1.0), (("ik", "ik"), 1.0), (("az",), 1.0), (("bz",), 1.0), (("iw",), 1.0))


def _regroup_kernel(wt_ref, o_ref):
    dst = 0
    for names, scale in _W_ORDER:
        rows = jnp.concatenate([wt_ref[_W_IN[n][0]:_W_IN[n][0] + _W_IN[n][1], :] for n in names], axis=0)
        width = -(-rows.shape[0] // 128) * 128
        if rows.shape[0] < width:
            rows = jnp.concatenate([rows, jnp.zeros((width - rows.shape[0], rows.shape[1]), F32)], axis=0)
        o_ref[:, dst:dst + width] = (rows * scale).T.astype(BF16)
        dst += width


def _regroup_weights(w_in):
    depth, d, n_in = w_in.shape
    return pl.pallas_call(
        _regroup_kernel,
        grid=(depth,),
        in_specs=[pl.BlockSpec((None, n_in, d), lambda l: (l, 0, 0))],
        out_specs=pl.BlockSpec((None, d, NCOL), lambda l: (l, 0, 0)),
        out_shape=jax.ShapeDtypeStruct((depth, d, NCOL), BF16),
        compiler_params=_cparams(("arbitrary",)),
        name="regroup_w_in",
    )(w_in.transpose(0, 2, 1))


def _pick_tile(n, candidates):
    for c in candidates:
        if n % c == 0:
            return c
    return n


def kernel(x_prompt, x_sample, cache_diff_k, cache_diff_v, cache_dsa_k, cache_dsa_v, cache_idx_k, page_table,
           meta_tokens, ln_in_g, ln_in_b, w_in, w_out, lambda_q1, lambda_k1, lambda_q2, lambda_k2,
           subln_g, ln_g, ln_b):
    nb, seq, d = x_prompt.shape
    db, ds, _ = x_sample.shape
    depth = w_in.shape[0]
    n_pool = cache_diff_k.shape[1]
    n_pages = page_table.shape[1]
    t_real = seq + N_META
    tpad = -(-t_real // DSA_CHUNK) * DSA_CHUNK
    assert ds == 8 and n_pages % PAGES_PER_STEP == 0 and d == 1024
    alpha = (2 * depth) ** 0.25
    topk_p = min(TOPK_MAX, seq // 4)
    topk_s = min(TOPK_MAX, (n_pages * PAGE_SIZE + ds) // 4)

    tm_ln = next((c for c in (512, 256, 128, 64, 32) if seq % c == 0 and tpad % c == 0), None)
    if tm_ln is not None:
        xp = _layernorm_prompt(x_prompt, meta_tokens.astype(x_prompt.dtype), ln_in_g, ln_in_b, tpad, tm_ln)
    else:
        meta = jnp.broadcast_to(meta_tokens[None].astype(x_prompt.dtype), (nb, N_META, d))
        xp = jnp.concatenate([meta, x_prompt, jnp.zeros((nb, tpad - t_real, d), x_prompt.dtype)], axis=1)
        xp = _layernorm_rows(xp.reshape(nb * tpad, d), ln_in_g, ln_in_b, KV_CHUNK).reshape(nb, tpad, d)
    xs = _layernorm_rows(x_sample.reshape(db * ds, d), ln_in_g, ln_in_b,
                         _pick_tile(db * ds, (256, 128, 8))).reshape(1, db * ds, d)

    w = _regroup_weights(w_in)
    w_o = w_out.astype(BF16)
    pad64 = lambda v: jnp.pad(v, ((0, 0), (0, 128 - v.shape[-1])))
    ck = cache_diff_k.transpose(0, 1, 3, 4, 5, 2).reshape(depth * n_pool, 256, PAGE_SIZE)
    cv = cache_diff_v.reshape(depth * n_pool, PAGE_SIZE * A_KV_HEADS, 128)
    cbk = cache_dsa_k.transpose(0, 1, 3, 4, 2).reshape(depth * n_pool, 128, PAGE_SIZE)
    cbv = cache_dsa_v.transpose(0, 1, 3, 4, 2).reshape(depth * n_pool, 128, PAGE_SIZE)
    ci = cache_idx_k.transpose(0, 1, 3, 2).reshape(depth * n_pool, IDX_DIM, PAGE_SIZE)
    pt_flat = page_table.reshape(-1).astype(I32)
    tm_p = _pick_tile(tpad, (512, 256, 128))
    tm_s = _pick_tile(db * ds, (256, 128, 8))

    rows_p = rows_s = None
    for layer in range(depth):
        lam_init = 0.8 - 0.6 * math.exp(-0.3 * layer)
        consts = jnp.stack([jnp.full((128,), lam_init, F32), jnp.full((128,), 1.0 - lam_init, F32),
                            jnp.zeros((128,), F32), jnp.zeros((128,), F32)])
        lam = jnp.concatenate([pad64(jnp.stack([lambda_q1[layer], lambda_k1[layer],
                                                 lambda_q2[layer], lambda_k2[layer]])), consts], axis=0)

        p16, p32, *rows_p = _project(xp, w, layer, depth, rows_p, t_real, tm_p, BF16, "proj_prompt")
        ya = _diff_prompt(p16, lam, t_real)
        yb = _dsa_prompt(p16, p32, t_real, topk_p)
        direct = layer == depth - 1 and seq % tm_p == 0
        xp = _merge(xp, ya, yb, p32, w_o, layer, lam, subln_g[layer], ln_g[layer], ln_b[layer],
                    alpha, tm_p, "merge_prompt", out_rows=seq if direct else None)

        p16s, p32s, *rows_s = _project(xs, w, layer, depth, rows_s, db * ds, tm_s, F32, "proj_sample")
        p16s3 = p16s.reshape(db, ds, W16)
        pt_layer = pt_flat + layer * n_pool
        ya_s = _diff_sample(p16s3, lam, ck, cv, pt_layer, n_pages)
        yb_s = _dsa_sample(p16s3, p32s.reshape(db, ds, W32), ci, cbk, cbv, pt_layer, n_pages, topk_s)
        xs = _merge(xs, ya_s.reshape(1, db * ds, 512), yb_s.reshape(1, db * ds, 512), p32s, w_o, layer, lam,
                    subln_g[layer], ln_g[layer], ln_b[layer], alpha, tm_s, "merge_sample")

    def cache_rows(stacks, lead, t):
        k, v, bk, bv, ik = stacks
        return (k.reshape(depth, -1, A_KV_HEADS, 2, HEAD_DIM, t).transpose(0, 1, 5, 2, 3, 4)
                 .reshape((depth,) + lead + (A_KV_HEADS, 2, HEAD_DIM)),
                v.reshape((depth,) + lead + (A_KV_HEADS, 2 * HEAD_DIM)),
                bk.reshape(depth, -1, B_KV_HEADS, HEAD_DIM, t).transpose(0, 1, 4, 2, 3)
                  .reshape((depth,) + lead + (B_KV_HEADS, HEAD_DIM)),
                bv.reshape(depth, -1, B_KV_HEADS, HEAD_DIM, t).transpose(0, 1, 4, 2, 3)
                  .reshape((depth,) + lead + (B_KV_HEADS, HEAD_DIM)),
                ik.transpose(0, 1, 3, 2).reshape((depth,) + lead + (IDX_DIM,)))

    y_prompt = xp if xp.shape[1] == seq else xp[:, N_META:t_real]
    return ((y_prompt, xs.reshape(db, ds, d))
            + cache_rows(rows_p, (nb, t_real), t_real) + cache_rows(rows_s, (db, ds), db * ds))
```

```python
import functools
import math

import jax
import jax.numpy as jnp
from jax import lax
from jax.experimental import pallas as pl
from jax.experimental.pallas import tpu as pltpu

N_META = 16
HEAD_DIM = 64
A_HEADS = 4
A_KV_HEADS = 2
B_HEADS = 8
B_KV_HEADS = 2
IDX_HEADS = 8
IDX_DIM = 64
IDX_SCALE = (IDX_HEADS * IDX_DIM) ** -0.5
TOPK_MAX = 256
PAGE_SIZE = 128
LN_EPS = 1e-5

Q_TILE = 128
DIFF_Q_TILE = 256
KV_CHUNK = 512
DSA_CHUNK = 512
PAGES_PER_STEP = 64
VMEM_LIMIT_BYTES = 56 * 1024 * 1024

LOG2E = math.log2(math.e)
INT_MIN = -(2 ** 31)
NEG_INF = float("-inf")

C_AQ, C_BQ, C_IQ, C_AK, C_AV, C_BK, C_BV, C_IK = 0, 512, 1024, 1536, 1792, 2048, 2176, 2304
W16 = 2432
C_AZ, C_BZ, C_IW = 0, 512, 1024
W32 = 1152
NCOL = W16 + W32

F32 = jnp.float32
BF16 = jnp.bfloat16
I32 = jnp.int32


def _dot_nt(a, b):
    return lax.dot_general(a, b, (((1,), (1,)), ((), ())), preferred_element_type=F32)


def _cparams(sem):
    return pltpu.CompilerParams(dimension_semantics=sem, vmem_limit_bytes=VMEM_LIMIT_BYTES)


def _sort_key(score):
    bits = lax.bitcast_convert_type(score, I32)
    return bits ^ ((bits >> 31) & 0x7FFFFFFF)


def _lane_lo(width=128):
    return lax.broadcasted_iota(I32, (1, width), 1) < 64


def _ln_kernel(x_ref, g_ref, b_ref, o_ref):
    x = x_ref[...]
    mu = jnp.mean(x, axis=-1, keepdims=True)
    xc = x - mu
    var = jnp.mean(xc * xc, axis=-1, keepdims=True)
    o_ref[...] = xc * lax.rsqrt(var + LN_EPS) * g_ref[...] + b_ref[...]


def _layernorm_rows(x, g, b, tm):
    rows, d = x.shape
    return pl.pallas_call(
        _ln_kernel,
        grid=(rows // tm,),
        in_specs=[pl.BlockSpec((tm, d), lambda i: (i, 0)),
                  pl.BlockSpec((1, d), lambda i: (0, 0)),
                  pl.BlockSpec((1, d), lambda i: (0, 0))],
        out_specs=pl.BlockSpec((tm, d), lambda i: (i, 0)),
        out_shape=jax.ShapeDtypeStruct((rows, d), F32),
        compiler_params=_cparams(("arbitrary",)),
        name="ln_in",
    )(x, g.reshape(1, d), b.reshape(1, d))


def _ln_prompt_kernel(meta_ref, prev_ref, cur_ref, g_ref, b_ref, o_ref, *, n_src_tiles):
    i = pl.program_id(1)
    top = jnp.where(i == 0, meta_ref[...], jnp.where(i <= n_src_tiles, prev_ref[...], 0.0))
    body = jnp.where(i < n_src_tiles, cur_ref[...][0:cur_ref.shape[0] - N_META], 0.0)
    x = jnp.concatenate([top, body], axis=0)
    mu = jnp.mean(x, axis=-1, keepdims=True)
    xc = x - mu
    var = jnp.mean(xc * xc, axis=-1, keepdims=True)
    o_ref[...] = xc * lax.rsqrt(var + LN_EPS) * g_ref[...] + b_ref[...]


def _layernorm_prompt(x_prompt, meta, g, b, tpad, tm):
    nb, seq, d = x_prompt.shape
    n_src = seq // tm
    per = tm // N_META
    return pl.pallas_call(
        functools.partial(_ln_prompt_kernel, n_src_tiles=n_src),
        grid=(nb, tpad // tm),
        in_specs=[pl.BlockSpec((N_META, d), lambda bi, i: (0, 0)),
                  pl.BlockSpec((None, N_META, d),
                               lambda bi, i: (bi, jnp.clip(i * per - 1, 0, seq // N_META - 1), 0)),
                  pl.BlockSpec((None, tm, d), lambda bi, i: (bi, jnp.minimum(i, n_src - 1), 0)),
                  pl.BlockSpec((1, d), lambda bi, i: (0, 0)),
                  pl.BlockSpec((1, d), lambda bi, i: (0, 0))],
        out_specs=pl.BlockSpec((None, tm, d), lambda bi, i: (bi, i, 0)),
        out_shape=jax.ShapeDtypeStruct((nb, tpad, d), F32),
        compiler_params=_cparams(("arbitrary", "arbitrary")),
        name="ln_in_prompt",
    )(meta, x_prompt, x_prompt, g.reshape(1, d), b.reshape(1, d))


def _proj_kernel(x_ref, w_ref, *rest, last_out_tile):
    p16_ref, p32_ref, ak_ref, av_ref, bk_ref, bv_ref, ik_ref = rest[-7:]
    xb = x_ref[...].astype(BF16)
    res = jnp.dot(xb, w_ref[...], preferred_element_type=F32)
    p16_ref[...] = res[:, :W16].astype(p16_ref.dtype)
    p32_ref[...] = res[:, W16:]

    @pl.when(pl.program_id(1) <= last_out_tile)
    def _():
        tm = res.shape[0]
        ak_ref[...] = res[:, C_AK:C_AK + 256].T
        for n in range(A_KV_HEADS):
            av_ref[pl.ds(n, tm, stride=A_KV_HEADS), :] = res[:, C_AV + 128 * n:C_AV + 128 * (n + 1)]
        bk_ref[...] = res[:, C_BK:C_BK + 128].T
        bv_ref[...] = res[:, C_BV:C_BV + 128].T
        ik_ref[...] = res[:, C_IK:C_IK + 128].T[0:IDX_DIM]


def _project(x, w, layer, depth, stacks, t_out, tm, p16_dtype, name):
    nb, t, d = x.shape
    last = pl.cdiv(t_out, tm) - 1
    row = lambda wd: pl.BlockSpec((None, tm, wd), lambda b, i: (b, i, 0))
    out = lambda ft: pl.BlockSpec((None, None, ft, tm), lambda b, i: (layer, b, 0, jnp.minimum(i, last)))
    shp = lambda ft: jax.ShapeDtypeStruct((depth, nb, ft, t_out), F32)
    out_v = pl.BlockSpec((None, None, A_KV_HEADS * tm, 128), lambda b, i: (layer, b, jnp.minimum(i, last), 0))
    shp_v = jax.ShapeDtypeStruct((depth, nb, A_KV_HEADS * t_out, 128), F32)
    stacks = list(stacks or ())
    return pl.pallas_call(
        functools.partial(_proj_kernel, last_out_tile=last),
        grid=(nb, t // tm),
        in_specs=[row(d), pl.BlockSpec((None, d, NCOL), lambda b, i: (layer, 0, 0))]
                 + [pl.BlockSpec(memory_space=pl.ANY)] * len(stacks),
        out_specs=[row(W16), row(W32), out(256), out_v, out(128), out(128), out(IDX_DIM)],
        out_shape=[jax.ShapeDtypeStruct((nb, t, W16), p16_dtype),
                   jax.ShapeDtypeStruct((nb, t, W32), F32),
                   shp(256), shp_v, shp(128), shp(128), shp(IDX_DIM)],
        input_output_aliases={2 + j: 2 + j for j in range(len(stacks))},
        compiler_params=_cparams(("arbitrary", "arbitrary")),
        name=name,
    )(x, w, *stacks)


def _diff_lambda(lam_ref):
    r = lam_ref[...]
    e1 = jnp.exp(jnp.sum(r[0:1] * r[1:2], axis=1, keepdims=True))
    e2 = jnp.exp(jnp.sum(r[2:3] * r[3:4], axis=1, keepdims=True))
    return e1 - e2 + r[4:5, 0:1]


def _half_select(x, keep_low):
    lo = _lane_lo()
    return jnp.where(lo, x, 0.0) if keep_low else jnp.where(lo, 0.0, x)


def _diff_query_rows(q, n_rows):
    pieces = [q[:, 0:128], q[:, 128:256]]
    return jnp.concatenate([_half_select(p, True) for p in pieces] +
                           [_half_select(p, False) for p in pieces], axis=0).astype(BF16)


def _dsa_query_rows(q):
    out = []
    for h in range(B_HEADS):
        n = h // (B_HEADS // B_KV_HEADS)
        p = q[:, 128 * (h // 2):128 * (h // 2) + 128]
        if h % 2 != n:
            p = pltpu.roll(p, 64, 1)
        out.append(_half_select(p, n == 0))
    return jnp.concatenate(out, axis=0).astype(BF16)


def _dsa_merge_heads(o, r):
    outs = []
    lo = _lane_lo()
    for j in range(B_HEADS // 2):
        n = j // 2
        oe, oo = o[2 * j * r:(2 * j + 1) * r], o[(2 * j + 1) * r:(2 * j + 2) * r]
        if n == 0:
            outs.append(jnp.where(lo, oe, pltpu.roll(oo, 64, 1)))
        else:
            outs.append(jnp.where(lo, pltpu.roll(oe, 64, 1), oo))
    return jnp.concatenate(outs, axis=1)


def _const_rows(values, r):
    return jnp.concatenate([jnp.full((r, 1), v, F32) for v in values], axis=0)


def _count(mask):
    return jnp.sum(jnp.where(mask, 1.0, 0.0), axis=1, keepdims=True)


def _lane_fold(x, op):
    acc = x[:, 0:128]
    for c in range(1, x.shape[1] // 128):
        acc = op(acc, x[:, 128 * c:128 * (c + 1)])
    return acc


def _for_chunks_paired(n, body):
    def pair(jj, _):
        body(2 * jj, 0)
        body(2 * jj + 1, 1)
        return 0

    lax.fori_loop(0, n // 2, pair, 0)

    @pl.when(n % 2 == 1)
    def _():
        body(n - 1, 0)


def _store_scores(j, qk, shifts, tq, s_ref, m_ref):
    for r, shift in enumerate(shifts):
        rows = slice(r * tq, (r + 1) * tq)
        s = qk[rows] * LOG2E + shift
        s_ref[j, rows, :] = s
        m_ref[rows, :] = jnp.maximum(m_ref[rows, :], _lane_fold(s, jnp.maximum))


def _softmax_weigh(n_chunks, tc, rows, s_ref, v_ref, m_ref, acc_ref, lacc_ref):
    m = jnp.broadcast_to(jnp.max(m_ref[...], axis=1, keepdims=True), (rows, 128))
    acc_ref[...] = jnp.zeros_like(acc_ref)
    lacc_ref[...] = jnp.zeros_like(lacc_ref)

    def weigh(j):
        s = s_ref[j]
        p = jnp.concatenate([jnp.exp2(s[:, 128 * c:128 * (c + 1)] - m) for c in range(tc // 128)], axis=1)
        pv = jnp.dot(p.astype(BF16), v_ref[pl.ds(pl.multiple_of(j * tc, tc), tc), :], preferred_element_type=F32)
        return _lane_fold(p, jnp.add), pv

    def step(j, _slot):
        l, pv = weigh(j)
        lacc_ref[...] += l
        acc_ref[...] += pv

    _for_chunks_paired(n_chunks, step)
    return acc_ref[...] / jnp.sum(lacc_ref[...], axis=1, keepdims=True)


def _diff_prompt_kernel(lam_ref, q_ref, k_ref, v_ref, o_ref, s_ref, m_ref, acc_ref, lacc_ref, *, n_real_tiles):
    n = pl.program_id(1)
    i = pl.program_id(2)
    tq, tc = DIFF_Q_TILE, KV_CHUNK

    @pl.when(i >= n_real_tiles)
    def _():
        o_ref[...] = jnp.zeros_like(o_ref)

    @pl.when(i < n_real_tiles)
    def _():
        lam = _diff_lambda(lam_ref)
        qst = _diff_query_rows(q_ref[...].astype(F32), tq)
        slope_g = (jnp.where(n == 0, 2.0 ** -2, 2.0 ** -6), jnp.where(n == 0, 2.0 ** -4, 2.0 ** -8))
        qpos = i * tq + lax.broadcasted_iota(I32, (tq, 1), 0)
        col = lax.broadcasted_iota(I32, (1, tc), 1)
        m_ref[...] = jnp.full_like(m_ref, NEG_INF)

        def score_step(j, masked):
            k0 = pl.multiple_of(j * tc, tc)
            qk = _dot_nt(qst, k_ref[pl.ds(k0, tc), :])
            kpos = (col + k0).astype(F32)
            shifts = [(slope_g[r % 2] * LOG2E) * kpos for r in range(4)]
            if masked:
                causal = jnp.where(col + k0 <= qpos, 0.0, NEG_INF)
                shifts = [sh + causal for sh in shifts[:2]] * 2
            _store_scores(j, qk, shifts, tq, s_ref, m_ref)

        n_full = (i * tq) // tc
        _for_chunks_paired(n_full, lambda j, _slot: score_step(j, False))
        score_step(n_full, True)
        o = _softmax_weigh(n_full + 1, tc, 4 * tq, s_ref, v_ref, m_ref, acc_ref, lacc_ref)
        out_g0 = o[0:tq] - lam * o[2 * tq:3 * tq]
        out_g1 = o[tq:2 * tq] - lam * o[3 * tq:4 * tq]
        o_ref[...] = jnp.concatenate([out_g0, out_g1], axis=1)


def _diff_prompt(p16, lam, t_real):
    nb, tpad, _ = p16.shape
    tq = DIFF_Q_TILE
    kern = functools.partial(_diff_prompt_kernel, n_real_tiles=pl.cdiv(t_real, tq))
    return pl.pallas_call(
        kern,
        grid=(nb, A_KV_HEADS, tpad // tq),
        in_specs=[pl.BlockSpec((8, 128), lambda b, n, i: (0, 0)),
                  pl.BlockSpec((None, tq, 256), lambda b, n, i: (b, i, C_AQ // 256 + n)),
                  pl.BlockSpec((None, tpad, 128), lambda b, n, i: (b, 0, C_AK // 128 + n)),
                  pl.BlockSpec((None, tpad, 128), lambda b, n, i: (b, 0, C_AV // 128 + n))],
        out_specs=pl.BlockSpec((None, tq, 256), lambda b, n, i: (b, i, n)),
        out_shape=jax.ShapeDtypeStruct((nb, tpad, 512), F32),
        scratch_shapes=[pltpu.VMEM((tpad // KV_CHUNK, 4 * tq, KV_CHUNK), F32), pltpu.VMEM((4 * tq, 128), F32),
                        pltpu.VMEM((4 * tq, 128), F32), pltpu.VMEM((4 * tq, 128), F32)],
        compiler_params=_cparams(("arbitrary", "arbitrary", "arbitrary")),
        name="diff_prompt",
    )(lam, p16, p16, p16)


def _dsa_prompt_kernel(bq_ref, iq_ref, iw_ref, bk_ref, bv_ref, ik_ref, o_ref,
                       keys_ref, keys_t_ref, mask_ref, s_ref, m_ref, acc_ref, lacc_ref, *,
                       n_real_tiles, topk, idx_bits):
    i = pl.program_id(1)
    tq, tc = Q_TILE, DSA_CHUNK

    @pl.when(i >= n_real_tiles)
    def _():
        o_ref[...] = jnp.zeros_like(o_ref)

    @pl.when(i < n_real_tiles)
    def _():
        n_chunks = (i * tq) // tc + 1
        col = lax.broadcasted_iota(I32, (1, tc), 1)
        qpos = i * tq + lax.broadcasted_iota(I32, (tq, 1), 0)

        iq = iq_ref[...].astype(F32)
        iqst = jnp.concatenate(
            [_half_select(iq[:, 128 * (h // 2):128 * (h // 2) + 128], h % 2 == 0)
             for h in range(IDX_HEADS)], axis=0).astype(BF16)
        w = iw_ref[...]
        wcol = [jnp.broadcast_to(w[:, h:h + 1], (tq, 128)) for h in range(IDX_HEADS)]

        def index_step(j, _slot):
            k0 = pl.multiple_of(j * tc, tc)
            logits = _dot_nt(iqst, ik_ref[pl.ds(k0, tc), :])
            blocks = []
            for c in range(tc // 128):
                cols = slice(128 * c, 128 * (c + 1))
                acc = jnp.maximum(logits[0:tq, cols], 0.0) * wcol[0]
                for h in range(1, IDX_HEADS):
                    acc = acc + jnp.maximum(logits[h * tq:(h + 1) * tq, cols], 0.0) * wcol[h]
                blocks.append(acc)
            sc = jnp.concatenate(blocks, axis=1)
            key = jnp.where(col + k0 <= qpos, _sort_key(sc * IDX_SCALE), INT_MIN)
            keys_ref[j] = key
            keys_t_ref[j] = key.T

        _for_chunks_paired(n_chunks, index_step)

        kidx = lax.broadcasted_iota(I32, (tc, 1), 0)

        def count_over_chunks(pred):
            def body(j, acc):
                hit = jnp.where(pred(keys_t_ref[j], kidx + j * tc), 1.0, 0.0)
                parts = [hit[8 * r:8 * (r + 1)] for r in range(tc // 8)]
                while len(parts) > 1:
                    parts = [parts[r] + parts[r + 1] for r in range(0, len(parts), 2)]
                return acc + parts[0]
            acc = lax.fori_loop(0, n_chunks, body, jnp.zeros((8, tq), F32))
            return jnp.sum(acc, axis=0, keepdims=True)

        def bit_step(b, carry):
            t, n_ge = carry
            cand = t + lax.shift_left(jnp.int32(1), 31 - b)
            cnt = count_over_chunks(lambda kc, _: kc >= cand)
            return jnp.where(cnt >= topk, cand, t), jnp.where(cnt >= topk, cnt, n_ge)

        t, n_ge = lax.fori_loop(0, 32, bit_step, (jnp.full((1, tq), INT_MIN, I32), jnp.zeros((1, tq), F32)))
        thr_t = jnp.maximum(t, INT_MIN + 1)
        n_gt = count_over_chunks(lambda kc, _: kc > thr_t)
        need = topk - n_gt
        n_eq = jnp.where(t == INT_MIN, 0.0, n_ge - n_gt)
        excess = n_eq > need

        def tie_bound():
            def tie_step(b, jb):
                cand = jb + lax.shift_left(jnp.int32(1), idx_bits - 1 - b)
                cnt = count_over_chunks(lambda kc, idx: (kc == thr_t) & (idx < cand))
                return jnp.where(cnt < need, cand, jb)
            return lax.fori_loop(0, idx_bits, tie_step, jnp.zeros((1, tq), I32))

        any_excess = jnp.max(jnp.where(excess, 1.0, 0.0)) > 0.0
        big = jnp.full((1, tq), 2 ** 30, I32)
        bound_t = lax.cond(any_excess, lambda: jnp.where(excess, tie_bound(), big), lambda: big)
        to_col = lambda v: jnp.broadcast_to(v, (tq, tq)).T[:, 0:1]
        thr, bound = to_col(thr_t), to_col(bound_t)

        qst = _dsa_query_rows(bq_ref[...].astype(F32))

        m_ref[...] = jnp.full_like(m_ref, NEG_INF)

        def score_step(j, slot):
            k0 = pl.multiple_of(j * tc, tc)
            kc = keys_ref[j]
            sel = (kc > thr) | ((kc == thr) & (col + k0 <= bound))
            mask_ref[slot] = jnp.where(sel, 0.0, NEG_INF)
            qk = _dot_nt(qst, bk_ref[pl.ds(k0, tc), :])
            kpos = (col + k0).astype(F32)
            _store_scores(j, qk, [(2.0 ** -(h + 1) * LOG2E) * kpos + mask_ref[slot] for h in range(B_HEADS)],
                          tq, s_ref, m_ref)

        _for_chunks_paired(n_chunks, score_step)
        o = _softmax_weigh(n_chunks, tc, B_HEADS * tq, s_ref, bv_ref, m_ref, acc_ref, lacc_ref)
        o_ref[...] = _dsa_merge_heads(o, tq)


def _dsa_prompt(p16, p32, t_real, topk):
    nb, tpad, _ = p16.shape
    nq = tpad // Q_TILE
    kern = functools.partial(_dsa_prompt_kernel, n_real_tiles=pl.cdiv(t_real, Q_TILE), topk=float(topk),
                             idx_bits=max(1, int(tpad).bit_length()))
    whole = lambda c: pl.BlockSpec((None, tpad, 128), lambda b, i: (b, 0, c // 128))
    return pl.pallas_call(
        kern,
        grid=(nb, nq),
        in_specs=[pl.BlockSpec((None, Q_TILE, 512), lambda b, i: (b, i, C_BQ // 512)),
                  pl.BlockSpec((None, Q_TILE, 512), lambda b, i: (b, i, C_IQ // 512)),
                  pl.BlockSpec((None, Q_TILE, 128), lambda b, i: (b, i, C_IW // 128)),
                  whole(C_BK), whole(C_BV), whole(C_IK)],
        out_specs=pl.BlockSpec((None, Q_TILE, 512), lambda b, i: (b, i, 0)),
        out_shape=jax.ShapeDtypeStruct((nb, tpad, 512), F32),
        scratch_shapes=[pltpu.VMEM((tpad // DSA_CHUNK, Q_TILE, DSA_CHUNK), I32),
                        pltpu.VMEM((tpad // DSA_CHUNK, DSA_CHUNK, Q_TILE), I32),
                        pltpu.VMEM((2, Q_TILE, DSA_CHUNK), F32),
                        pltpu.VMEM((tpad // DSA_CHUNK, B_HEADS * Q_TILE, DSA_CHUNK), F32),
                        pltpu.VMEM((B_HEADS * Q_TILE, 128), F32), pltpu.VMEM((B_HEADS * Q_TILE, 128), F32),
                        pltpu.VMEM((B_HEADS * Q_TILE, 128), F32)],
        compiler_params=_cparams(("arbitrary", "arbitrary")),
        name="dsa_prompt",
    )(p16, p16, p32, p16, p16, p16)


def _merge_kernel(sc_ref, x_ref, ya_ref, yb_ref, az_ref, bz_ref, w_ref, sg_ref, g_ref, b_ref, o_ref, *carry,
                  alpha, drop_rows):
    one_minus_lam_init = sc_ref[5:6, 0:1]
    ya = ya_ref[...]
    parts = []
    for h in range(A_HEADS):
        o = ya[:, 128 * h:128 * h + 128]
        y = o * lax.rsqrt(jnp.mean(o * o, axis=-1, keepdims=True) + LN_EPS) * sg_ref[...]
        parts.append(y * one_minus_lam_init)
    az = az_ref[...]
    bz = bz_ref[...]
    ya_g = jnp.concatenate(parts, axis=1) * (az * (1.0 / (1.0 + jnp.exp(-az))))
    yb_g = yb_ref[...] * (bz * (1.0 / (1.0 + jnp.exp(-bz))))
    mix = jnp.concatenate([ya_g, yb_g], axis=1).astype(BF16)
    y = alpha * x_ref[...] + jnp.dot(mix, w_ref[...], preferred_element_type=F32)
    mu = jnp.mean(y, axis=-1, keepdims=True)
    yc = y - mu
    var = jnp.mean(yc * yc, axis=-1, keepdims=True)
    out = yc * lax.rsqrt(var + LN_EPS) * g_ref[...] + b_ref[...]
    if drop_rows == 0:
        o_ref[...] = out
    else:
        (tail_ref,) = carry

        @pl.when(pl.program_id(1) > 0)
        def _():
            o_ref[...] = jnp.concatenate([tail_ref[...], out[0:drop_rows]], axis=0)

        tail_ref[...] = out[drop_rows:]


def _merge(x, ya, yb, p32, w_out, layer, sc, sub_g, ln_g, ln_b, alpha, tm, name, out_rows=None):
    nb, t, d = x.shape
    row = lambda wd, c: pl.BlockSpec((None, tm, wd), lambda b, i: (b, i, c))
    vec = lambda wd: pl.BlockSpec((1, wd), lambda b, i: (0, 0))
    if out_rows is None:
        out_spec, out_shape, scratch, drop = row(d, 0), (nb, t, d), [], 0
    else:
        last = out_rows // tm - 1
        out_spec = pl.BlockSpec((None, tm, d), lambda b, i: (b, jnp.clip(i - 1, 0, last), 0))
        out_shape, scratch, drop = (nb, out_rows, d), [pltpu.VMEM((tm - N_META, d), F32)], N_META
    return pl.pallas_call(
        functools.partial(_merge_kernel, alpha=alpha, drop_rows=drop),
        grid=(nb, t // tm),
        in_specs=[pl.BlockSpec((8, 128), lambda b, i: (0, 0)),
                  row(d, 0), row(512, 0), row(512, 0), row(512, C_AZ // 512), row(512, C_BZ // 512),
                  pl.BlockSpec((None, d, d), lambda b, i: (layer, 0, 0)),
                  vec(128), vec(d), vec(d)],
        out_specs=out_spec,
        out_shape=jax.ShapeDtypeStruct(out_shape, F32),
        scratch_shapes=scratch,
        compiler_params=_cparams(("arbitrary", "arbitrary")),
        name=name,
    )(sc, x, ya, yb, p32, p32, w_out, sub_g.reshape(1, 128), ln_g.reshape(1, d), ln_b.reshape(1, d))


def _pad_rows(x, rows):
    return jnp.concatenate([x, jnp.zeros((rows - x.shape[0], x.shape[1]), x.dtype)], axis=0)


def _diff_sample_kernel(pt_ref, lam_ref, q_ref, kn_ref, vn_ref, *rest, n_pg, past):
    k_refs, v_refs = rest[:n_pg], rest[n_pg:2 * n_pg]
    o_ref, m_ref, l_ref, acc_ref = rest[2 * n_pg:]
    pg = pl.program_id(1)
    ds = q_ref.shape[0]
    rows = 2 * A_KV_HEADS * 2 * ds
    span = n_pg * PAGE_SIZE

    @pl.when(pg == 0)
    def _():
        m_ref[...] = jnp.full_like(m_ref, NEG_INF)
        l_ref[...] = jnp.zeros_like(l_ref)
        acc_ref[...] = jnp.zeros_like(acc_ref)

    q = q_ref[...]
    qst = jnp.concatenate([_diff_query_rows(q[:, 256 * n:256 * n + 256], ds) for n in range(A_KV_HEADS)], axis=0)
    slope = _const_rows([2.0 ** (-2 * (2 * n + g + 1)) for n in range(A_KV_HEADS) for _c in range(2)
                         for g in range(2)], ds)
    half = rows // A_KV_HEADS

    def attend(score, weigh, width, k0, mask):
        col = lax.broadcasted_iota(I32, (1, width), 1)
        s = jnp.concatenate([score(n, qst[half * n:half * (n + 1)]) for n in range(A_KV_HEADS)],
                            axis=0) + slope * col.astype(F32)
        if mask is not None:
            s = jnp.where(mask(col), s, NEG_INF)
        cj = slope * k0
        m, l, acc = m_ref[...][:, 0:1], l_ref[...][:, 0:1], acc_ref[...]
        mt = jnp.max(s, axis=1, keepdims=True)
        m_new = jnp.maximum(m, mt + cj)
        p = jnp.exp(s - (m_new - cj))
        alpha = jnp.exp(m - m_new)
        l = alpha * l + jnp.sum(p, axis=1, keepdims=True)
        pv = jnp.concatenate([weigh(n, p[half * n:half * (n + 1)].astype(BF16)) for n in range(A_KV_HEADS)],
                             axis=0)
        m_ref[...] = jnp.broadcast_to(m_new, m_ref.shape)
        l_ref[...] = jnp.broadcast_to(l, l_ref.shape)
        acc_ref[...] = alpha * acc + pv

    def past_score(n, q):
        kt = jnp.concatenate([r[128 * n:128 * n + 128, :] for r in k_refs], axis=1).astype(BF16)
        return jnp.dot(q, kt, preferred_element_type=F32)

    def past_weigh(n, p):
        v = jnp.concatenate([r[pl.ds(n, PAGE_SIZE, stride=A_KV_HEADS), :] for r in v_refs], axis=0).astype(BF16)
        return jnp.dot(p, v, preferred_element_type=F32)

    attend(past_score, past_weigh, span, (pg * span).astype(F32), None)

    @pl.when(pg == past // span - 1)
    def _():
        r = lax.broadcasted_iota(I32, (rows, 1), 0) & (ds - 1)
        kn = _pad_rows(kn_ref[...], 128).astype(BF16)
        vn = _pad_rows(vn_ref[...], 128).astype(BF16)
        attend(lambda n, q: _dot_nt(q, kn[:, 128 * n:128 * n + 128]),
               lambda n, p: jnp.dot(p, vn[:, 128 * n:128 * n + 128], preferred_element_type=F32),
               128, jnp.float32(past), lambda col: col <= r)
        lam = _diff_lambda(lam_ref)
        o = acc_ref[...] / l_ref[...][:, 0:1]
        outs = []
        for n in range(A_KV_HEADS):
            base = half * n
            for g in range(2):
                outs.append(o[base + g * ds:base + (g + 1) * ds]
                            - lam * o[base + (2 + g) * ds:base + (3 + g) * ds])
        o_ref[...] = jnp.concatenate(outs, axis=1)


def _diff_sample(p16s, lam, cache_k, cache_v, pt_flat, n_pages):
    db, ds, _ = p16s.shape
    n_pg = PAGES_PER_STEP
    rows = 2 * A_KV_HEADS * 2 * ds
    past = n_pages * PAGE_SIZE
    kern = functools.partial(_diff_sample_kernel, n_pg=n_pg, past=past)

    def page(r):
        return pl.BlockSpec((None, 256, PAGE_SIZE), lambda sb, pg, pt: (pt[sb * n_pages + pg * n_pg + r], 0, 0))

    new = lambda c: pl.BlockSpec((None, ds, 256), lambda sb, pg, pt: (sb, 0, c // 256))
    grid_spec = pltpu.PrefetchScalarGridSpec(
        num_scalar_prefetch=1,
        grid=(db, n_pages // n_pg),
        in_specs=[pl.BlockSpec((8, 128), lambda sb, pg, pt: (0, 0)),
                  pl.BlockSpec((None, ds, 512), lambda sb, pg, pt: (sb, 0, C_AQ // 512)),
                  new(C_AK), new(C_AV)] + [page(r) for r in range(n_pg)] * 2,
        out_specs=pl.BlockSpec((None, ds, 512), lambda sb, pg, pt: (sb, 0, 0)),
        scratch_shapes=[pltpu.VMEM((rows, 128), F32), pltpu.VMEM((rows, 128), F32),
                        pltpu.VMEM((rows, 128), F32)])
    return pl.pallas_call(
        kern, grid_spec=grid_spec,
        out_shape=jax.ShapeDtypeStruct((db, ds, 512), F32),
        compiler_params=_cparams(("arbitrary", "arbitrary")),
        name="diff_sample",
    )(pt_flat, lam, p16s, p16s, p16s, *([cache_k] * n_pg), *([cache_v] * n_pg))


def _dsa_sample_index_kernel(pt_ref, iq_ref, iw_ref, in_ref, *rest, n_pg, past, topk, idx_bits):
    i_refs = rest[:n_pg]
    keys_ref, keys_new_ref, thr_ref, bound_ref = rest[n_pg:]
    sb, g = pl.program_id(0), pl.program_id(1)
    ds = iq_ref.shape[0]
    span = n_pg * PAGE_SIZE
    n_groups = past // span
    cw = keys_ref.shape[2]
    n_sub = span // cw
    all_rows = keys_ref.shape[1]
    row0 = pl.multiple_of(sb * ds, ds)
    r_id = lax.broadcasted_iota(I32, (ds, 1), 0)

    def index_keys(ik, transposed):
        iq = iq_ref[...]
        heads = []
        for h in range(IDX_HEADS):
            p = iq[:, 128 * (h // 2):128 * (h // 2) + 128]
            if h % 2:
                p = pltpu.roll(p, 64, 1)
            heads.append(p[:, 0:64])
        iqst = jnp.concatenate(heads, axis=0).astype(BF16)
        if transposed:
            logits = jnp.dot(iqst, ik.astype(BF16), preferred_element_type=F32)
        else:
            logits = _dot_nt(iqst, ik.astype(BF16))
        w = iw_ref[...]
        sc = jnp.maximum(logits[0:ds], 0.0) * w[:, 0:1]
        for h in range(1, IDX_HEADS):
            sc = sc + jnp.maximum(logits[h * ds:(h + 1) * ds], 0.0) * w[:, h:h + 1]
        return _sort_key(sc * IDX_SCALE)

    ik = jnp.concatenate([r[...] for r in i_refs], axis=1)
    keys = index_keys(ik, True)
    for c in range(n_sub):
        keys_ref[g * n_sub + c, pl.ds(row0, ds), :] = keys[:, c * cw:(c + 1) * cw]

    @pl.when(g == n_groups - 1)
    def _():
        col = lax.broadcasted_iota(I32, (1, 128), 1)
        key_new = index_keys(_pad_rows(in_ref[...][:, 0:64], 128), False)
        keys_new_ref[pl.ds(row0, ds), :] = jnp.where(col <= r_id, key_new, INT_MIN)

    @pl.when((sb == pl.num_programs(0) - 1) & (g == n_groups - 1))
    def _():
        col = lax.broadcasted_iota(I32, (1, cw), 1)
        col_new = lax.broadcasted_iota(I32, (1, 128), 1) + past

        def count(pred):
            def body(j, acc):
                return acc + _lane_fold(jnp.where(pred(keys_ref[j], col + j * cw), 1.0, 0.0), jnp.add)
            acc = lax.fori_loop(0, keys_ref.shape[0], body, jnp.zeros((all_rows, 128), F32))
            acc = acc + jnp.where(pred(keys_new_ref[...], col_new), 1.0, 0.0)
            return jnp.sum(acc, axis=1, keepdims=True)

        def bit_step(b, t):
            cand = t + lax.shift_left(jnp.int32(1), 31 - b)
            return jnp.where(count(lambda kc, _: kc >= cand) >= topk, cand, t)

        t = lax.fori_loop(0, 32, bit_step, jnp.full((all_rows, 1), INT_MIN, I32))
        thr = jnp.maximum(t, INT_MIN + 1)
        need = topk - count(lambda kc, _: kc > thr)
        excess = count(lambda kc, _: kc == thr) > need

        def tie_bound():
            def tie_step(b, jb):
                cand = jb + lax.shift_left(jnp.int32(1), idx_bits - 1 - b)
                return jnp.where(count(lambda kc, ix: (kc == thr) & (ix < cand)) < need, cand, jb)
            return lax.fori_loop(0, idx_bits, tie_step, jnp.zeros((all_rows, 1), I32))

        big = jnp.full((all_rows, 1), 2 ** 30, I32)
        any_excess = jnp.max(jnp.where(excess, 1.0, 0.0)) > 0.0
        bound = lax.cond(any_excess, lambda: jnp.where(excess, tie_bound(), big), lambda: big)
        thr_ref[...] = jnp.broadcast_to(thr, thr_ref.shape)
        bound_ref[...] = jnp.broadcast_to(bound, bound_ref.shape)


def _dsa_sample_attend_kernel(pt_ref, bq_ref, kn_ref, vn_ref, keys_ref, keys_new_ref, thr_ref, bound_ref, *rest,
                              n_pg, past):
    k_refs, v_refs = rest[:n_pg], rest[n_pg:2 * n_pg]
    o_ref, m_ref, l_ref, acc_ref = rest[2 * n_pg:]
    g = pl.program_id(1)
    ds = bq_ref.shape[0]
    rows = B_HEADS * ds
    span = n_pg * PAGE_SIZE
    n_groups = past // span
    qst = _dsa_query_rows(bq_ref[...])
    slope = _const_rows([2.0 ** -(h + 1) for h in range(B_HEADS)], ds)

    def attend(qk, weigh, kc, k0):
        width = qk.shape[1]
        col = lax.broadcasted_iota(I32, (1, width), 1)
        thr, bound = thr_ref[...][:, 0:1], bound_ref[...][:, 0:1]
        sel = (kc > thr) | ((kc == thr) & (col + k0 <= bound))
        s = (qk + slope * col.astype(F32)).reshape(B_HEADS, ds, width)
        s = jnp.where(sel[None], s, NEG_INF).reshape(rows, width)
        cj = slope * k0.astype(F32)
        m, l, acc = m_ref[...][:, 0:1], l_ref[...][:, 0:1], acc_ref[...]
        mt = jnp.max(s, axis=1, keepdims=True)
        m_new = jnp.maximum(m, mt + cj)
        m_safe = jnp.where(m_new == NEG_INF, 0.0, m_new)
        p = jnp.exp(s - (m_safe - cj))
        alpha = jnp.exp(m - m_safe)
        m_ref[...] = jnp.broadcast_to(m_new, m_ref.shape)
        l_ref[...] = jnp.broadcast_to(alpha * l + jnp.sum(p, axis=1, keepdims=True), l_ref.shape)
        acc_ref[...] = alpha * acc + weigh(p.astype(BF16))

    @pl.when(g == 0)
    def _():
        m_ref[...] = jnp.full_like(m_ref, NEG_INF)
        l_ref[...] = jnp.zeros_like(l_ref)
        acc_ref[...] = jnp.zeros_like(acc_ref)

    kt = jnp.concatenate([r[...] for r in k_refs], axis=1).astype(BF16)
    vt = jnp.concatenate([r[...] for r in v_refs], axis=1).astype(BF16)
    kc = jnp.concatenate([keys_ref[c] for c in range(keys_ref.shape[0])], axis=1)
    attend(jnp.dot(qst, kt, preferred_element_type=F32), lambda p: _dot_nt(p, vt), kc, g * span)

    @pl.when(g == n_groups - 1)
    def _():
        kn = _pad_rows(kn_ref[...], 128).astype(BF16)
        vn = _pad_rows(vn_ref[...], 128).astype(BF16)
        attend(_dot_nt(qst, kn), lambda p: jnp.dot(p, vn, preferred_element_type=F32),
               keys_new_ref[...], jnp.int32(past))
        o_ref[...] = _dsa_merge_heads(acc_ref[...] / l_ref[...][:, 0:1], ds)


def _dsa_sample(p16s, p32s, cache_i, cache_k, cache_v, pt_flat, n_pages, topk):
    db, ds, _ = p16s.shape
    n_pg = PAGES_PER_STEP
    n_groups = n_pages // n_pg
    past = n_pages * PAGE_SIZE
    rows = B_HEADS * ds
    cw = min(DSA_CHUNK, n_pg * PAGE_SIZE)
    n_sub = n_pg * PAGE_SIZE // cw

    def page(r, feat):
        return pl.BlockSpec((None, feat, PAGE_SIZE), lambda sb, g, pt: (pt[sb * n_pages + g * n_pg + r], 0, 0))

    new = lambda wd, c: pl.BlockSpec((None, ds, wd), lambda sb, g, pt: (sb, 0, c // wd))
    whole = lambda shape: pl.BlockSpec(shape, lambda sb, g, pt: (0,) * len(shape))

    key_shapes = [(past // cw, db * ds, cw), (db * ds, 128), (db * ds, 128), (db * ds, 128)]
    keys, keys_new, thr, bound = pl.pallas_call(
        functools.partial(_dsa_sample_index_kernel, n_pg=n_pg, past=past, topk=float(topk),
                          idx_bits=int(past + 128).bit_length()),
        grid_spec=pltpu.PrefetchScalarGridSpec(
            num_scalar_prefetch=1,
            grid=(db, n_groups),
            in_specs=[new(512, C_IQ), new(128, C_IW), new(128, C_IK)] + [page(r, 64) for r in range(n_pg)],
            out_specs=[whole(s) for s in key_shapes]),
        out_shape=[jax.ShapeDtypeStruct(s, I32) for s in key_shapes],
        compiler_params=_cparams(("arbitrary", "arbitrary")),
        name="dsa_sample_index",
    )(pt_flat, p16s, p32s, p16s, *([cache_i] * n_pg))

    per_seq = pl.BlockSpec((ds, 128), lambda sb, g, pt: (sb, 0))
    return pl.pallas_call(
        functools.partial(_dsa_sample_attend_kernel, n_pg=n_pg, past=past),
        grid_spec=pltpu.PrefetchScalarGridSpec(
            num_scalar_prefetch=1,
            grid=(db, n_groups),
            in_specs=[new(512, C_BQ), new(128, C_BK), new(128, C_BV),
                      pl.BlockSpec((n_sub, ds, cw), lambda sb, g, pt: (g, sb, 0)), per_seq, per_seq, per_seq]
                     + [page(r, 128) for r in range(n_pg)] * 2,
            out_specs=pl.BlockSpec((None, ds, 512), lambda sb, g, pt: (sb, 0, 0)),
            scratch_shapes=[pltpu.VMEM((rows, 128), F32), pltpu.VMEM((rows, 128), F32),
                            pltpu.VMEM((rows, 128), F32)]),
        out_shape=jax.ShapeDtypeStruct((db, ds, 512), F32),
        compiler_params=_cparams(("arbitrary", "arbitrary")),
        name="dsa_sample_attend",
    )(pt_flat, p16s, p16s, p16s, keys, keys_new, thr, bound, *([cache_k] * n_pg), *([cache_v] * n_pg))


_W_IN = dict(aq=(0, 512), ak=(512, 256), av=(768, 256), az=(1024, 512), bq=(1536, 512), bk=(2048, 128),
             bv=(2176, 128), iq=(2304, 512), ik=(2816, 64), iw=(2880, 8), bz=(2888, 512))
_W_ORDER = ((("aq",), HEAD_DIM ** -0.5), (("bq",), HEAD_DIM ** -0.5), (("iq",), 1.0), (("ak",), 1.0), (("av",), 1.0),
            (("bk",), 1.0), (("bv",), 1.0), (("ik", "ik"), 1.0), (("az",), 1.0), (("bz",), 1.0), (("iw",), 1.0))


def _regroup_kernel(wt_ref, o_ref):
    dst = 0
    for names, scale in _W_ORDER:
        rows = jnp.concatenate([wt_ref[_W_IN[n][0]:_W_IN[n][0] + _W_IN[n][1], :] for n in names], axis=0)
        width = -(-rows.shape[0] // 128) * 128
        if rows.shape[0] < width:
            rows = jnp.concatenate([rows, jnp.zeros((width - rows.shape[0], rows.shape[1]), F32)], axis=0)
        o_ref[:, dst:dst + width] = (rows * scale).T.astype(BF16)
        dst += width


def _regroup_weights(w_in):
    depth, d, n_in = w_in.shape
    return pl.pallas_call(
        _regroup_kernel,
        grid=(depth,),
        in_specs=[pl.BlockSpec((None, n_in, d), lambda l: (l, 0, 0))],
        out_specs=pl.BlockSpec((None, d, NCOL), lambda l: (l, 0, 0)),
        out_shape=jax.ShapeDtypeStruct((depth, d, NCOL), BF16),
        compiler_params=_cparams(("arbitrary",)),
        name="regroup_w_in",
    )(w_in.transpose(0, 2, 1))


def _pick_tile(n, candidates):
    for c in candidates:
        if n % c == 0:
            return c
    return n


def kernel(x_prompt, x_sample, cache_diff_k, cache_diff_v, cache_dsa_k, cache_dsa_v, cache_idx_k, page_table,
           meta_tokens, ln_in_g, ln_in_b, w_in, w_out, lambda_q1, lambda_k1, lambda_q2, lambda_k2,
           subln_g, ln_g, ln_b):
    nb, seq, d = x_prompt.shape
    db, ds, _ = x_sample.shape
    depth = w_in.shape[0]
    n_pool = cache_diff_k.shape[1]
    n_pages = page_table.shape[1]
    t_real = seq + N_META
    tpad = -(-t_real // DSA_CHUNK) * DSA_CHUNK
    assert ds == 8 and n_pages % PAGES_PER_STEP == 0 and d == 1024
    alpha = (2 * depth) ** 0.25
    topk_p = min(TOPK_MAX, seq // 4)
    topk_s = min(TOPK_MAX, (n_pages * PAGE_SIZE + ds) // 4)

    tm_ln = next((c for c in (512, 256, 128, 64, 32) if seq % c == 0 and tpad % c == 0), None)
    if tm_ln is not None:
        xp = _layernorm_prompt(x_prompt, meta_tokens.astype(x_prompt.dtype), ln_in_g, ln_in_b, tpad, tm_ln)
    else:
        meta = jnp.broadcast_to(meta_tokens[None].astype(x_prompt.dtype), (nb, N_META, d))
        xp = jnp.concatenate([meta, x_prompt, jnp.zeros((nb, tpad - t_real, d), x_prompt.dtype)], axis=1)
        xp = _layernorm_rows(xp.reshape(nb * tpad, d), ln_in_g, ln_in_b, KV_CHUNK).reshape(nb, tpad, d)
    xs = _layernorm_rows(x_sample.reshape(db * ds, d), ln_in_g, ln_in_b,
                         _pick_tile(db * ds, (256, 128, 8))).reshape(1, db * ds, d)

    w = _regroup_weights(w_in)
    w_o = w_out.astype(BF16)
    pad64 = lambda v: jnp.pad(v, ((0, 0), (0, 128 - v.shape[-1])))
    ck = cache_diff_k.transpose(0, 1, 3, 4, 5, 2).reshape(depth * n_pool, 256, PAGE_SIZE)
    cv = cache_diff_v.reshape(depth * n_pool, PAGE_SIZE * A_KV_HEADS, 128)
    cbk = cache_dsa_k.transpose(0, 1, 3, 4, 2).reshape(depth * n_pool, 128, PAGE_SIZE)
    cbv = cache_dsa_v.transpose(0, 1, 3, 4, 2).reshape(depth * n_pool, 128, PAGE_SIZE)
    ci = cache_idx_k.transpose(0, 1, 3, 2).reshape(depth * n_pool, IDX_DIM, PAGE_SIZE)
    pt_flat = page_table.reshape(-1).astype(I32)
    tm_p = _pick_tile(tpad, (512, 256, 128))
    tm_s = _pick_tile(db * ds, (256, 128, 8))

    rows_p = rows_s = None
    for layer in range(depth):
        lam_init = 0.8 - 0.6 * math.exp(-0.3 * layer)
        consts = jnp.stack([jnp.full((128,), lam_init, F32), jnp.full((128,), 1.0 - lam_init, F32),
                            jnp.zeros((128,), F32), jnp.zeros((128,), F32)])
        lam = jnp.concatenate([pad64(jnp.stack([lambda_q1[layer], lambda_k1[layer],
                                                 lambda_q2[layer], lambda_k2[layer]])), consts], axis=0)

        p16, p32, *rows_p = _project(xp, w, layer, depth, rows_p, t_real, tm_p, BF16, "proj_prompt")
        ya = _diff_prompt(p16, lam, t_real)
        yb = _dsa_prompt(p16, p32, t_real, topk_p)
        direct = layer == depth - 1 and seq % tm_p == 0
        xp = _merge(xp, ya, yb, p32, w_o, layer, lam, subln_g[layer], ln_g[layer], ln_b[layer],
                    alpha, tm_p, "merge_prompt", out_rows=seq if direct else None)

        p16s, p32s, *rows_s = _project(xs, w, layer, depth, rows_s, db * ds, tm_s, F32, "proj_sample")
        p16s3 = p16s.reshape(db, ds, W16)
        pt_layer = pt_flat + layer * n_pool
        ya_s = _diff_sample(p16s3, lam, ck, cv, pt_layer, n_pages)
        yb_s = _dsa_sample(p16s3, p32s.reshape(db, ds, W32), ci, cbk, cbv, pt_layer, n_pages, topk_s)
        xs = _merge(xs, ya_s.reshape(1, db * ds, 512), yb_s.reshape(1, db * ds, 512), p32s, w_o, layer, lam,
                    subln_g[layer], ln_g[layer], ln_b[layer], alpha, tm_s, "merge_sample")

    def cache_rows(stacks, lead, t):
        k, v, bk, bv, ik = stacks
        return (k.reshape(depth, -1, A_KV_HEADS, 2, HEAD_DIM, t).transpose(0, 1, 5, 2, 3, 4)
                 .reshape((depth,) + lead + (A_KV_HEADS, 2, HEAD_DIM)),
                v.reshape((depth,) + lead + (A_KV_HEADS, 2 * HEAD_DIM)),
                bk.reshape(depth, -1, B_KV_HEADS, HEAD_DIM, t).transpose(0, 1, 4, 2, 3)
                  .reshape((depth,) + lead + (B_KV_HEADS, HEAD_DIM)),
                bv.reshape(depth, -1, B_KV_HEADS, HEAD_DIM, t).transpose(0, 1, 4, 2, 3)
                  .reshape((depth,) + lead + (B_KV_HEADS, HEAD_DIM)),
                ik.transpose(0, 1, 3, 2).reshape((depth,) + lead + (IDX_DIM,)))

    y_prompt = xp if xp.shape[1] == seq else xp[:, N_META:t_real]
    return ((y_prompt, xs.reshape(db, ds, d))
            + cache_rows(rows_p, (nb, t_real), t_real) + cache_rows(rows_s, (db, ds), db * ds))
```

```python
import functools
import math

import jax
import jax.numpy as jnp
from jax import lax
from jax.experimental import pallas as pl
from jax.experimental.pallas import tpu as pltpu

N_META = 16
HEAD_DIM = 64
A_HEADS = 4
A_KV_HEADS = 2
B_HEADS = 8
B_KV_HEADS = 2
IDX_HEADS = 8
IDX_DIM = 64
IDX_SCALE = (IDX_HEADS * IDX_DIM) ** -0.5
TOPK_MAX = 256
PAGE_SIZE = 128
LN_EPS = 1e-5

Q_TILE = 128
DIFF_Q_TILE = 256
KV_CHUNK = 512
DSA_CHUNK = 512
PAGES_PER_STEP = 64
VMEM_LIMIT_BYTES = 56 * 1024 * 1024

LOG2E = math.log2(math.e)
INT_MIN = -(2 ** 31)
NEG_INF = float("-inf")

C_AQ, C_BQ, C_IQ, C_AK, C_AV, C_BK, C_BV, C_IK = 0, 512, 1024, 1536, 1792, 2048, 2176, 2304
W16 = 2432
C_AZ, C_BZ, C_IW = 0, 512, 1024
W32 = 1152
NCOL = W16 + W32

F32 = jnp.float32
BF16 = jnp.bfloat16
I32 = jnp.int32


def _dot_nt(a, b):
    return lax.dot_general(a, b, (((1,), (1,)), ((), ())), preferred_element_type=F32)


def _cparams(sem):
    return pltpu.CompilerParams(dimension_semantics=sem, vmem_limit_bytes=VMEM_LIMIT_BYTES)


def _sort_key(score):
    bits = lax.bitcast_convert_type(score, I32)
    return bits ^ ((bits >> 31) & 0x7FFFFFFF)


def _lane_lo(width=128):
    return lax.broadcasted_iota(I32, (1, width), 1) < 64


def _ln_kernel(x_ref, g_ref, b_ref, o_ref):
    x = x_ref[...]
    mu = jnp.mean(x, axis=-1, keepdims=True)
    xc = x - mu
    var = jnp.mean(xc * xc, axis=-1, keepdims=True)
    o_ref[...] = xc * lax.rsqrt(var + LN_EPS) * g_ref[...] + b_ref[...]


def _layernorm_rows(x, g, b, tm):
    rows, d = x.shape
    return pl.pallas_call(
        _ln_kernel,
        grid=(rows // tm,),
        in_specs=[pl.BlockSpec((tm, d), lambda i: (i, 0)),
                  pl.BlockSpec((1, d), lambda i: (0, 0)),
                  pl.BlockSpec((1, d), lambda i: (0, 0))],
        out_specs=pl.BlockSpec((tm, d), lambda i: (i, 0)),
        out_shape=jax.ShapeDtypeStruct((rows, d), F32),
        compiler_params=_cparams(("arbitrary",)),
        name="ln_in",
    )(x, g.reshape(1, d), b.reshape(1, d))


def _ln_prompt_kernel(meta_ref, prev_ref, cur_ref, g_ref, b_ref, o_ref, *, n_src_tiles):
    i = pl.program_id(1)
    top = jnp.where(i == 0, meta_ref[...], jnp.where(i <= n_src_tiles, prev_ref[...], 0.0))
    body = jnp.where(i < n_src_tiles, cur_ref[...][0:cur_ref.shape[0] - N_META], 0.0)
    x = jnp.concatenate([top, body], axis=0)
    mu = jnp.mean(x, axis=-1, keepdims=True)
    xc = x - mu
    var = jnp.mean(xc * xc, axis=-1, keepdims=True)
    o_ref[...] = xc * lax.rsqrt(var + LN_EPS) * g_ref[...] + b_ref[...]


def _layernorm_prompt(x_prompt, meta, g, b, tpad, tm):
    nb, seq, d = x_prompt.shape
    n_src = seq // tm
    per = tm // N_META
    return pl.pallas_call(
        functools.partial(_ln_prompt_kernel, n_src_tiles=n_src),
        grid=(nb, tpad // tm),
        in_specs=[pl.BlockSpec((N_META, d), lambda bi, i: (0, 0)),
                  pl.BlockSpec((None, N_META, d),
                               lambda bi, i: (bi, jnp.clip(i * per - 1, 0, seq // N_META - 1), 0)),
                  pl.BlockSpec((None, tm, d), lambda bi, i: (bi, jnp.minimum(i, n_src - 1), 0)),
                  pl.BlockSpec((1, d), lambda bi, i: (0, 0)),
                  pl.BlockSpec((1, d), lambda bi, i: (0, 0))],
        out_specs=pl.BlockSpec((None, tm, d), lambda bi, i: (bi, i, 0)),
        out_shape=jax.ShapeDtypeStruct((nb, tpad, d), F32),
        compiler_params=_cparams(("arbitrary", "arbitrary")),
        name="ln_in_prompt",
    )(meta, x_prompt, x_prompt, g.reshape(1, d), b.reshape(1, d))


def _proj_kernel(x_ref, w_ref, *rest, last_out_tile):
    p16_ref, p32_ref, ak_ref, av_ref, bk_ref, bv_ref, ik_ref = rest[-7:]
    xb = x_ref[...].astype(BF16)
    res = jnp.dot(xb, w_ref[...], preferred_element_type=F32)
    p16_ref[...] = res[:, :W16].astype(p16_ref.dtype)
    p32_ref[...] = res[:, W16:]

    @pl.when(pl.program_id(1) <= last_out_tile)
    def _():
        tm = res.shape[0]
        ak_ref[...] = res[:, C_AK:C_AK + 256].T
        for n in range(A_KV_HEADS):
            av_ref[pl.ds(n, tm, stride=A_KV_HEADS), :] = res[:, C_AV + 128 * n:C_AV + 128 * (n + 1)]
        bk_ref[...] = res[:, C_BK:C_BK + 128].T
        bv_ref[...] = res[:, C_BV:C_BV + 128].T
        ik_ref[...] = res[:, C_IK:C_IK + 128].T[0:IDX_DIM]


def _project(x, w, layer, depth, stacks, t_out, tm, p16_dtype, name):
    nb, t, d = x.shape
    last = pl.cdiv(t_out, tm) - 1
    row = lambda wd: pl.BlockSpec((None, tm, wd), lambda b, i: (b, i, 0))
    out = lambda ft: pl.BlockSpec((None, None, ft, tm), lambda b, i: (layer, b, 0, jnp.minimum(i, last)))
    shp = lambda ft: jax.ShapeDtypeStruct((depth, nb, ft, t_out), F32)
    out_v = pl.BlockSpec((None, None, A_KV_HEADS * tm, 128), lambda b, i: (layer, b, jnp.minimum(i, last), 0))
    shp_v = jax.ShapeDtypeStruct((depth, nb, A_KV_HEADS * t_out, 128), F32)
    stacks = list(stacks or ())
    return pl.pallas_call(
        functools.partial(_proj_kernel, last_out_tile=last),
        grid=(nb, t // tm),
        in_specs=[row(d), pl.BlockSpec((None, d, NCOL), lambda b, i: (layer, 0, 0))]
                 + [pl.BlockSpec(memory_space=pl.ANY)] * len(stacks),
        out_specs=[row(W16), row(W32), out(256), out_v, out(128), out(128), out(IDX_DIM)],
        out_shape=[jax.ShapeDtypeStruct((nb, t, W16), p16_dtype),
                   jax.ShapeDtypeStruct((nb, t, W32), F32),
                   shp(256), shp_v, shp(128), shp(128), shp(IDX_DIM)],
        input_output_aliases={2 + j: 2 + j for j in range(len(stacks))},
        compiler_params=_cparams(("arbitrary", "arbitrary")),
        name=name,
    )(x, w, *stacks)


def _diff_lambda(lam_ref):
    r = lam_ref[...]
    e1 = jnp.exp(jnp.sum(r[0:1] * r[1:2], axis=1, keepdims=True))
    e2 = jnp.exp(jnp.sum(r[2:3] * r[3:4], axis=1, keepdims=True))
    return e1 - e2 + r[4:5, 0:1]


def _half_select(x, keep_low):
    lo = _lane_lo()
    return jnp.where(lo, x, 0.0) if keep_low else jnp.where(lo, 0.0, x)


def _diff_query_rows(q, n_rows):
    pieces = [q[:, 0:128], q[:, 128:256]]
    return jnp.concatenate([_half_select(p, True) for p in pieces] +
                           [_half_select(p, False) for p in pieces], axis=0).astype(BF16)


def _dsa_query_rows(q):
    out = []
    for h in range(B_HEADS):
        n = h // (B_HEADS // B_KV_HEADS)
        p = q[:, 128 * (h // 2):128 * (h // 2) + 128]
        if h % 2 != n:
            p = pltpu.roll(p, 64, 1)
        out.append(_half_select(p, n == 0))
    return jnp.concatenate(out, axis=0).astype(BF16)


def _dsa_merge_heads(o, r):
    outs = []
    lo = _lane_lo()
    for j in range(B_HEADS // 2):
        n = j // 2
        oe, oo = o[2 * j * r:(2 * j + 1) * r], o[(2 * j + 1) * r:(2 * j + 2) * r]
        if n == 0:
            outs.append(jnp.where(lo, oe, pltpu.roll(oo, 64, 1)))
        else:
            outs.append(jnp.where(lo, pltpu.roll(oe, 64, 1), oo))
    return jnp.concatenate(outs, axis=1)


def _const_rows(values, r):
    return jnp.concatenate([jnp.full((r, 1), v, F32) for v in values], axis=0)


def _lane_fold(x, op):
    acc = x[:, 0:128]
    for c in range(1, x.shape[1] // 128):
        acc = op(acc, x[:, 128 * c:128 * (c + 1)])
    return acc


def _for_chunks_paired(n, body):
    def pair(jj, _):
        body(2 * jj, 0)
        body(2 * jj + 1, 1)
        return 0

    lax.fori_loop(0, n // 2, pair, 0)

    @pl.when(n % 2 == 1)
    def _():
        body(n - 1, 0)


def _store_scores(j, qk, shifts, tq, s_ref, m_ref):
    for r, shift in enumerate(shifts):
        rows = slice(r * tq, (r + 1) * tq)
        s = qk[rows] * LOG2E + shift
        s_ref[j, rows, :] = s
        m_ref[rows, :] = jnp.maximum(m_ref[rows, :], _lane_fold(s, jnp.maximum))


def _softmax_weigh(n_chunks, tc, rows, s_ref, v_ref, m_ref, acc_ref, lacc_ref):
    m = jnp.broadcast_to(jnp.max(m_ref[...], axis=1, keepdims=True), (rows, 128))
    acc_ref[...] = jnp.zeros_like(acc_ref)
    lacc_ref[...] = jnp.zeros_like(lacc_ref)

    def weigh(j):
        s = s_ref[j]
        p = jnp.concatenate([jnp.exp2(s[:, 128 * c:128 * (c + 1)] - m) for c in range(tc // 128)], axis=1)
        pv = jnp.dot(p.astype(BF16), v_ref[pl.ds(pl.multiple_of(j * tc, tc), tc), :], preferred_element_type=F32)
        return _lane_fold(p, jnp.add), pv

    def step(j, _slot):
        l, pv = weigh(j)
        lacc_ref[...] += l
        acc_ref[...] += pv

    _for_chunks_paired(n_chunks, step)
    return acc_ref[...] / jnp.sum(lacc_ref[...], axis=1, keepdims=True)


def _diff_prompt_kernel(lam_ref, q_ref, k_ref, v_ref, o_ref, s_ref, m_ref, acc_ref, lacc_ref, *, n_real_tiles):
    n = pl.program_id(1)
    i = pl.program_id(2)
    tq, tc = DIFF_Q_TILE, KV_CHUNK

    @pl.when(i >= n_real_tiles)
    def _():
        o_ref[...] = jnp.zeros_like(o_ref)

    @pl.when(i < n_real_tiles)
    def _():
        lam = _diff_lambda(lam_ref)
        qst = _diff_query_rows(q_ref[...].astype(F32), tq)
        slope_g = (jnp.where(n == 0, 2.0 ** -2, 2.0 ** -6), jnp.where(n == 0, 2.0 ** -4, 2.0 ** -8))
        qpos = i * tq + lax.broadcasted_iota(I32, (tq, 1), 0)
        col = lax.broadcasted_iota(I32, (1, tc), 1)
        m_ref[...] = jnp.full_like(m_ref, NEG_INF)

        def score_step(j, masked):
            k0 = pl.multiple_of(j * tc, tc)
            qk = _dot_nt(qst, k_ref[pl.ds(k0, tc), :])
            kpos = (col + k0).astype(F32)
            shifts = [(slope_g[r % 2] * LOG2E) * kpos for r in range(4)]
            if masked:
                causal = jnp.where(col + k0 <= qpos, 0.0, NEG_INF)
                shifts = [sh + causal for sh in shifts[:2]] * 2
            _store_scores(j, qk, shifts, tq, s_ref, m_ref)

        n_full = (i * tq) // tc
        _for_chunks_paired(n_full, lambda j, _slot: score_step(j, False))
        score_step(n_full, True)
        o = _softmax_weigh(n_full + 1, tc, 4 * tq, s_ref, v_ref, m_ref, acc_ref, lacc_ref)
        out_g0 = o[0:tq] - lam * o[2 * tq:3 * tq]
        out_g1 = o[tq:2 * tq] - lam * o[3 * tq:4 * tq]
        o_ref[...] = jnp.concatenate([out_g0, out_g1], axis=1)


def _diff_prompt(p16, lam, t_real):
    nb, tpad, _ = p16.shape
    tq = DIFF_Q_TILE
    kern = functools.partial(_diff_prompt_kernel, n_real_tiles=pl.cdiv(t_real, tq))
    return pl.pallas_call(
        kern,
        grid=(nb, A_KV_HEADS, tpad // tq),
        in_specs=[pl.BlockSpec((8, 128), lambda b, n, i: (0, 0)),
                  pl.BlockSpec((None, tq, 256), lambda b, n, i: (b, i, C_AQ // 256 + n)),
                  pl.BlockSpec((None, tpad, 128), lambda b, n, i: (b, 0, C_AK // 128 + n)),
                  pl.BlockSpec((None, tpad, 128), lambda b, n, i: (b, 0, C_AV // 128 + n))],
        out_specs=pl.BlockSpec((None, tq, 256), lambda b, n, i: (b, i, n)),
        out_shape=jax.ShapeDtypeStruct((nb, tpad, 512), F32),
        scratch_shapes=[pltpu.VMEM((tpad // KV_CHUNK, 4 * tq, KV_CHUNK), F32), pltpu.VMEM((4 * tq, 128), F32),
                        pltpu.VMEM((4 * tq, 128), F32), pltpu.VMEM((4 * tq, 128), F32)],
        compiler_params=_cparams(("arbitrary", "arbitrary", "arbitrary")),
        name="diff_prompt",
    )(lam, p16, p16, p16)


def _dsa_prompt_kernel(bq_ref, iq_ref, iw_ref, bk_ref, bv_ref, ik_ref, o_ref,
                       keys_ref, keys_t_ref, mask_ref, s_ref, m_ref, acc_ref, lacc_ref, *,
                       n_real_tiles, topk, idx_bits):
    i = pl.program_id(1)
    tq, tc = Q_TILE, DSA_CHUNK

    @pl.when(i >= n_real_tiles)
    def _():
        o_ref[...] = jnp.zeros_like(o_ref)

    @pl.when(i < n_real_tiles)
    def _():
        n_chunks = (i * tq) // tc + 1
        col = lax.broadcasted_iota(I32, (1, tc), 1)
        qpos = i * tq + lax.broadcasted_iota(I32, (tq, 1), 0)

        iq = iq_ref[...].astype(F32)
        iqst = jnp.concatenate(
            [_half_select(iq[:, 128 * (h // 2):128 * (h // 2) + 128], h % 2 == 0)
             for h in range(IDX_HEADS)], axis=0).astype(BF16)
        w = iw_ref[...]
        wcol = [jnp.broadcast_to(w[:, h:h + 1], (tq, 128)) for h in range(IDX_HEADS)]

        def index_step(j, _slot):
            k0 = pl.multiple_of(j * tc, tc)
            logits = _dot_nt(iqst, ik_ref[pl.ds(k0, tc), :])
            blocks = []
            for c in range(tc // 128):
                cols = slice(128 * c, 128 * (c + 1))
                acc = jnp.maximum(logits[0:tq, cols], 0.0) * wcol[0]
                for h in range(1, IDX_HEADS):
                    acc = acc + jnp.maximum(logits[h * tq:(h + 1) * tq, cols], 0.0) * wcol[h]
                blocks.append(acc)
            sc = jnp.concatenate(blocks, axis=1)
            key = jnp.where(col + k0 <= qpos, _sort_key(sc * IDX_SCALE), INT_MIN)
            keys_ref[j] = key
            keys_t_ref[j] = key.T

        _for_chunks_paired(n_chunks, index_step)

        kidx = lax.broadcasted_iota(I32, (tc, 1), 0)

        def count_over_chunks(pred):
            def body(j, acc):
                hit = jnp.where(pred(keys_t_ref[j], kidx + j * tc), 1.0, 0.0)
                parts = [hit[8 * r:8 * (r + 1)] for r in range(tc // 8)]
                while len(parts) > 1:
                    parts = [parts[r] + parts[r + 1] for r in range(0, len(parts), 2)]
                return acc + parts[0]
            acc = lax.fori_loop(0, n_chunks, body, jnp.zeros((8, tq), F32))
            return jnp.sum(acc, axis=0, keepdims=True)

        def bit_step(b, carry):
            t, n_ge = carry
            cand = t + lax.shift_left(jnp.int32(1), 31 - b)
            cnt = count_over_chunks(lambda kc, _: kc >= cand)
            return jnp.where(cnt >= topk, cand, t), jnp.where(cnt >= topk, cnt, n_ge)

        t, n_ge = lax.fori_loop(0, 32, bit_step, (jnp.full((1, tq), INT_MIN, I32), jnp.zeros((1, tq), F32)))
        thr_t = jnp.maximum(t, INT_MIN + 1)
        n_gt = count_over_chunks(lambda kc, _: kc > thr_t)
        need = topk - n_gt
        n_eq = jnp.where(t == INT_MIN, 0.0, n_ge - n_gt)
        excess = n_eq > need

        def tie_bound():
            def tie_step(b, jb):
                cand = jb + lax.shift_left(jnp.int32(1), idx_bits - 1 - b)
                cnt = count_over_chunks(lambda kc, idx: (kc == thr_t) & (idx < cand))
                return jnp.where(cnt < need, cand, jb)
            return lax.fori_loop(0, idx_bits, tie_step, jnp.zeros((1, tq), I32))

        any_excess = jnp.max(jnp.where(excess, 1.0, 0.0)) > 0.0
        big = jnp.full((1, tq), 2 ** 30, I32)
        bound_t = lax.cond(any_excess, lambda: jnp.where(excess, tie_bound(), big), lambda: big)
        to_col = lambda v: jnp.broadcast_to(v, (tq, tq)).T[:, 0:1]
        thr, bound = to_col(thr_t), to_col(bound_t)

        qst = _dsa_query_rows(bq_ref[...].astype(F32))

        m_ref[...] = jnp.full_like(m_ref, NEG_INF)

        def score_step(j, slot):
            k0 = pl.multiple_of(j * tc, tc)
            kc = keys_ref[j]
            sel = (kc > thr) | ((kc == thr) & (col + k0 <= bound))
            mask_ref[slot] = jnp.where(sel, 0.0, NEG_INF)
            qk = _dot_nt(qst, bk_ref[pl.ds(k0, tc), :])
            kpos = (col + k0).astype(F32)
            _store_scores(j, qk, [(2.0 ** -(h + 1) * LOG2E) * kpos + mask_ref[slot] for h in range(B_HEADS)],
                          tq, s_ref, m_ref)

        _for_chunks_paired(n_chunks, score_step)
        o = _softmax_weigh(n_chunks, tc, B_HEADS * tq, s_ref, bv_ref, m_ref, acc_ref, lacc_ref)
        o_ref[...] = _dsa_merge_heads(o, tq)


def _dsa_prompt(p16, p32, t_real, topk):
    nb, tpad, _ = p16.shape
    nq = tpad // Q_TILE
    kern = functools.partial(_dsa_prompt_kernel, n_real_tiles=pl.cdiv(t_real, Q_TILE), topk=float(topk),
                             idx_bits=max(1, int(tpad).bit_length()))
    whole = lambda c: pl.BlockSpec((None, tpad, 128), lambda b, i: (b, 0, c // 128))
    return pl.pallas_call(
        kern,
        grid=(nb, nq),
        in_specs=[pl.BlockSpec((None, Q_TILE, 512), lambda b, i: (b, i, C_BQ // 512)),
                  pl.BlockSpec((None, Q_TILE, 512), lambda b, i: (b, i, C_IQ // 512)),
                  pl.BlockSpec((None, Q_TILE, 128), lambda b, i: (b, i, C_IW // 128)),
                  whole(C_BK), whole(C_BV), whole(C_IK)],
        out_specs=pl.BlockSpec((None, Q_TILE, 512), lambda b, i: (b, i, 0)),
        out_shape=jax.ShapeDtypeStruct((nb, tpad, 512), F32),
        scratch_shapes=[pltpu.VMEM((tpad // DSA_CHUNK, Q_TILE, DSA_CHUNK), I32),
                        pltpu.VMEM((tpad // DSA_CHUNK, DSA_CHUNK, Q_TILE), I32),
                        pltpu.VMEM((2, Q_TILE, DSA_CHUNK), F32),
                        pltpu.VMEM((tpad // DSA_CHUNK, B_HEADS * Q_TILE, DSA_CHUNK), F32),
                        pltpu.VMEM((B_HEADS * Q_TILE, 128), F32), pltpu.VMEM((B_HEADS * Q_TILE, 128), F32),
                        pltpu.VMEM((B_HEADS * Q_TILE, 128), F32)],
        compiler_params=_cparams(("arbitrary", "arbitrary")),
        name="dsa_prompt",
    )(p16, p16, p32, p16, p16, p16)


def _merge_kernel(sc_ref, x_ref, ya_ref, yb_ref, az_ref, bz_ref, w_ref, sg_ref, g_ref, b_ref, o_ref, *carry,
                  alpha, drop_rows):
    one_minus_lam_init = sc_ref[5:6, 0:1]
    ya = ya_ref[...]
    parts = []
    for h in range(A_HEADS):
        o = ya[:, 128 * h:128 * h + 128]
        y = o * lax.rsqrt(jnp.mean(o * o, axis=-1, keepdims=True) + LN_EPS) * sg_ref[...]
        parts.append(y * one_minus_lam_init)
    az = az_ref[...]
    bz = bz_ref[...]
    ya_g = jnp.concatenate(parts, axis=1) * (az * (1.0 / (1.0 + jnp.exp(-az))))
    yb_g = yb_ref[...] * (bz * (1.0 / (1.0 + jnp.exp(-bz))))
    mix = jnp.concatenate([ya_g, yb_g], axis=1).astype(BF16)
    y = alpha * x_ref[...] + jnp.dot(mix, w_ref[...], preferred_element_type=F32)
    mu = jnp.mean(y, axis=-1, keepdims=True)
    yc = y - mu
    var = jnp.mean(yc * yc, axis=-1, keepdims=True)
    out = yc * lax.rsqrt(var + LN_EPS) * g_ref[...] + b_ref[...]
    if drop_rows == 0:
        o_ref[...] = out
    else:
        (tail_ref,) = carry

        @pl.when(pl.program_id(1) > 0)
        def _():
            o_ref[...] = jnp.concatenate([tail_ref[...], out[0:drop_rows]], axis=0)

        tail_ref[...] = out[drop_rows:]


def _merge(x, ya, yb, p32, w_out, layer, sc, sub_g, ln_g, ln_b, alpha, tm, name, out_rows=None):
    nb, t, d = x.shape
    row = lambda wd, c: pl.BlockSpec((None, tm, wd), lambda b, i: (b, i, c))
    vec = lambda wd: pl.BlockSpec((1, wd), lambda b, i: (0, 0))
    if out_rows is None:
        out_spec, out_shape, scratch, drop = row(d, 0), (nb, t, d), [], 0
    else:
        last = out_rows // tm - 1
        out_spec = pl.BlockSpec((None, tm, d), lambda b, i: (b, jnp.clip(i - 1, 0, last), 0))
        out_shape, scratch, drop = (nb, out_rows, d), [pltpu.VMEM((tm - N_META, d), F32)], N_META
    return pl.pallas_call(
        functools.partial(_merge_kernel, alpha=alpha, drop_rows=drop),
        grid=(nb, t // tm),
        in_specs=[pl.BlockSpec((8, 128), lambda b, i: (0, 0)),
                  row(d, 0), row(512, 0), row(512, 0), row(512, C_AZ // 512), row(512, C_BZ // 512),
                  pl.BlockSpec((None, d, d), lambda b, i: (layer, 0, 0)),
                  vec(128), vec(d), vec(d)],
        out_specs=out_spec,
        out_shape=jax.ShapeDtypeStruct(out_shape, F32),
        scratch_shapes=scratch,
        compiler_params=_cparams(("arbitrary", "arbitrary")),
        name=name,
    )(sc, x, ya, yb, p32, p32, w_out, sub_g.reshape(1, 128), ln_g.reshape(1, d), ln_b.reshape(1, d))


def _pad_rows(x, rows):
    return jnp.concatenate([x, jnp.zeros((rows - x.shape[0], x.shape[1]), x.dtype)], axis=0)


def _diff_sample_kernel(pt_ref, lam_ref, q_ref, kn_ref, vn_ref, *rest, n_pg, past):
    k_refs, v_refs = rest[:n_pg], rest[n_pg:2 * n_pg]
    o_ref, m_ref, l_ref, acc_ref = rest[2 * n_pg:]
    pg = pl.program_id(1)
    ds = q_ref.shape[0]
    rows = 2 * A_KV_HEADS * 2 * ds
    span = n_pg * PAGE_SIZE

    @pl.when(pg == 0)
    def _():
        m_ref[...] = jnp.full_like(m_ref, NEG_INF)
        l_ref[...] = jnp.zeros_like(l_ref)
        acc_ref[...] = jnp.zeros_like(acc_ref)

    q = q_ref[...]
    qst = jnp.concatenate([_diff_query_rows(q[:, 256 * n:256 * n + 256], ds) for n in range(A_KV_HEADS)], axis=0)
    slope = _const_rows([2.0 ** (-2 * (2 * n + g + 1)) for n in range(A_KV_HEADS) for _c in range(2)
                         for g in range(2)], ds)
    half = rows // A_KV_HEADS

    def attend(score, weigh, width, k0, mask):
        col = lax.broadcasted_iota(I32, (1, width), 1)
        s = jnp.concatenate([score(n, qst[half * n:half * (n + 1)]) for n in range(A_KV_HEADS)],
                            axis=0) + slope * col.astype(F32)
        if mask is not None:
            s = jnp.where(mask(col), s, NEG_INF)
        cj = slope * k0
        m, l, acc = m_ref[...][:, 0:1], l_ref[...][:, 0:1], acc_ref[...]
        mt = jnp.max(s, axis=1, keepdims=True)
        m_new = jnp.maximum(m, mt + cj)
        p = jnp.exp(s - (m_new - cj))
        alpha = jnp.exp(m - m_new)
        l = alpha * l + jnp.sum(p, axis=1, keepdims=True)
        pv = jnp.concatenate([weigh(n, p[half * n:half * (n + 1)].astype(BF16)) for n in range(A_KV_HEADS)],
                             axis=0)
        m_ref[...] = jnp.broadcast_to(m_new, m_ref.shape)
        l_ref[...] = jnp.broadcast_to(l, l_ref.shape)
        acc_ref[...] = alpha * acc + pv

    def past_score(n, q):
        kt = jnp.concatenate([r[128 * n:128 * n + 128, :] for r in k_refs], axis=1).astype(BF16)
        return jnp.dot(q, kt, preferred_element_type=F32)

    def past_weigh(n, p):
        v = jnp.concatenate([r[pl.ds(n, PAGE_SIZE, stride=A_KV_HEADS), :] for r in v_refs], axis=0).astype(BF16)
        return jnp.dot(p, v, preferred_element_type=F32)

    attend(past_score, past_weigh, span, (pg * span).astype(F32), None)

    @pl.when(pg == past // span - 1)
    def _():
        r = lax.broadcasted_iota(I32, (rows, 1), 0) & (ds - 1)
        kn = _pad_rows(kn_ref[...], 128).astype(BF16)
        vn = _pad_rows(vn_ref[...], 128).astype(BF16)
        attend(lambda n, q: _dot_nt(q, kn[:, 128 * n:128 * n + 128]),
               lambda n, p: jnp.dot(p, vn[:, 128 * n:128 * n + 128], preferred_element_type=F32),
               128, jnp.float32(past), lambda col: col <= r)
        lam = _diff_lambda(lam_ref)
        o = acc_ref[...] / l_ref[...][:, 0:1]
        outs = []
        for n in range(A_KV_HEADS):
            base = half * n
            for g in range(2):
                outs.append(o[base + g * ds:base + (g + 1) * ds]
                            - lam * o[base + (2 + g) * ds:base + (3 + g) * ds])
        o_ref[...] = jnp.concatenate(outs, axis=1)


def _diff_sample(p16s, lam, cache_k, cache_v, pt_flat, n_pages):
    db, ds, _ = p16s.shape
    n_pg = PAGES_PER_STEP
    rows = 2 * A_KV_HEADS * 2 * ds
    past = n_pages * PAGE_SIZE
    kern = functools.partial(_diff_sample_kernel, n_pg=n_pg, past=past)

    def page(r):
        return pl.BlockSpec((None, 256, PAGE_SIZE), lambda sb, pg, pt: (pt[sb * n_pages + pg * n_pg + r], 0, 0))

    new = lambda c: pl.BlockSpec((None, ds, 256), lambda sb, pg, pt: (sb, 0, c // 256))
    grid_spec = pltpu.PrefetchScalarGridSpec(
        num_scalar_prefetch=1,
        grid=(db, n_pages // n_pg),
        in_specs=[pl.BlockSpec((8, 128), lambda sb, pg, pt: (0, 0)),
                  pl.BlockSpec((None, ds, 512), lambda sb, pg, pt: (sb, 0, C_AQ // 512)),
                  new(C_AK), new(C_AV)] + [page(r) for r in range(n_pg)] * 2,
        out_specs=pl.BlockSpec((None, ds, 512), lambda sb, pg, pt: (sb, 0, 0)),
        scratch_shapes=[pltpu.VMEM((rows, 128), F32), pltpu.VMEM((rows, 128), F32),
                        pltpu.VMEM((rows, 128), F32)])
    return pl.pallas_call(
        kern, grid_spec=grid_spec,
        out_shape=jax.ShapeDtypeStruct((db, ds, 512), F32),
        compiler_params=_cparams(("arbitrary", "arbitrary")),
        name="diff_sample",
    )(pt_flat, lam, p16s, p16s, p16s, *([cache_k] * n_pg), *([cache_v] * n_pg))


def _dsa_sample_index_kernel(pt_ref, iq_ref, iw_ref, in_ref, *rest, n_pg, past, topk, idx_bits):
    i_refs = rest[:n_pg]
    keys_ref, keys_new_ref, thr_ref, bound_ref = rest[n_pg:]
    sb, g = pl.program_id(0), pl.program_id(1)
    ds = iq_ref.shape[0]
    span = n_pg * PAGE_SIZE
    n_groups = past // span
    cw = keys_ref.shape[2]
    n_sub = span // cw
    all_rows = keys_ref.shape[1]
    row0 = pl.multiple_of(sb * ds, ds)
    r_id = lax.broadcasted_iota(I32, (ds, 1), 0)

    def index_keys(ik, transposed):
        iq = iq_ref[...]
        heads = []
        for h in range(IDX_HEADS):
            p = iq[:, 128 * (h // 2):128 * (h // 2) + 128]
            if h % 2:
                p = pltpu.roll(p, 64, 1)
            heads.append(p[:, 0:64])
        iqst = jnp.concatenate(heads, axis=0).astype(BF16)
        if transposed:
            logits = jnp.dot(iqst, ik.astype(BF16), preferred_element_type=F32)
        else:
            logits = _dot_nt(iqst, ik.astype(BF16))
        w = iw_ref[...]
        sc = jnp.maximum(logits[0:ds], 0.0) * w[:, 0:1]
        for h in range(1, IDX_HEADS):
            sc = sc + jnp.maximum(logits[h * ds:(h + 1) * ds], 0.0) * w[:, h:h + 1]
        return _sort_key(sc * IDX_SCALE)

    ik = jnp.concatenate([r[...] for r in i_refs], axis=1)
    keys = index_keys(ik, True)
    for c in range(n_sub):
        keys_ref[g * n_sub + c, pl.ds(row0, ds), :] = keys[:, c * cw:(c + 1) * cw]

    @pl.when(g == n_groups - 1)
    def _():
        col = lax.broadcasted_iota(I32, (1, 128), 1)
        key_new = index_keys(_pad_rows(in_ref[...][:, 0:64], 128), False)
        keys_new_ref[pl.ds(row0, ds), :] = jnp.where(col <= r_id, key_new, INT_MIN)

    @pl.when((sb == pl.num_programs(0) - 1) & (g == n_groups - 1))
    def _():
        col = lax.broadcasted_iota(I32, (1, cw), 1)
        col_new = lax.broadcasted_iota(I32, (1, 128), 1) + past

        def count(pred):
            def body(j, acc):
                return acc + _lane_fold(jnp.where(pred(keys_ref[j], col + j * cw), 1.0, 0.0), jnp.add)
            acc = lax.fori_loop(0, keys_ref.shape[0], body, jnp.zeros((all_rows, 128), F32))
            acc = acc + jnp.where(pred(keys_new_ref[...], col_new), 1.0, 0.0)
            return jnp.sum(acc, axis=1, keepdims=True)

        def bit_step(b, t):
            cand = t + lax.shift_left(jnp.int32(1), 31 - b)
            return jnp.where(count(lambda kc, _: kc >= cand) >= topk, cand, t)

        t = lax.fori_loop(0, 32, bit_step, jnp.full((all_rows, 1), INT_MIN, I32))
        thr = jnp.maximum(t, INT_MIN + 1)
        need = topk - count(lambda kc, _: kc > thr)
        excess = count(lambda kc, _: kc == thr) > need

        def tie_bound():
            def tie_step(b, jb):
                cand = jb + lax.shift_left(jnp.int32(1), idx_bits - 1 - b)
                return jnp.where(count(lambda kc, ix: (kc == thr) & (ix < cand)) < need, cand, jb)
            return lax.fori_loop(0, idx_bits, tie_step, jnp.zeros((all_rows, 1), I32))

        big = jnp.full((all_rows, 1), 2 ** 30, I32)
        any_excess = jnp.max(jnp.where(excess, 1.0, 0.0)) > 0.0
        bound = lax.cond(any_excess, lambda: jnp.where(excess, tie_bound(), big), lambda: big)
        thr_ref[...] = jnp.broadcast_to(thr, thr_ref.shape)
        bound_ref[...] = jnp.broadcast_to(bound, bound_ref.shape)


def _dsa_sample_attend_kernel(pt_ref, bq_ref, kn_ref, vn_ref, keys_ref, keys_new_ref, thr_ref, bound_ref, *rest,
                              n_pg, past):
    k_refs, v_refs = rest[:n_pg], rest[n_pg:2 * n_pg]
    o_ref, m_ref, l_ref, acc_ref = rest[2 * n_pg:]
    g = pl.program_id(1)
    ds = bq_ref.shape[0]
    rows = B_HEADS * ds
    span = n_pg * PAGE_SIZE
    n_groups = past // span
    qst = _dsa_query_rows(bq_ref[...])
    slope = _const_rows([2.0 ** -(h + 1) for h in range(B_HEADS)], ds)

    def attend(qk, weigh, kc, k0):
        width = qk.shape[1]
        col = lax.broadcasted_iota(I32, (1, width), 1)
        thr, bound = thr_ref[...][:, 0:1], bound_ref[...][:, 0:1]
        sel = (kc > thr) | ((kc == thr) & (col + k0 <= bound))
        s = (qk + slope * col.astype(F32)).reshape(B_HEADS, ds, width)
        s = jnp.where(sel[None], s, NEG_INF).reshape(rows, width)
        cj = slope * k0.astype(F32)
        m, l, acc = m_ref[...][:, 0:1], l_ref[...][:, 0:1], acc_ref[...]
        mt = jnp.max(s, axis=1, keepdims=True)
        m_new = jnp.maximum(m, mt + cj)
        m_safe = jnp.where(m_new == NEG_INF, 0.0, m_new)
        p = jnp.exp(s - (m_safe - cj))
        alpha = jnp.exp(m - m_safe)
        m_ref[...] = jnp.broadcast_to(m_new, m_ref.shape)
        l_ref[...] = jnp.broadcast_to(alpha * l + jnp.sum(p, axis=1, keepdims=True), l_ref.shape)
        acc_ref[...] = alpha * acc + weigh(p.astype(BF16))

    @pl.when(g == 0)
    def _():
        m_ref[...] = jnp.full_like(m_ref, NEG_INF)
        l_ref[...] = jnp.zeros_like(l_ref)
        acc_ref[...] = jnp.zeros_like(acc_ref)

    kt = jnp.concatenate([r[...] for r in k_refs], axis=1).astype(BF16)
    vt = jnp.concatenate([r[...] for r in v_refs], axis=1).astype(BF16)
    kc = jnp.concatenate([keys_ref[c] for c in range(keys_ref.shape[0])], axis=1)
    attend(jnp.dot(qst, kt, preferred_element_type=F32), lambda p: _dot_nt(p, vt), kc, g * span)

    @pl.when(g == n_groups - 1)
    def _():
        kn = _pad_rows(kn_ref[...], 128).astype(BF16)
        vn = _pad_rows(vn_ref[...], 128).astype(BF16)
        attend(_dot_nt(qst, kn), lambda p: jnp.dot(p, vn, preferred_element_type=F32),
               keys_new_ref[...], jnp.int32(past))
        o_ref[...] = _dsa_merge_heads(acc_ref[...] / l_ref[...][:, 0:1], ds)


def _dsa_sample(p16s, p32s, cache_i, cache_k, cache_v, pt_flat, n_pages, topk):
    db, ds, _ = p16s.shape
    n_pg = PAGES_PER_STEP
    n_groups = n_pages // n_pg
    past = n_pages * PAGE_SIZE
    rows = B_HEADS * ds
    cw = min(DSA_CHUNK, n_pg * PAGE_SIZE)
    n_sub = n_pg * PAGE_SIZE // cw

    def page(r, feat):
        return pl.BlockSpec((None, feat, PAGE_SIZE), lambda sb, g, pt: (pt[sb * n_pages + g * n_pg + r], 0, 0))

    new = lambda wd, c: pl.BlockSpec((None, ds, wd), lambda sb, g, pt: (sb, 0, c // wd))
    whole = lambda shape: pl.BlockSpec(shape, lambda sb, g, pt: (0,) * len(shape))

    key_shapes = [(past // cw, db * ds, cw), (db * ds, 128), (db * ds, 128), (db * ds, 128)]
    keys, keys_new, thr, bound = pl.pallas_call(
        functools.partial(_dsa_sample_index_kernel, n_pg=n_pg, past=past, topk=float(topk),
                          idx_bits=int(past + 128).bit_length()),
        grid_spec=pltpu.PrefetchScalarGridSpec(
            num_scalar_prefetch=1,
            grid=(db, n_groups),
            in_specs=[new(512, C_IQ), new(128, C_IW), new(128, C_IK)] + [page(r, 64) for r in range(n_pg)],
            out_specs=[whole(s) for s in key_shapes]),
        out_shape=[jax.ShapeDtypeStruct(s, I32) for s in key_shapes],
        compiler_params=_cparams(("arbitrary", "arbitrary")),
        name="dsa_sample_index",
    )(pt_flat, p16s, p32s, p16s, *([cache_i] * n_pg))

    per_seq = pl.BlockSpec((ds, 128), lambda sb, g, pt: (sb, 0))
    return pl.pallas_call(
        functools.partial(_dsa_sample_attend_kernel, n_pg=n_pg, past=past),
        grid_spec=pltpu.PrefetchScalarGridSpec(
            num_scalar_prefetch=1,
            grid=(db, n_groups),
            in_specs=[new(512, C_BQ), new(128, C_BK), new(128, C_BV),
                      pl.BlockSpec((n_sub, ds, cw), lambda sb, g, pt: (g, sb, 0)), per_seq, per_seq, per_seq]
                     + [page(r, 128) for r in range(n_pg)] * 2,
            out_specs=pl.BlockSpec((None, ds, 512), lambda sb, g, pt: (sb, 0, 0)),
            scratch_shapes=[pltpu.VMEM((rows, 128), F32), pltpu.VMEM((rows, 128), F32),
                            pltpu.VMEM((rows, 128), F32)]),
        out_shape=jax.ShapeDtypeStruct((db, ds, 512), F32),
        compiler_params=_cparams(("arbitrary", "arbitrary")),
        name="dsa_sample_attend",
    )(pt_flat, p16s, p16s, p16s, keys, keys_new, thr, bound, *([cache_k] * n_pg), *([cache_v] * n_pg))


_W_IN = dict(aq=(0, 512), ak=(512, 256), av=(768, 256), az=(1024, 512), bq=(1536, 512), bk=(2048, 128),
             bv=(2176, 128), iq=(2304, 512), ik=(2816, 64), iw=(2880, 8), bz=(2888, 512))
_W_ORDER = ((("aq",), HEAD_DIM ** -0.5), (("bq",), HEAD_DIM ** -0.5), (("iq",), 1.0), (("ak",), 1.0), (("av",), 1.0),
            (("bk",), 1.0), (("bv",), 1.0), (("ik", "ik"), 1.0), (("az",), 1.0), (("bz",), 1.0), (("iw",), 1.0))


def _regroup_kernel(wt_ref, o_ref):
    dst = 0
    for names, scale in _W_ORDER:
        rows = jnp.concatenate([wt_ref[_W_IN[n][0]:_W_IN[n][0] + _W_IN[n][1], :] for n in names], axis=0)
        width = -(-rows.shape[0] // 128) * 128
        if rows.shape[0] < width:
            rows = jnp.concatenate([rows, jnp.zeros((width - rows.shape[0], rows.shape[1]), F32)], axis=0)
        o_ref[:, dst:dst + width] = (rows * scale).T.astype(BF16)
        dst += width


def _regroup_weights(w_in):
    depth, d, n_in = w_in.shape
    return pl.pallas_call(
        _regroup_kernel,
        grid=(depth,),
        in_specs=[pl.BlockSpec((None, n_in, d), lambda l: (l, 0, 0))],
        out_specs=pl.BlockSpec((None, d, NCOL), lambda l: (l, 0, 0)),
        out_shape=jax.ShapeDtypeStruct((depth, d, NCOL), BF16),
        compiler_params=_cparams(("arbitrary",)),
        name="regroup_w_in",
    )(w_in.transpose(0, 2, 1))


def _pick_tile(n, candidates):
    for c in candidates:
        if n % c == 0:
            return c
    return n


def kernel(x_prompt, x_sample, cache_diff_k, cache_diff_v, cache_dsa_k, cache_dsa_v, cache_idx_k, page_table,
           meta_tokens, ln_in_g, ln_in_b, w_in, w_out, lambda_q1, lambda_k1, lambda_q2, lambda_k2,
           subln_g, ln_g, ln_b):
    nb, seq, d = x_prompt.shape
    db, ds, _ = x_sample.shape
    depth = w_in.shape[0]
    n_pool = cache_diff_k.shape[1]
    n_pages = page_table.shape[1]
    t_real = seq + N_META
    tpad = -(-t_real // DSA_CHUNK) * DSA_CHUNK
    assert ds == 8 and n_pages % PAGES_PER_STEP == 0 and d == 1024
    alpha = (2 * depth) ** 0.25
    topk_p = min(TOPK_MAX, seq // 4)
    topk_s = min(TOPK_MAX, (n_pages * PAGE_SIZE + ds) // 4)

    tm_ln = next((c for c in (512, 256, 128, 64, 32) if seq % c == 0 and tpad % c == 0), None)
    if tm_ln is not None:
        xp = _layernorm_prompt(x_prompt, meta_tokens.astype(x_prompt.dtype), ln_in_g, ln_in_b, tpad, tm_ln)
    else:
        meta = jnp.broadcast_to(meta_tokens[None].astype(x_prompt.dtype), (nb, N_META, d))
        xp = jnp.concatenate([meta, x_prompt, jnp.zeros((nb, tpad - t_real, d), x_prompt.dtype)], axis=1)
        xp = _layernorm_rows(xp.reshape(nb * tpad, d), ln_in_g, ln_in_b, KV_CHUNK).reshape(nb, tpad, d)
    xs = _layernorm_rows(x_sample.reshape(db * ds, d), ln_in_g, ln_in_b,
                         _pick_tile(db * ds, (256, 128, 8))).reshape(1, db * ds, d)

    w = _regroup_weights(w_in)
    w_o = w_out.astype(BF16)
    pad64 = lambda v: jnp.pad(v, ((0, 0), (0, 128 - v.shape[-1])))
    ck = cache_diff_k.transpose(0, 1, 3, 4, 5, 2).reshape(depth * n_pool, 256, PAGE_SIZE)
    cv = cache_diff_v.reshape(depth * n_pool, PAGE_SIZE * A_KV_HEADS, 128)
    cbk = cache_dsa_k.transpose(0, 1, 3, 4, 2).reshape(depth * n_pool, 128, PAGE_SIZE)
    cbv = cache_dsa_v.transpose(0, 1, 3, 4, 2).reshape(depth * n_pool, 128, PAGE_SIZE)
    ci = cache_idx_k.transpose(0, 1, 3, 2).reshape(depth * n_pool, IDX_DIM, PAGE_SIZE)
    pt_flat = page_table.reshape(-1).astype(I32)
    tm_p = _pick_tile(tpad, (512, 256, 128))
    tm_s = _pick_tile(db * ds, (256, 128, 8))

    rows_p = rows_s = None
    for layer in range(depth):
        lam_init = 0.8 - 0.6 * math.exp(-0.3 * layer)
        consts = jnp.stack([jnp.full((128,), lam_init, F32), jnp.full((128,), 1.0 - lam_init, F32),
                            jnp.zeros((128,), F32), jnp.zeros((128,), F32)])
        lam = jnp.concatenate([pad64(jnp.stack([lambda_q1[layer], lambda_k1[layer],
                                                 lambda_q2[layer], lambda_k2[layer]])), consts], axis=0)

        p16, p32, *rows_p = _project(xp, w, layer, depth, rows_p, t_real, tm_p, BF16, "proj_prompt")
        ya = _diff_prompt(p16, lam, t_real)
        yb = _dsa_prompt(p16, p32, t_real, topk_p)
        direct = layer == depth - 1 and seq % tm_p == 0
        xp = _merge(xp, ya, yb, p32, w_o, layer, lam, subln_g[layer], ln_g[layer], ln_b[layer],
                    alpha, tm_p, "merge_prompt", out_rows=seq if direct else None)

        p16s, p32s, *rows_s = _project(xs, w, layer, depth, rows_s, db * ds, tm_s, F32, "proj_sample")
        p16s3 = p16s.reshape(db, ds, W16)
        pt_layer = pt_flat + layer * n_pool
        ya_s = _diff_sample(p16s3, lam, ck, cv, pt_layer, n_pages)
        yb_s = _dsa_sample(p16s3, p32s.reshape(db, ds, W32), ci, cbk, cbv, pt_layer, n_pages, topk_s)
        xs = _merge(xs, ya_s.reshape(1, db * ds, 512), yb_s.reshape(1, db * ds, 512), p32s, w_o, layer, lam,
                    subln_g[layer], ln_g[layer], ln_b[layer], alpha, tm_s, "merge_sample")

    def cache_rows(stacks, lead, t):
        k, v, bk, bv, ik = stacks
        return (k.reshape(depth, -1, A_KV_HEADS, 2, HEAD_DIM, t).transpose(0, 1, 5, 2, 3, 4)
                 .reshape((depth,) + lead + (A_KV_HEADS, 2, HEAD_DIM)),
                v.reshape((depth,) + lead + (A_KV_HEADS, 2 * HEAD_DIM)),
                bk.reshape(depth, -1, B_KV_HEADS, HEAD_DIM, t).transpose(0, 1, 4, 2, 3)
                  .reshape((depth,) + lead + (B_KV_HEADS, HEAD_DIM)),
                bv.reshape(depth, -1, B_KV_HEADS, HEAD_DIM, t).transpose(0, 1, 4, 2, 3)
                  .reshape((depth,) + lead + (B_KV_HEADS, HEAD_DIM)),
                ik.transpose(0, 1, 3, 2).reshape((depth,) + lead + (IDX_DIM,)))

    y_prompt = xp if xp.shape[1] == seq else xp[:, N_META:t_real]
    return ((y_prompt, xs.reshape(db, ds, d))
            + cache_rows(rows_p, (nb, t_real), t_real) + cache_rows(rows_s, (db, ds), db * ds))
```

```python
import functools
import math

import jax
import jax.numpy as jnp
from jax import lax
from jax.experimental import pallas as pl
from jax.experimental.pallas import tpu as pltpu

N_META = 16
HEAD_DIM = 64
A_HEADS = 4
A_KV_HEADS = 2
B_HEADS = 8
B_KV_HEADS = 2
IDX_HEADS = 8
IDX_DIM = 64
IDX_SCALE = (IDX_HEADS * IDX_DIM) ** -0.5
TOPK_MAX = 256
PAGE_SIZE = 128
LN_EPS = 1e-5

Q_TILE = 128
DIFF_Q_TILE = 256
KV_CHUNK = 512
DSA_CHUNK = 512
PAGES_PER_STEP = 64
VMEM_LIMIT_BYTES = 56 * 1024 * 1024

LOG2E = math.log2(math.e)
INT_MIN = -(2 ** 31)
NEG_INF = float("-inf")

C_AQ, C_BQ, C_IQ, C_AK, C_AV, C_BK, C_BV, C_IK = 0, 512, 1024, 1536, 1792, 2048, 2176, 2304
W16 = 2432
C_AZ, C_BZ, C_IW = 0, 512, 1024
W32 = 1152
NCOL = W16 + W32

F32 = jnp.float32
BF16 = jnp.bfloat16
I32 = jnp.int32


def _dot_nt(a, b):
    return lax.dot_general(a, b, (((1,), (1,)), ((), ())), preferred_element_type=F32)


def _cparams(sem):
    return pltpu.CompilerParams(dimension_semantics=sem, vmem_limit_bytes=VMEM_LIMIT_BYTES)


def _sort_key(score):
    bits = lax.bitcast_convert_type(score, I32)
    return bits ^ ((bits >> 31) & 0x7FFFFFFF)


def _lane_lo(width=128):
    return lax.broadcasted_iota(I32, (1, width), 1) < 64


def _ln_kernel(x_ref, g_ref, b_ref, o_ref):
    x = x_ref[...]
    mu = jnp.mean(x, axis=-1, keepdims=True)
    xc = x - mu
    var = jnp.mean(xc * xc, axis=-1, keepdims=True)
    o_ref[...] = xc * lax.rsqrt(var + LN_EPS) * g_ref[...] + b_ref[...]


def _layernorm_rows(x, g, b, tm):
    rows, d = x.shape
    return pl.pallas_call(
        _ln_kernel,
        grid=(rows // tm,),
        in_specs=[pl.BlockSpec((tm, d), lambda i: (i, 0)),
                  pl.BlockSpec((1, d), lambda i: (0, 0)),
                  pl.BlockSpec((1, d), lambda i: (0, 0))],
        out_specs=pl.BlockSpec((tm, d), lambda i: (i, 0)),
        out_shape=jax.ShapeDtypeStruct((rows, d), F32),
        compiler_params=_cparams(("arbitrary",)),
        name="ln_in",
    )(x, g.reshape(1, d), b.reshape(1, d))


def _ln_prompt_kernel(meta_ref, prev_ref, cur_ref, g_ref, b_ref, o_ref, *, n_src_tiles):
    i = pl.program_id(1)
    top = jnp.where(i == 0, meta_ref[...], jnp.where(i <= n_src_tiles, prev_ref[...], 0.0))
    body = jnp.where(i < n_src_tiles, cur_ref[...][0:cur_ref.shape[0] - N_META], 0.0)
    x = jnp.concatenate([top, body], axis=0)
    mu = jnp.mean(x, axis=-1, keepdims=True)
    xc = x - mu
    var = jnp.mean(xc * xc, axis=-1, keepdims=True)
    o_ref[...] = xc * lax.rsqrt(var + LN_EPS) * g_ref[...] + b_ref[...]


def _layernorm_prompt(x_prompt, meta, g, b, tpad, tm):
    nb, seq, d = x_prompt.shape
    n_src = seq // tm
    per = tm // N_META
    return pl.pallas_call(
        functools.partial(_ln_prompt_kernel, n_src_tiles=n_src),
        grid=(nb, tpad // tm),
        in_specs=[pl.BlockSpec((N_META, d), lambda bi, i: (0, 0)),
                  pl.BlockSpec((None, N_META, d),
                               lambda bi, i: (bi, jnp.clip(i * per - 1, 0, seq // N_META - 1), 0)),
                  pl.BlockSpec((None, tm, d), lambda bi, i: (bi, jnp.minimum(i, n_src - 1), 0)),
                  pl.BlockSpec((1, d), lambda bi, i: (0, 0)),
                  pl.BlockSpec((1, d), lambda bi, i: (0, 0))],
        out_specs=pl.BlockSpec((None, tm, d), lambda bi, i: (bi, i, 0)),
        out_shape=jax.ShapeDtypeStruct((nb, tpad, d), F32),
        compiler_params=_cparams(("arbitrary", "arbitrary")),
        name="ln_in_prompt",
    )(meta, x_prompt, x_prompt, g.reshape(1, d), b.reshape(1, d))


def _proj_kernel(x_ref, w_ref, *rest, last_out_tile):
    p16_ref, p32_ref, ak_ref, av_ref, bk_ref, bv_ref, ik_ref = rest[-7:]
    xb = x_ref[...].astype(BF16)
    res = jnp.dot(xb, w_ref[...], preferred_element_type=F32)
    p16_ref[...] = res[:, :W16].astype(p16_ref.dtype)
    p32_ref[...] = res[:, W16:]

    @pl.when(pl.program_id(1) <= last_out_tile)
    def _():
        tm = res.shape[0]
        ak_ref[...] = res[:, C_AK:C_AK + 256].T
        for n in range(A_KV_HEADS):
            av_ref[pl.ds(n, tm, stride=A_KV_HEADS), :] = res[:, C_AV + 128 * n:C_AV + 128 * (n + 1)]
        bk_ref[...] = res[:, C_BK:C_BK + 128].T
        bv_ref[...] = res[:, C_BV:C_BV + 128].T
        ik_ref[...] = res[:, C_IK:C_IK + 128].T[0:IDX_DIM]


def _project(x, w, layer, depth, stacks, t_out, tm, p16_dtype, name):
    nb, t, d = x.shape
    last = pl.cdiv(t_out, tm) - 1
    row = lambda wd: pl.BlockSpec((None, tm, wd), lambda b, i: (b, i, 0))
    out = lambda ft: pl.BlockSpec((None, None, ft, tm), lambda b, i: (layer, b, 0, jnp.minimum(i, last)))
    shp = lambda ft: jax.ShapeDtypeStruct((depth, nb, ft, t_out), F32)
    out_v = pl.BlockSpec((None, None, A_KV_HEADS * tm, 128), lambda b, i: (layer, b, jnp.minimum(i, last), 0))
    shp_v = jax.ShapeDtypeStruct((depth, nb, A_KV_HEADS * t_out, 128), F32)
    stacks = list(stacks or ())
    return pl.pallas_call(
        functools.partial(_proj_kernel, last_out_tile=last),
        grid=(nb, t // tm),
        in_specs=[row(d), pl.BlockSpec((None, d, NCOL), lambda b, i: (layer, 0, 0))]
                 + [pl.BlockSpec(memory_space=pl.ANY)] * len(stacks),
        out_specs=[row(W16), row(W32), out(256), out_v, out(128), out(128), out(IDX_DIM)],
        out_shape=[jax.ShapeDtypeStruct((nb, t, W16), p16_dtype),
                   jax.ShapeDtypeStruct((nb, t, W32), F32),
                   shp(256), shp_v, shp(128), shp(128), shp(IDX_DIM)],
        input_output_aliases={2 + j: 2 + j for j in range(len(stacks))},
        compiler_params=_cparams(("arbitrary", "arbitrary")),
        name=name,
    )(x, w, *stacks)


def _diff_lambda(lam_ref):
    r = lam_ref[...]
    e1 = jnp.exp(jnp.sum(r[0:1] * r[1:2], axis=1, keepdims=True))
    e2 = jnp.exp(jnp.sum(r[2:3] * r[3:4], axis=1, keepdims=True))
    return e1 - e2 + r[4:5, 0:1]


def _half_select(x, keep_low):
    lo = _lane_lo()
    return jnp.where(lo, x, 0.0) if keep_low else jnp.where(lo, 0.0, x)


def _diff_query_rows(q, n_rows):
    pieces = [q[:, 0:128], q[:, 128:256]]
    return jnp.concatenate([_half_select(p, True) for p in pieces] +
                           [_half_select(p, False) for p in pieces], axis=0).astype(BF16)


def _dsa_query_rows(q):
    out = []
    for h in range(B_HEADS):
        n = h // (B_HEADS // B_KV_HEADS)
        p = q[:, 128 * (h // 2):128 * (h // 2) + 128]
        if h % 2 != n:
            p = pltpu.roll(p, 64, 1)
        out.append(_half_select(p, n == 0))
    return jnp.concatenate(out, axis=0).astype(BF16)


def _dsa_merge_heads(o, r):
    outs = []
    lo = _lane_lo()
    for j in range(B_HEADS // 2):
        n = j // 2
        oe, oo = o[2 * j * r:(2 * j + 1) * r], o[(2 * j + 1) * r:(2 * j + 2) * r]
        if n == 0:
            outs.append(jnp.where(lo, oe, pltpu.roll(oo, 64, 1)))
        else:
            outs.append(jnp.where(lo, pltpu.roll(oe, 64, 1), oo))
    return jnp.concatenate(outs, axis=1)


def _const_rows(values, r):
    return jnp.concatenate([jnp.full((r, 1), v, F32) for v in values], axis=0)


def _lane_fold(x, op):
    acc = x[:, 0:128]
    for c in range(1, x.shape[1] // 128):
        acc = op(acc, x[:, 128 * c:128 * (c + 1)])
    return acc


def _for_chunks_paired(n, body):
    def quad(jj, _):
        body(4 * jj, 0)
        body(4 * jj + 1, 1)
        body(4 * jj + 2, 0)
        body(4 * jj + 3, 1)
        return 0

    lax.fori_loop(0, n // 4, quad, 0)

    def single(j, _):
        body(j, 0)
        return 0

    lax.fori_loop((n // 4) * 4, n, single, 0)


def _store_scores(j, qk, shifts, tq, s_ref, m_ref):
    for r, shift in enumerate(shifts):
        rows = slice(r * tq, (r + 1) * tq)
        s = qk[rows] * LOG2E + shift
        s_ref[j, rows, :] = s
        m_ref[rows, :] = jnp.maximum(m_ref[rows, :], _lane_fold(s, jnp.maximum))


def _softmax_weigh(n_chunks, tc, rows, s_ref, v_ref, m_ref, acc_ref, lacc_ref):
    m = jnp.broadcast_to(jnp.max(m_ref[...], axis=1, keepdims=True), (rows, 128))
    acc_ref[...] = jnp.zeros_like(acc_ref)
    lacc_ref[...] = jnp.zeros_like(lacc_ref)

    def weigh(j):
        s = s_ref[j]
        p = jnp.concatenate([jnp.exp2(s[:, 128 * c:128 * (c + 1)] - m) for c in range(tc // 128)], axis=1)
        pv = jnp.dot(p.astype(BF16), v_ref[pl.ds(pl.multiple_of(j * tc, tc), tc), :], preferred_element_type=F32)
        return _lane_fold(p, jnp.add), pv

    def step(j, _slot):
        l, pv = weigh(j)
        lacc_ref[...] += l
        acc_ref[...] += pv

    _for_chunks_paired(n_chunks, step)
    return acc_ref[...] / jnp.sum(lacc_ref[...], axis=1, keepdims=True)


def _diff_prompt_kernel(lam_ref, q_ref, k_ref, v_ref, o_ref, s_ref, m_ref, acc_ref, lacc_ref, *, n_real_tiles):
    n = pl.program_id(1)
    i = pl.program_id(2)
    tq, tc = DIFF_Q_TILE, KV_CHUNK

    @pl.when(i >= n_real_tiles)
    def _():
        o_ref[...] = jnp.zeros_like(o_ref)

    @pl.when(i < n_real_tiles)
    def _():
        lam = _diff_lambda(lam_ref)
        qst = _diff_query_rows(q_ref[...].astype(F32), tq)
        slope_g = (jnp.where(n == 0, 2.0 ** -2, 2.0 ** -6), jnp.where(n == 0, 2.0 ** -4, 2.0 ** -8))
        qpos = i * tq + lax.broadcasted_iota(I32, (tq, 1), 0)
        col = lax.broadcasted_iota(I32, (1, tc), 1)
        m_ref[...] = jnp.full_like(m_ref, NEG_INF)

        def score_step(j, masked):
            k0 = pl.multiple_of(j * tc, tc)
            qk = _dot_nt(qst, k_ref[pl.ds(k0, tc), :])
            kpos = (col + k0).astype(F32)
            shifts = [(slope_g[r % 2] * LOG2E) * kpos for r in range(4)]
            if masked:
                causal = jnp.where(col + k0 <= qpos, 0.0, NEG_INF)
                shifts = [sh + causal for sh in shifts[:2]] * 2
            _store_scores(j, qk, shifts, tq, s_ref, m_ref)

        n_full = (i * tq) // tc
        _for_chunks_paired(n_full, lambda j, _slot: score_step(j, False))
        score_step(n_full, True)
        o = _softmax_weigh(n_full + 1, tc, 4 * tq, s_ref, v_ref, m_ref, acc_ref, lacc_ref)
        out_g0 = o[0:tq] - lam * o[2 * tq:3 * tq]
        out_g1 = o[tq:2 * tq] - lam * o[3 * tq:4 * tq]
        o_ref[...] = jnp.concatenate([out_g0, out_g1], axis=1)


def _diff_prompt(p16, lam, t_real):
    nb, tpad, _ = p16.shape
    tq = DIFF_Q_TILE
    kern = functools.partial(_diff_prompt_kernel, n_real_tiles=pl.cdiv(t_real, tq))
    return pl.pallas_call(
        kern,
        grid=(nb, A_KV_HEADS, tpad // tq),
        in_specs=[pl.BlockSpec((8, 128), lambda b, n, i: (0, 0)),
                  pl.BlockSpec((None, tq, 256), lambda b, n, i: (b, i, C_AQ // 256 + n)),
                  pl.BlockSpec((None, tpad, 128), lambda b, n, i: (b, 0, C_AK // 128 + n)),
                  pl.BlockSpec((None, tpad, 128), lambda b, n, i: (b, 0, C_AV // 128 + n))],
        out_specs=pl.BlockSpec((None, tq, 256), lambda b, n, i: (b, i, n)),
        out_shape=jax.ShapeDtypeStruct((nb, tpad, 512), F32),
        scratch_shapes=[pltpu.VMEM((tpad // KV_CHUNK, 4 * tq, KV_CHUNK), F32), pltpu.VMEM((4 * tq, 128), F32),
                        pltpu.VMEM((4 * tq, 128), F32), pltpu.VMEM((4 * tq, 128), F32)],
        compiler_params=_cparams(("arbitrary", "arbitrary", "arbitrary")),
        name="diff_prompt",
    )(lam, p16, p16, p16)


def _dsa_prompt_kernel(bq_ref, iq_ref, iw_ref, bk_ref, bv_ref, ik_ref, o_ref,
                       keys_ref, keys_t_ref, mask_ref, s_ref, m_ref, acc_ref, lacc_ref, *,
                       n_real_tiles, topk, idx_bits):
    i = pl.program_id(1)
    tq, tc = Q_TILE, DSA_CHUNK

    @pl.when(i >= n_real_tiles)
    def _():
        o_ref[...] = jnp.zeros_like(o_ref)

    @pl.when(i < n_real_tiles)
    def _():
        n_chunks = (i * tq) // tc + 1
        col = lax.broadcasted_iota(I32, (1, tc), 1)
        qpos = i * tq + lax.broadcasted_iota(I32, (tq, 1), 0)

        iq = iq_ref[...].astype(F32)
        iqst = jnp.concatenate(
            [_half_select(iq[:, 128 * (h // 2):128 * (h // 2) + 128], h % 2 == 0)
             for h in range(IDX_HEADS)], axis=0).astype(BF16)
        w = iw_ref[...]
        wcol = [jnp.broadcast_to(w[:, h:h + 1], (tq, 128)) for h in range(IDX_HEADS)]

        def index_step(j, _slot):
            k0 = pl.multiple_of(j * tc, tc)
            logits = _dot_nt(iqst, ik_ref[pl.ds(k0, tc), :])
            blocks = []
            for c in range(tc // 128):
                cols = slice(128 * c, 128 * (c + 1))
                acc = jnp.maximum(logits[0:tq, cols], 0.0) * wcol[0]
                for h in range(1, IDX_HEADS):
                    acc = acc + jnp.maximum(logits[h * tq:(h + 1) * tq, cols], 0.0) * wcol[h]
                blocks.append(acc)
            sc = jnp.concatenate(blocks, axis=1)
            key = jnp.where(col + k0 <= qpos, _sort_key(sc * IDX_SCALE), INT_MIN)
            keys_ref[j] = key
            keys_t_ref[j] = key.T

        _for_chunks_paired(n_chunks, index_step)

        kidx = lax.broadcasted_iota(I32, (tc, 1), 0)

        def count_over_chunks(pred):
            def body(j, acc):
                hit = jnp.where(pred(keys_t_ref[j], kidx + j * tc), 1.0, 0.0)
                parts = [hit[8 * r:8 * (r + 1)] for r in range(tc // 8)]
                while len(parts) > 1:
                    parts = [parts[r] + parts[r + 1] for r in range(0, len(parts), 2)]
                return acc + parts[0]
            acc = lax.fori_loop(0, n_chunks, body, jnp.zeros((8, tq), F32))
            return jnp.sum(acc, axis=0, keepdims=True)

        def bit_step(b, carry):
            t, n_ge = carry
            cand = t + lax.shift_left(jnp.int32(1), 31 - b)
            cnt = count_over_chunks(lambda kc, _: kc >= cand)
            return jnp.where(cnt >= topk, cand, t), jnp.where(cnt >= topk, cnt, n_ge)

        t, n_ge = lax.fori_loop(0, 32, bit_step, (jnp.full((1, tq), INT_MIN, I32), jnp.zeros((1, tq), F32)))
        thr_t = jnp.maximum(t, INT_MIN + 1)
        n_gt = count_over_chunks(lambda kc, _: kc > thr_t)
        need = topk - n_gt
        n_eq = jnp.where(t == INT_MIN, 0.0, n_ge - n_gt)
        excess = n_eq > need

        def tie_bound():
            def tie_step(b, jb):
                cand = jb + lax.shift_left(jnp.int32(1), idx_bits - 1 - b)
                cnt = count_over_chunks(lambda kc, idx: (kc == thr_t) & (idx < cand))
                return jnp.where(cnt < need, cand, jb)
            return lax.fori_loop(0, idx_bits, tie_step, jnp.zeros((1, tq), I32))

        any_excess = jnp.max(jnp.where(excess, 1.0, 0.0)) > 0.0
        big = jnp.full((1, tq), 2 ** 30, I32)
        bound_t = lax.cond(any_excess, lambda: jnp.where(excess, tie_bound(), big), lambda: big)
        to_col = lambda v: jnp.broadcast_to(v, (tq, tq)).T[:, 0:1]
        thr, bound = to_col(thr_t), to_col(bound_t)

        qst = _dsa_query_rows(bq_ref[...].astype(F32))

        m_ref[...] = jnp.full_like(m_ref, NEG_INF)

        def score_step(j, slot):
            k0 = pl.multiple_of(j * tc, tc)
            kc = keys_ref[j]
            sel = (kc > thr) | ((kc == thr) & (col + k0 <= bound))
            mask_ref[slot] = jnp.where(sel, 0.0, NEG_INF)
            qk = _dot_nt(qst, bk_ref[pl.ds(k0, tc), :])
            kpos = (col + k0).astype(F32)
            _store_scores(j, qk, [(2.0 ** -(h + 1) * LOG2E) * kpos + mask_ref[slot] for h in range(B_HEADS)],
                          tq, s_ref, m_ref)

        _for_chunks_paired(n_chunks, score_step)
        o = _softmax_weigh(n_chunks, tc, B_HEADS * tq, s_ref, bv_ref, m_ref, acc_ref, lacc_ref)
        o_ref[...] = _dsa_merge_heads(o, tq)


def _dsa_prompt(p16, p32, t_real, topk):
    nb, tpad, _ = p16.shape
    nq = tpad // Q_TILE
    kern = functools.partial(_dsa_prompt_kernel, n_real_tiles=pl.cdiv(t_real, Q_TILE), topk=float(topk),
                             idx_bits=max(1, int(tpad).bit_length()))
    whole = lambda c: pl.BlockSpec((None, tpad, 128), lambda b, i: (b, 0, c // 128))
    return pl.pallas_call(
        kern,
        grid=(nb, nq),
        in_specs=[pl.BlockSpec((None, Q_TILE, 512), lambda b, i: (b, i, C_BQ // 512)),
                  pl.BlockSpec((None, Q_TILE, 512), lambda b, i: (b, i, C_IQ // 512)),
                  pl.BlockSpec((None, Q_TILE, 128), lambda b, i: (b, i, C_IW // 128)),
                  whole(C_BK), whole(C_BV), whole(C_IK)],
        out_specs=pl.BlockSpec((None, Q_TILE, 512), lambda b, i: (b, i, 0)),
        out_shape=jax.ShapeDtypeStruct((nb, tpad, 512), F32),
        scratch_shapes=[pltpu.VMEM((tpad // DSA_CHUNK, Q_TILE, DSA_CHUNK), I32),
                        pltpu.VMEM((tpad // DSA_CHUNK, DSA_CHUNK, Q_TILE), I32),
                        pltpu.VMEM((2, Q_TILE, DSA_CHUNK), F32),
                        pltpu.VMEM((tpad // DSA_CHUNK, B_HEADS * Q_TILE, DSA_CHUNK), F32),
                        pltpu.VMEM((B_HEADS * Q_TILE, 128), F32), pltpu.VMEM((B_HEADS * Q_TILE, 128), F32),
                        pltpu.VMEM((B_HEADS * Q_TILE, 128), F32)],
        compiler_params=_cparams(("arbitrary", "arbitrary")),
        name="dsa_prompt",
    )(p16, p16, p32, p16, p16, p16)


def _merge_kernel(sc_ref, x_ref, ya_ref, yb_ref, az_ref, bz_ref, w_ref, sg_ref, g_ref, b_ref, o_ref, *carry,
                  alpha, drop_rows):
    one_minus_lam_init = sc_ref[5:6, 0:1]
    ya = ya_ref[...]
    parts = []
    for h in range(A_HEADS):
        o = ya[:, 128 * h:128 * h + 128]
        y = o * lax.rsqrt(jnp.mean(o * o, axis=-1, keepdims=True) + LN_EPS) * sg_ref[...]
        parts.append(y * one_minus_lam_init)
    az = az_ref[...]
    bz = bz_ref[...]
    ya_g = jnp.concatenate(parts, axis=1) * (az * (1.0 / (1.0 + jnp.exp(-az))))
    yb_g = yb_ref[...] * (bz * (1.0 / (1.0 + jnp.exp(-bz))))
    mix = jnp.concatenate([ya_g, yb_g], axis=1).astype(BF16)
    y = alpha * x_ref[...] + jnp.dot(mix, w_ref[...], preferred_element_type=F32)
    mu = jnp.mean(y, axis=-1, keepdims=True)
    yc = y - mu
    var = jnp.mean(yc * yc, axis=-1, keepdims=True)
    out = yc * lax.rsqrt(var + LN_EPS) * g_ref[...] + b_ref[...]
    if drop_rows == 0:
        o_ref[...] = out
    else:
        (tail_ref,) = carry

        @pl.when(pl.program_id(1) > 0)
        def _():
            o_ref[...] = jnp.concatenate([tail_ref[...], out[0:drop_rows]], axis=0)

        tail_ref[...] = out[drop_rows:]


def _merge(x, ya, yb, p32, w_out, layer, sc, sub_g, ln_g, ln_b, alpha, tm, name, out_rows=None):
    nb, t, d = x.shape
    row = lambda wd, c: pl.BlockSpec((None, tm, wd), lambda b, i: (b, i, c))
    vec = lambda wd: pl.BlockSpec((1, wd), lambda b, i: (0, 0))
    if out_rows is None:
        out_spec, out_shape, scratch, drop = row(d, 0), (nb, t, d), [], 0
    else:
        last = out_rows // tm - 1
        out_spec = pl.BlockSpec((None, tm, d), lambda b, i: (b, jnp.clip(i - 1, 0, last), 0))
        out_shape, scratch, drop = (nb, out_rows, d), [pltpu.VMEM((tm - N_META, d), F32)], N_META
    return pl.pallas_call(
        functools.partial(_merge_kernel, alpha=alpha, drop_rows=drop),
        grid=(nb, t // tm),
        in_specs=[pl.BlockSpec((8, 128), lambda b, i: (0, 0)),
                  row(d, 0), row(512, 0), row(512, 0), row(512, C_AZ // 512), row(512, C_BZ // 512),
                  pl.BlockSpec((None, d, d), lambda b, i: (layer, 0, 0)),
                  vec(128), vec(d), vec(d)],
        out_specs=out_spec,
        out_shape=jax.ShapeDtypeStruct(out_shape, F32),
        scratch_shapes=scratch,
        compiler_params=_cparams(("arbitrary", "arbitrary")),
        name=name,
    )(sc, x, ya, yb, p32, p32, w_out, sub_g.reshape(1, 128), ln_g.reshape(1, d), ln_b.reshape(1, d))


def _pad_rows(x, rows):
    return jnp.concatenate([x, jnp.zeros((rows - x.shape[0], x.shape[1]), x.dtype)], axis=0)


def _diff_sample_kernel(pt_ref, lam_ref, q_ref, kn_ref, vn_ref, *rest, n_pg, past):
    k_refs, v_refs = rest[:n_pg], rest[n_pg:2 * n_pg]
    o_ref, m_ref, l_ref, acc_ref = rest[2 * n_pg:]
    pg = pl.program_id(1)
    ds = q_ref.shape[0]
    rows = 2 * A_KV_HEADS * 2 * ds
    span = n_pg * PAGE_SIZE

    @pl.when(pg == 0)
    def _():
        m_ref[...] = jnp.full_like(m_ref, NEG_INF)
        l_ref[...] = jnp.zeros_like(l_ref)
        acc_ref[...] = jnp.zeros_like(acc_ref)

    q = q_ref[...]
    qst = jnp.concatenate([_diff_query_rows(q[:, 256 * n:256 * n + 256], ds) for n in range(A_KV_HEADS)], axis=0)
    slope = _const_rows([2.0 ** (-2 * (2 * n + g + 1)) for n in range(A_KV_HEADS) for _c in range(2)
                         for g in range(2)], ds)
    half = rows // A_KV_HEADS

    def attend(score, weigh, width, k0, mask):
        col = lax.broadcasted_iota(I32, (1, width), 1)
        s = jnp.concatenate([score(n, qst[half * n:half * (n + 1)]) for n in range(A_KV_HEADS)],
                            axis=0) + slope * col.astype(F32)
        if mask is not None:
            s = jnp.where(mask(col), s, NEG_INF)
        cj = slope * k0
        m, l, acc = m_ref[...][:, 0:1], l_ref[...][:, 0:1], acc_ref[...]
        mt = jnp.max(s, axis=1, keepdims=True)
        m_new = jnp.maximum(m, mt + cj)
        p = jnp.exp(s - (m_new - cj))
        alpha = jnp.exp(m - m_new)
        l = alpha * l + jnp.sum(p, axis=1, keepdims=True)
        pv = jnp.concatenate([weigh(n, p[half * n:half * (n + 1)].astype(BF16)) for n in range(A_KV_HEADS)],
                             axis=0)
        m_ref[...] = jnp.broadcast_to(m_new, m_ref.shape)
        l_ref[...] = jnp.broadcast_to(l, l_ref.shape)
        acc_ref[...] = alpha * acc + pv

    def past_score(n, q):
        kt = jnp.concatenate([r[128 * n:128 * n + 128, :] for r in k_refs], axis=1).astype(BF16)
        return jnp.dot(q, kt, preferred_element_type=F32)

    def past_weigh(n, p):
        v = jnp.concatenate([r[pl.ds(n, PAGE_SIZE, stride=A_KV_HEADS), :] for r in v_refs], axis=0).astype(BF16)
        return jnp.dot(p, v, preferred_element_type=F32)

    attend(past_score, past_weigh, span, (pg * span).astype(F32), None)

    @pl.when(pg == past // span - 1)
    def _():
        r = lax.broadcasted_iota(I32, (rows, 1), 0) & (ds - 1)
        kn = _pad_rows(kn_ref[...], 128).astype(BF16)
        vn = _pad_rows(vn_ref[...], 128).astype(BF16)
        attend(lambda n, q: _dot_nt(q, kn[:, 128 * n:128 * n + 128]),
               lambda n, p: jnp.dot(p, vn[:, 128 * n:128 * n + 128], preferred_element_type=F32),
               128, jnp.float32(past), lambda col: col <= r)
        lam = _diff_lambda(lam_ref)
        o = acc_ref[...] / l_ref[...][:, 0:1]
        outs = []
        for n in range(A_KV_HEADS):
            base = half * n
            for g in range(2):
                outs.append(o[base + g * ds:base + (g + 1) * ds]
                            - lam * o[base + (2 + g) * ds:base + (3 + g) * ds])
        o_ref[...] = jnp.concatenate(outs, axis=1)


def _diff_sample(p16s, lam, cache_k, cache_v, pt_flat, n_pages):
    db, ds, _ = p16s.shape
    n_pg = PAGES_PER_STEP
    rows = 2 * A_KV_HEADS * 2 * ds
    past = n_pages * PAGE_SIZE
    kern = functools.partial(_diff_sample_kernel, n_pg=n_pg, past=past)

    def page(r):
        return pl.BlockSpec((None, 256, PAGE_SIZE), lambda sb, pg, pt: (pt[sb * n_pages + pg * n_pg + r], 0, 0))

    new = lambda c: pl.BlockSpec((None, ds, 256), lambda sb, pg, pt: (sb, 0, c // 256))
    grid_spec = pltpu.PrefetchScalarGridSpec(
        num_scalar_prefetch=1,
        grid=(db, n_pages // n_pg),
        in_specs=[pl.BlockSpec((8, 128), lambda sb, pg, pt: (0, 0)),
                  pl.BlockSpec((None, ds, 512), lambda sb, pg, pt: (sb, 0, C_AQ // 512)),
                  new(C_AK), new(C_AV)] + [page(r) for r in range(n_pg)] * 2,
        out_specs=pl.BlockSpec((None, ds, 512), lambda sb, pg, pt: (sb, 0, 0)),
        scratch_shapes=[pltpu.VMEM((rows, 128), F32), pltpu.VMEM((rows, 128), F32),
                        pltpu.VMEM((rows, 128), F32)])
    return pl.pallas_call(
        kern, grid_spec=grid_spec,
        out_shape=jax.ShapeDtypeStruct((db, ds, 512), F32),
        compiler_params=_cparams(("arbitrary", "arbitrary")),
        name="diff_sample",
    )(pt_flat, lam, p16s, p16s, p16s, *([cache_k] * n_pg), *([cache_v] * n_pg))


def _dsa_sample_index_kernel(pt_ref, iq_ref, iw_ref, in_ref, *rest, n_pg, past, topk, idx_bits):
    i_refs = rest[:n_pg]
    keys_ref, keys_new_ref, thr_ref, bound_ref = rest[n_pg:]
    sb, g = pl.program_id(0), pl.program_id(1)
    ds = iq_ref.shape[0]
    span = n_pg * PAGE_SIZE
    n_groups = past // span
    cw = keys_ref.shape[2]
    n_sub = span // cw
    all_rows = keys_ref.shape[1]
    row0 = pl.multiple_of(sb * ds, ds)
    r_id = lax.broadcasted_iota(I32, (ds, 1), 0)

    def index_keys(ik, transposed):
        iq = iq_ref[...]
        heads = []
        for h in range(IDX_HEADS):
            p = iq[:, 128 * (h // 2):128 * (h // 2) + 128]
            if h % 2:
                p = pltpu.roll(p, 64, 1)
            heads.append(p[:, 0:64])
        iqst = jnp.concatenate(heads, axis=0).astype(BF16)
        if transposed:
            logits = jnp.dot(iqst, ik.astype(BF16), preferred_element_type=F32)
        else:
            logits = _dot_nt(iqst, ik.astype(BF16))
        w = iw_ref[...]
        sc = jnp.maximum(logits[0:ds], 0.0) * w[:, 0:1]
        for h in range(1, IDX_HEADS):
            sc = sc + jnp.maximum(logits[h * ds:(h + 1) * ds], 0.0) * w[:, h:h + 1]
        return _sort_key(sc * IDX_SCALE)

    ik = jnp.concatenate([r[...] for r in i_refs], axis=1)
    keys = index_keys(ik, True)
    for c in range(n_sub):
        keys_ref[g * n_sub + c, pl.ds(row0, ds), :] = keys[:, c * cw:(c + 1) * cw]

    @pl.when(g == n_groups - 1)
    def _():
        col = lax.broadcasted_iota(I32, (1, 128), 1)
        key_new = index_keys(_pad_rows(in_ref[...][:, 0:64], 128), False)
        keys_new_ref[pl.ds(row0, ds), :] = jnp.where(col <= r_id, key_new, INT_MIN)

    @pl.when((sb == pl.num_programs(0) - 1) & (g == n_groups - 1))
    def _():
        col = lax.broadcasted_iota(I32, (1, cw), 1)
        col_new = lax.broadcasted_iota(I32, (1, 128), 1) + past

        def count(pred):
            def body(j, acc):
                return acc + _lane_fold(jnp.where(pred(keys_ref[j], col + j * cw), 1.0, 0.0), jnp.add)
            acc = lax.fori_loop(0, keys_ref.shape[0], body, jnp.zeros((all_rows, 128), F32))
            acc = acc + jnp.where(pred(keys_new_ref[...], col_new), 1.0, 0.0)
            return jnp.sum(acc, axis=1, keepdims=True)

        def bit_step(b, t):
            cand = t + lax.shift_left(jnp.int32(1), 31 - b)
            return jnp.where(count(lambda kc, _: kc >= cand) >= topk, cand, t)

        t = lax.fori_loop(0, 32, bit_step, jnp.full((all_rows, 1), INT_MIN, I32))
        thr = jnp.maximum(t, INT_MIN + 1)
        need = topk - count(lambda kc, _: kc > thr)
        excess = count(lambda kc, _: kc == thr) > need

        def tie_bound():
            def tie_step(b, jb):
                cand = jb + lax.shift_left(jnp.int32(1), idx_bits - 1 - b)
                return jnp.where(count(lambda kc, ix: (kc == thr) & (ix < cand)) < need, cand, jb)
            return lax.fori_loop(0, idx_bits, tie_step, jnp.zeros((all_rows, 1), I32))

        big = jnp.full((all_rows, 1), 2 ** 30, I32)
        any_excess = jnp.max(jnp.where(excess, 1.0, 0.0)) > 0.0
        bound = lax.cond(any_excess, lambda: jnp.where(excess, tie_bound(), big), lambda: big)
        thr_ref[...] = jnp.broadcast_to(thr, thr_ref.shape)
        bound_ref[...] = jnp.broadcast_to(bound, bound_ref.shape)


def _dsa_sample_attend_kernel(pt_ref, bq_ref, kn_ref, vn_ref, keys_ref, keys_new_ref, thr_ref, bound_ref, *rest,
                              n_pg, past):
    k_refs, v_refs = rest[:n_pg], rest[n_pg:2 * n_pg]
    o_ref, m_ref, l_ref, acc_ref = rest[2 * n_pg:]
    g = pl.program_id(1)
    ds = bq_ref.shape[0]
    rows = B_HEADS * ds
    span = n_pg * PAGE_SIZE
    n_groups = past // span
    qst = _dsa_query_rows(bq_ref[...])
    slope = _const_rows([2.0 ** -(h + 1) for h in range(B_HEADS)], ds)

    def attend(qk, weigh, kc, k0):
        width = qk.shape[1]
        col = lax.broadcasted_iota(I32, (1, width), 1)
        thr, bound = thr_ref[...][:, 0:1], bound_ref[...][:, 0:1]
        sel = (kc > thr) | ((kc == thr) & (col + k0 <= bound))
        s = (qk + slope * col.astype(F32)).reshape(B_HEADS, ds, width)
        s = jnp.where(sel[None], s, NEG_INF).reshape(rows, width)
        cj = slope * k0.astype(F32)
        m, l, acc = m_ref[...][:, 0:1], l_ref[...][:, 0:1], acc_ref[...]
        mt = jnp.max(s, axis=1, keepdims=True)
        m_new = jnp.maximum(m, mt + cj)
        m_safe = jnp.where(m_new == NEG_INF, 0.0, m_new)
        p = jnp.exp(s - (m_safe - cj))
        alpha = jnp.exp(m - m_safe)
        m_ref[...] = jnp.broadcast_to(m_new, m_ref.shape)
        l_ref[...] = jnp.broadcast_to(alpha * l + jnp.sum(p, axis=1, keepdims=True), l_ref.shape)
        acc_ref[...] = alpha * acc + weigh(p.astype(BF16))

    @pl.when(g == 0)
    def _():
        m_ref[...] = jnp.full_like(m_ref, NEG_INF)
        l_ref[...] = jnp.zeros_like(l_ref)
        acc_ref[...] = jnp.zeros_like(acc_ref)

    kt = jnp.concatenate([r[...] for r in k_refs], axis=1).astype(BF16)
    vt = jnp.concatenate([r[...] for r in v_refs], axis=1).astype(BF16)
    kc = jnp.concatenate([keys_ref[c] for c in range(keys_ref.shape[0])], axis=1)
    attend(jnp.dot(qst, kt, preferred_element_type=F32), lambda p: _dot_nt(p, vt), kc, g * span)

    @pl.when(g == n_groups - 1)
    def _():
        kn = _pad_rows(kn_ref[...], 128).astype(BF16)
        vn = _pad_rows(vn_ref[...], 128).astype(BF16)
        attend(_dot_nt(qst, kn), lambda p: jnp.dot(p, vn, preferred_element_type=F32),
               keys_new_ref[...], jnp.int32(past))
        o_ref[...] = _dsa_merge_heads(acc_ref[...] / l_ref[...][:, 0:1], ds)


def _dsa_sample(p16s, p32s, cache_i, cache_k, cache_v, pt_flat, n_pages, topk):
    db, ds, _ = p16s.shape
    n_pg = PAGES_PER_STEP
    n_groups = n_pages // n_pg
    past = n_pages * PAGE_SIZE
    rows = B_HEADS * ds
    cw = min(DSA_CHUNK, n_pg * PAGE_SIZE)
    n_sub = n_pg * PAGE_SIZE // cw

    def page(r, feat):
        return pl.BlockSpec((None, feat, PAGE_SIZE), lambda sb, g, pt: (pt[sb * n_pages + g * n_pg + r], 0, 0))

    new = lambda wd, c: pl.BlockSpec((None, ds, wd), lambda sb, g, pt: (sb, 0, c // wd))
    whole = lambda shape: pl.BlockSpec(shape, lambda sb, g, pt: (0,) * len(shape))

    key_shapes = [(past // cw, db * ds, cw), (db * ds, 128), (db * ds, 128), (db * ds, 128)]
    keys, keys_new, thr, bound = pl.pallas_call(
        functools.partial(_dsa_sample_index_kernel, n_pg=n_pg, past=past, topk=float(topk),
                          idx_bits=int(past + 128).bit_length()),
        grid_spec=pltpu.PrefetchScalarGridSpec(
            num_scalar_prefetch=1,
            grid=(db, n_groups),
            in_specs=[new(512, C_IQ), new(128, C_IW), new(128, C_IK)] + [page(r, 64) for r in range(n_pg)],
            out_specs=[whole(s) for s in key_shapes]),
        out_shape=[jax.ShapeDtypeStruct(s, I32) for s in key_shapes],
        compiler_params=_cparams(("arbitrary", "arbitrary")),
        name="dsa_sample_index",
    )(pt_flat, p16s, p32s, p16s, *([cache_i] * n_pg))

    per_seq = pl.BlockSpec((ds, 128), lambda sb, g, pt: (sb, 0))
    return pl.pallas_call(
        functools.partial(_dsa_sample_attend_kernel, n_pg=n_pg, past=past),
        grid_spec=pltpu.PrefetchScalarGridSpec(
            num_scalar_prefetch=1,
            grid=(db, n_groups),
            in_specs=[new(512, C_BQ), new(128, C_BK), new(128, C_BV),
                      pl.BlockSpec((n_sub, ds, cw), lambda sb, g, pt: (g, sb, 0)), per_seq, per_seq, per_seq]
                     + [page(r, 128) for r in range(n_pg)] * 2,
            out_specs=pl.BlockSpec((None, ds, 512), lambda sb, g, pt: (sb, 0, 0)),
            scratch_shapes=[pltpu.VMEM((rows, 128), F32), pltpu.VMEM((rows, 128), F32),
                            pltpu.VMEM((rows, 128), F32)]),
        out_shape=jax.ShapeDtypeStruct((db, ds, 512), F32),
        compiler_params=_cparams(("arbitrary", "arbitrary")),
        name="dsa_sample_attend",
    )(pt_flat, p16s, p16s, p16s, keys, keys_new, thr, bound, *([cache_k] * n_pg), *([cache_v] * n_pg))


_W_IN = dict(aq=(0, 512), ak=(512, 256), av=(768, 256), az=(1024, 512), bq=(1536, 512), bk=(2048, 128),
             bv=(2176, 128), iq=(2304, 512), ik=(2816, 64), iw=(2880, 8), bz=(2888, 512))
_W_ORDER = ((("aq",), HEAD_DIM ** -0.5), (("bq",), HEAD_DIM ** -0.5), (("iq",), 1.0), (("ak",), 1.0), (("av",), 1.0),
            (("bk",), 1.0), (("bv",), 1.0), (("ik", "ik"), 1.0), (("az",), 1.0), (("bz",), 1.0), (("iw",), 1.0))


def _regroup_kernel(wt_ref, o_ref):
    dst = 0
    for names, scale in _W_ORDER:
        rows = jnp.concatenate([wt_ref[_W_IN[n][0]:_W_IN[n][0] + _W_IN[n][1], :] for n in names], axis=0)
        width = -(-rows.shape[0] // 128) * 128
        if rows.shape[0] < width:
            rows = jnp.concatenate([rows, jnp.zeros((width - rows.shape[0], rows.shape[1]), F32)], axis=0)
        o_ref[:, dst:dst + width] = (rows * scale).T.astype(BF16)
        dst += width


def _regroup_weights(w_in):
    depth, d, n_in = w_in.shape
    return pl.pallas_call(
        _regroup_kernel,
        grid=(depth,),
        in_specs=[pl.BlockSpec((None, n_in, d), lambda l: (l, 0, 0))],
        out_specs=pl.BlockSpec((None, d, NCOL), lambda l: (l, 0, 0)),
        out_shape=jax.ShapeDtypeStruct((depth, d, NCOL), BF16),
        compiler_params=_cparams(("arbitrary",)),
        name="regroup_w_in",
    )(w_in.transpose(0, 2, 1))


def _pick_tile(n, candidates):
    for c in candidates:
        if n % c == 0:
            return c
    return n


def kernel(x_prompt, x_sample, cache_diff_k, cache_diff_v, cache_dsa_k, cache_dsa_v, cache_idx_k, page_table,
           meta_tokens, ln_in_g, ln_in_b, w_in, w_out, lambda_q1, lambda_k1, lambda_q2, lambda_k2,
           subln_g, ln_g, ln_b):
    nb, seq, d = x_prompt.shape
    db, ds, _ = x_sample.shape
    depth = w_in.shape[0]
    n_pool = cache_diff_k.shape[1]
    n_pages = page_table.shape[1]
    t_real = seq + N_META
    tpad = -(-t_real // DSA_CHUNK) * DSA_CHUNK
    assert ds == 8 and n_pages % PAGES_PER_STEP == 0 and d == 1024
    alpha = (2 * depth) ** 0.25
    topk_p = min(TOPK_MAX, seq // 4)
    topk_s = min(TOPK_MAX, (n_pages * PAGE_SIZE + ds) // 4)

    tm_ln = next((c for c in (512, 256, 128, 64, 32) if seq % c == 0 and tpad % c == 0), None)
    if tm_ln is not None:
        xp = _layernorm_prompt(x_prompt, meta_tokens.astype(x_prompt.dtype), ln_in_g, ln_in_b, tpad, tm_ln)
    else:
        meta = jnp.broadcast_to(meta_tokens[None].astype(x_prompt.dtype), (nb, N_META, d))
        xp = jnp.concatenate([meta, x_prompt, jnp.zeros((nb, tpad - t_real, d), x_prompt.dtype)], axis=1)
        xp = _layernorm_rows(xp.reshape(nb * tpad, d), ln_in_g, ln_in_b, KV_CHUNK).reshape(nb, tpad, d)
    xs = _layernorm_rows(x_sample.reshape(db * ds, d), ln_in_g, ln_in_b,
                         _pick_tile(db * ds, (256, 128, 8))).reshape(1, db * ds, d)

    w = _regroup_weights(w_in)
    w_o = w_out.astype(BF16)
    pad64 = lambda v: jnp.pad(v, ((0, 0), (0, 128 - v.shape[-1])))
    ck = cache_diff_k.transpose(0, 1, 3, 4, 5, 2).reshape(depth * n_pool, 256, PAGE_SIZE)
    cv = cache_diff_v.reshape(depth * n_pool, PAGE_SIZE * A_KV_HEADS, 128)
    cbk = cache_dsa_k.transpose(0, 1, 3, 4, 2).reshape(depth * n_pool, 128, PAGE_SIZE)
    cbv = cache_dsa_v.transpose(0, 1, 3, 4, 2).reshape(depth * n_pool, 128, PAGE_SIZE)
    ci = cache_idx_k.transpose(0, 1, 3, 2).reshape(depth * n_pool, IDX_DIM, PAGE_SIZE)
    pt_flat = page_table.reshape(-1).astype(I32)
    tm_p = _pick_tile(tpad, (512, 256, 128))
    tm_s = _pick_tile(db * ds, (256, 128, 8))

    rows_p = rows_s = None
    for layer in range(depth):
        lam_init = 0.8 - 0.6 * math.exp(-0.3 * layer)
        consts = jnp.stack([jnp.full((128,), lam_init, F32), jnp.full((128,), 1.0 - lam_init, F32),
                            jnp.zeros((128,), F32), jnp.zeros((128,), F32)])
        lam = jnp.concatenate([pad64(jnp.stack([lambda_q1[layer], lambda_k1[layer],
                                                 lambda_q2[layer], lambda_k2[layer]])), consts], axis=0)

        p16, p32, *rows_p = _project(xp, w, layer, depth, rows_p, t_real, tm_p, BF16, "proj_prompt")
        ya = _diff_prompt(p16, lam, t_real)
        yb = _dsa_prompt(p16, p32, t_real, topk_p)
        direct = layer == depth - 1 and seq % tm_p == 0
        xp = _merge(xp, ya, yb, p32, w_o, layer, lam, subln_g[layer], ln_g[layer], ln_b[layer],
                    alpha, tm_p, "merge_prompt", out_rows=seq if direct else None)

        p16s, p32s, *rows_s = _project(xs, w, layer, depth, rows_s, db * ds, tm_s, F32, "proj_sample")
        p16s3 = p16s.reshape(db, ds, W16)
        pt_layer = pt_flat + layer * n_pool
        ya_s = _diff_sample(p16s3, lam, ck, cv, pt_layer, n_pages)
        yb_s = _dsa_sample(p16s3, p32s.reshape(db, ds, W32), ci, cbk, cbv, pt_layer, n_pages, topk_s)
        xs = _merge(xs, ya_s.reshape(1, db * ds, 512), yb_s.reshape(1, db * ds, 512), p32s, w_o, layer, lam,
                    subln_g[layer], ln_g[layer], ln_b[layer], alpha, tm_s, "merge_sample")

    def cache_rows(stacks, lead, t):
        k, v, bk, bv, ik = stacks
        return (k.reshape(depth, -1, A_KV_HEADS, 2, HEAD_DIM, t).transpose(0, 1, 5, 2, 3, 4)
                 .reshape((depth,) + lead + (A_KV_HEADS, 2, HEAD_DIM)),
                v.reshape((depth,) + lead + (A_KV_HEADS, 2 * HEAD_DIM)),
                bk.reshape(depth, -1, B_KV_HEADS, HEAD_DIM, t).transpose(0, 1, 4, 2, 3)
                  .reshape((depth,) + lead + (B_KV_HEADS, HEAD_DIM)),
                bv.reshape(depth, -1, B_KV_HEADS, HEAD_DIM, t).transpose(0, 1, 4, 2, 3)
                  .reshape((depth,) + lead + (B_KV_HEADS, HEAD_DIM)),
                ik.transpose(0, 1, 3, 2).reshape((depth,) + lead + (IDX_DIM,)))

    y_prompt = xp if xp.shape[1] == seq else xp[:, N_META:t_real]
    return ((y_prompt, xs.reshape(db, ds, d))
            + cache_rows(rows_p, (nb, t_real), t_real) + cache_rows(rows_s, (db, ds), db * ds))
```
